```python
import math
import jax, jax.numpy as jnp
from jax import lax
import numpy as np

D_MODEL = 1024
BATCH = 4
SEQ = 4096
DEPTH = 2
DEC_BATCH = 32
DEC_SEQ = 8
PAST_LEN = 8192
PAGE_SIZE = 128

RET_H = 4
RET_DK = 128
RET_DV = 128
RET_QK_W = RET_H * RET_DK
RET_W = RET_H * RET_DV
RET_CHUNK = 128
RET_GN_EPS = 1e-6
ROPE_BASE = 10000.0
FOX_H = 8
FOX_DH = 64
FOX_W = FOX_H * FOX_DH
FOX_QBLOCK = 128
RW_N = 64
RW_H = 8
RW_W = RW_H * RW_N
RW_DECAY_R = 64
RW_AAA_R = 64
RW_GATE_R = 128
RW_PROJ = 3 * RW_W + RW_DECAY_R + RW_AAA_R + RW_GATE_R
RW_SPLITS = (RW_W, 2 * RW_W, 3 * RW_W, 3 * RW_W + RW_DECAY_R, 3 * RW_W + RW_DECAY_R + RW_AAA_R)
RW_GN_EPS = 64e-5
N_BRANCH = 3
D_FF = 4 * D_MODEL
NORM_EPS = 1e-6
RET_COLS = 2 * RET_QK_W + 2 * RET_W
FOX_COLS = 3 * FOX_W + FOX_H
GATE_COLS = N_BRANCH * D_MODEL
IN_SPLITS = (RET_COLS, RET_COLS + FOX_COLS, RET_COLS + FOX_COLS + RW_PROJ)
IN_W = RET_COLS + FOX_COLS + RW_PROJ + GATE_COLS
F32 = jnp.float32

kernel_name = 'hybrid_ret_fox_rwkv7_adaln_step'


def rmsnorm(x, g):
    xf = x.astype(F32)
    y = xf * lax.rsqrt(jnp.mean(xf * xf, axis=-1, keepdims=True) + NORM_EPS)
    return (y * g.astype(F32)).astype(x.dtype)


def head_norm(x, eps):
    xf = x.astype(F32)
    xc = xf - jnp.mean(xf, axis=-1, keepdims=True)
    return xc * lax.rsqrt(jnp.mean(xc * xc, axis=-1, keepdims=True) + eps)


def rotary(x, pos):
    d = x.shape[-1]
    inv = 1.0 / (ROPE_BASE ** (jnp.arange(0, d, 2, dtype=F32) / d))
    ang = pos.astype(F32)[:, None] * inv[None, :]
    cos = jnp.cos(ang)[None, :, None, :]
    sin = jnp.sin(ang)[None, :, None, :]
    xf = x.astype(F32)
    x1, x2 = xf[..., : d // 2], xf[..., d // 2:]
    return jnp.concatenate([x1 * cos - x2 * sin, x1 * sin + x2 * cos], axis=-1)


def ret_log_gamma():
    return jnp.log1p(-jnp.exp2(-5.0 - jnp.arange(RET_H, dtype=F32)))


def retention_chunk(q, k, v, R):
    C = q.shape[1]
    lg = ret_log_gamma()
    t = jnp.arange(C, dtype=F32)
    diff = t[:, None] - t[None, :]
    causal = diff >= 0
    dmask = jnp.where(causal[None], jnp.exp(lg[:, None, None] * jnp.where(causal, diff, 0.0)[None]), 0.0)
    s = jnp.einsum('bqhd,bkhd->bhqk', q, k) * dmask[None]
    inner = jnp.einsum('bhqk,bkhe->bqhe', s, v)
    q_decay = jnp.exp(lg[None, :] * (t[:, None] + 1.0))
    cross = jnp.einsum('bqhd,bhde->bqhe', q, R) * q_decay[None, :, :, None]
    k_decay = jnp.exp(lg[None, :] * (C - 1.0 - t[:, None]))
    R_new = jnp.exp(lg * C)[None, :, None, None] * R + jnp.einsum('bkhd,kh,bkhe->bhde', k, k_decay, v)
    return inner + cross, R_new


def retention(q, k, v, R0):
    B, T = q.shape[:2]
    C = min(RET_CHUNK, T)
    n = T // C

    def split(a):
        return jnp.moveaxis(a.reshape((B, n, C) + a.shape[2:]), 1, 0)

    def step(R, xs):
        o, R = retention_chunk(xs[0], xs[1], xs[2], R)
        return R, o

    R, o = lax.scan(step, R0, (split(q), split(k), split(v)))
    return jnp.moveaxis(o, 0, 1).reshape((B, T) + o.shape[3:]), R


def fox_attention(q, k, v, cum_q, cum_k, q0):
    B, T, H, d = q.shape
    Sk = k.shape[1]
    blk = min(FOX_QBLOCK, T)
    nb = T // blk
    kf = k.astype(F32)
    vf = v.astype(F32)
    cum_kT = jnp.swapaxes(cum_k, 1, 2)
    kpos = jnp.arange(Sk)
    scale = d ** -0.5

    def block(args):
        qi, ci, s0 = args
        qpos = q0 + s0 + jnp.arange(blk)
        logits = jnp.einsum('bqhd,bkhd->bhqk', qi.astype(F32), kf) * scale
        logits = logits + jnp.swapaxes(ci, 1, 2)[..., None] - cum_kT[:, :, None, :]
        logits = jnp.where(kpos[None, :] <= qpos[:, None], logits, -jnp.inf)
        p = jax.nn.softmax(logits, axis=-1)
        return jnp.einsum('bhqk,bkhd->bqhd', p, vf)

    qb = jnp.moveaxis(q.reshape(B, nb, blk, H, d), 1, 0)
    cb = jnp.moveaxis(cum_q.reshape(B, nb, blk, H), 1, 0)
    o = lax.map(block, (qb, cb, jnp.arange(nb) * blk))
    return jnp.moveaxis(o, 0, 1).reshape(B, T, H, d)


def rwkv_scan(S0, r, w, k, v, kk, a):
    def step(S, xs):
        r_t, w_t, k_t, v_t, kk_t, a_t = xs
        Skk = jnp.einsum('bhij,bhj->bhi', S, kk_t)
        S = S * w_t[:, :, None, :] - Skk[..., None] * (kk_t * a_t)[:, :, None, :] + v_t[..., None] * k_t[:, :, None, :]
        return S, jnp.einsum('bhij,bhj->bhi', S, r_t)

    xs = tuple(jnp.moveaxis(t, 1, 0) for t in (r, w, k, v, kk, a))
    S, y = lax.scan(step, S0, xs)
    return jnp.moveaxis(y, 0, 1), S


def rwkv7_mix(p_rw, shift0, S0, W):
    B, T, _ = p_rw.shape
    pf = p_rw.astype(F32)
    prev = jnp.concatenate([shift0.astype(F32)[:, None], pf[:, :-1]], axis=1)
    xm = pf + (prev - pf) * W['rw_mu'].astype(F32)
    r, k, v, wl, al, gl = jnp.split(xm, RW_SPLITS, axis=-1)
    w_log = -jax.nn.softplus(-(W['rw_w0'] + jnp.tanh(wl) @ W['rw_w2'])) - 0.5
    decay = jnp.exp(-jnp.exp(w_log.astype(F32)))
    a = jax.nn.sigmoid(W['rw_a0'] + al @ W['rw_a2']).astype(F32)
    g = (jax.nn.sigmoid(gl) @ W['rw_g2']).astype(F32)

    def hd(t):
        return t.reshape(B, T, RW_H, RW_N)

    r, k, v, decay, a = hd(r), hd(k), hd(v), hd(decay), hd(a)
    kk = k * W['rw_kk'].astype(F32).reshape(RW_H, RW_N)
    kk = kk / jnp.maximum(jnp.linalg.norm(kk, axis=-1, keepdims=True), 1e-12)
    k = k * (1.0 + (a - 1.0) * W['rw_ka'].astype(F32).reshape(RW_H, RW_N))
    y, S = rwkv_scan(S0.astype(F32), r, decay, k, v, kk, a)
    y = head_norm(y, RW_GN_EPS).reshape(B, T, RW_W) * W['rw_lnw'] + W['rw_lnb']
    y = y + (jnp.sum(r * k * W['rw_rk'].astype(F32), axis=-1, keepdims=True) * v).reshape(B, T, RW_W)
    return y * g, S, p_rw[:, -1]


def layer_forward(x, c, q0, fox_past, ret0, wkv0, shift0, W):
    B, T, _ = x.shape
    mod = jnp.einsum('bd,de->be', jax.nn.silu(c), W['ada_w']) + W['ada_b']
    sh1, sc1, g1, sh2, sc2, g2 = jnp.split(mod[:, None, :], 6, axis=-1)
    h = rmsnorm(x, W['norm_mix']) * (1.0 + sc1) + sh1
    proj = jnp.einsum('btd,de->bte', h, W['w_in'])
    p_ret, p_fox, p_rw, p_gate = jnp.split(proj, IN_SPLITS, axis=-1)
    pos = q0 + jnp.arange(T)
    rq, rk, rv, rg = jnp.split(p_ret, (RET_QK_W, 2 * RET_QK_W, 2 * RET_QK_W + RET_W), axis=-1)
    rq = rotary(rq.reshape(B, T, RET_H, RET_DK), pos)
    rk = rotary(rk.reshape(B, T, RET_H, RET_DK), pos) * (RET_DK ** -0.5)
    o_ret, R = retention(rq, rk, rv.reshape(B, T, RET_H, RET_DV).astype(F32), ret0.astype(F32))
    o_ret = (head_norm(o_ret, RET_GN_EPS) * jax.nn.silu(rg.astype(F32)).reshape(B, T, RET_H, RET_DV)).reshape(B, T, RET_W)
    fq, fk, fv, ff = jnp.split(p_fox, (FOX_W, 2 * FOX_W, 3 * FOX_W), axis=-1)
    fq = fq.reshape(B, T, FOX_H, FOX_DH)
    fk = fk.reshape(B, T, FOX_H, FOX_DH)
    fv = fv.reshape(B, T, FOX_H, FOX_DH)
    logf = jax.nn.log_sigmoid((ff + W['fox_bf']).astype(F32))
    if fox_past is None:
        cum = jnp.cumsum(logf, axis=1)
        o_fox = fox_attention(fq, fk, fv, cum, cum, 0)
    else:
        k_past, v_past, logf_past = fox_past
        n_past = k_past.shape[1]
        cum = jnp.cumsum(jnp.concatenate([logf_past.astype(F32), logf], axis=1), axis=1)
        o_fox = fox_attention(fq, jnp.concatenate([k_past, fk], axis=1), jnp.concatenate([v_past, fv], axis=1), cum[:, n_past:], cum, n_past)
    o_fox = o_fox.reshape(B, T, FOX_W)
    o_rw, S, shift = rwkv7_mix(p_rw, shift0, wkv0, W)
    gates = jax.nn.sigmoid(p_gate.astype(F32)).reshape(B, T, N_BRANCH, D_MODEL)
    merged = (gates[:, :, 0] * jnp.einsum('bte,ed->btd', o_ret, W['w_br_ret'])
              + gates[:, :, 1] * jnp.einsum('bte,ed->btd', o_fox, W['w_br_fox'])
              + gates[:, :, 2] * jnp.einsum('bte,ed->btd', o_rw, W['w_br_rw']))
    mix_out = jnp.einsum('btd,de->bte', merged, W['w_out'])
    x = x + (g1 * mix_out).astype(x.dtype)
    h2 = rmsnorm(x, W['norm_ffn']) * (1.0 + sc2) + sh2
    u = jnp.einsum('btd,df->btf', h2, W['w_up'])
    ffn = jnp.einsum('btf,fd->btd', jnp.square(jax.nn.relu(u)), W['w_down'])
    x = x + (g2 * ffn).astype(x.dtype)
    dt = x.dtype
    return x, (fk.astype(dt), fv.astype(dt), logf.astype(dt), R.astype(dt), S.astype(dt), shift.astype(dt))


def setup_inputs(seed: int = 0) -> dict:
    key = jax.random.key(seed)
    keys = jax.random.split(key, 48)

    def nrm(i, shape, scale=1.0):
        return jax.random.normal(keys[i], shape, F32) * scale

    n_pages = PAST_LEN // PAGE_SIZE
    n_used = DEC_BATCH * n_pages
    n_pool = n_used + max(1, n_used // 4)
    page_table = jax.random.permutation(keys[7], n_pool)[:n_used].reshape(DEC_BATCH, n_pages).astype(jnp.int32)
    return {
        'x_prompt': nrm(0, (BATCH, SEQ, D_MODEL)),
        'x_sample': nrm(1, (DEC_BATCH, DEC_SEQ, D_MODEL)),
        'c_prompt': nrm(2, (BATCH, D_MODEL)),
        'c_sample': nrm(3, (DEC_BATCH, D_MODEL)),
        'cache_k': nrm(4, (DEPTH, n_pool, PAGE_SIZE, FOX_H, FOX_DH)),
        'cache_v': nrm(5, (DEPTH, n_pool, PAGE_SIZE, FOX_H, FOX_DH)),
        'cache_logf': jax.nn.log_sigmoid(3.0 + nrm(6, (DEPTH, n_pool, PAGE_SIZE, FOX_H))),
        'page_table': page_table,
        'state_ret': nrm(8, (DEPTH, DEC_BATCH, RET_H, RET_DK, RET_DV)),
        'state_wkv': nrm(9, (DEPTH, DEC_BATCH, RW_H, RW_N, RW_N)),
        'state_shift': nrm(10, (DEPTH, DEC_BATCH, RW_PROJ)),
        'ada_w': nrm(11, (DEPTH, D_MODEL, 6 * D_MODEL), 0.5 * D_MODEL ** -0.5),
        'ada_b': nrm(12, (DEPTH, 6 * D_MODEL), 0.01),
        'norm_mix': 1.0 + nrm(13, (DEPTH, D_MODEL), 0.02),
        'norm_ffn': 1.0 + nrm(14, (DEPTH, D_MODEL), 0.02),
        'w_in': nrm(15, (DEPTH, D_MODEL, IN_W), D_MODEL ** -0.5),
        'fox_bf': 3.0 + nrm(16, (DEPTH, FOX_H), 0.1),
        'rw_mu': jax.random.uniform(keys[17], (DEPTH, RW_PROJ), F32),
        'rw_w0': nrm(18, (DEPTH, RW_W), 0.5),
        'rw_w2': nrm(19, (DEPTH, RW_DECAY_R, RW_W), RW_DECAY_R ** -0.5),
        'rw_a0': nrm(20, (DEPTH, RW_W), 0.1),
        'rw_a2': nrm(21, (DEPTH, RW_AAA_R, RW_W), RW_AAA_R ** -0.5),
        'rw_g2': nrm(22, (DEPTH, RW_GATE_R, RW_W), RW_GATE_R ** -0.5),
        'rw_kk': 0.85 + nrm(23, (DEPTH, RW_W), 0.05),
        'rw_ka': 1.0 + nrm(24, (DEPTH, RW_W), 0.05),
        'rw_rk': nrm(25, (DEPTH, RW_H, RW_N), 0.1),
        'rw_lnw': 1.0 + nrm(26, (DEPTH, RW_W), 0.02),
        'rw_lnb': nrm(27, (DEPTH, RW_W), 0.01),
        'w_br_ret': nrm(28, (DEPTH, RET_W, D_MODEL), RET_W ** -0.5),
        'w_br_fox': nrm(29, (DEPTH, FOX_W, D_MODEL), FOX_W ** -0.5),
        'w_br_rw': nrm(30, (DEPTH, RW_W, D_MODEL), RW_W ** -0.5),
        'w_out': nrm(31, (DEPTH, D_MODEL, D_MODEL), D_MODEL ** -0.5),
        'w_up': nrm(32, (DEPTH, D_MODEL, D_FF), D_MODEL ** -0.5),
        'w_down': nrm(33, (DEPTH, D_FF, D_MODEL), D_FF ** -0.5),
        'norm_final': 1.0 + nrm(34, (D_MODEL,), 0.02),
    }


def reference(x_prompt, x_sample, c_prompt, c_sample, cache_k, cache_v, cache_logf, page_table,
              state_ret, state_wkv, state_shift, ada_w, ada_b, norm_mix, norm_ffn, w_in, fox_bf,
              rw_mu, rw_w0, rw_w2, rw_a0, rw_a2, rw_g2, rw_kk, rw_ka, rw_rk, rw_lnw, rw_lnb,
              w_br_ret, w_br_fox, w_br_rw, w_out, w_up, w_down, norm_final):
    past_len = page_table.shape[1] * cache_k.shape[2]
    B = x_prompt.shape[0]
    DB = x_sample.shape[0]
    xp, xs = x_prompt, x_sample
    st_p, st_s = [], []
    for l in range(DEPTH):
        W = {'ada_w': ada_w[l], 'ada_b': ada_b[l], 'norm_mix': norm_mix[l], 'norm_ffn': norm_ffn[l],
             'w_in': w_in[l], 'fox_bf': fox_bf[l], 'rw_mu': rw_mu[l], 'rw_w0': rw_w0[l], 'rw_w2': rw_w2[l],
             'rw_a0': rw_a0[l], 'rw_a2': rw_a2[l], 'rw_g2': rw_g2[l], 'rw_kk': rw_kk[l], 'rw_ka': rw_ka[l],
             'rw_rk': rw_rk[l], 'rw_lnw': rw_lnw[l], 'rw_lnb': rw_lnb[l], 'w_br_ret': w_br_ret[l],
             'w_br_fox': w_br_fox[l], 'w_br_rw': w_br_rw[l], 'w_out': w_out[l], 'w_up': w_up[l], 'w_down': w_down[l]}
        ret0 = jnp.zeros((B, RET_H, RET_DK, RET_DV), F32)
        wkv0 = jnp.zeros((B, RW_H, RW_N, RW_N), F32)
        shift0 = jnp.zeros((B, RW_PROJ), x_prompt.dtype)
        xp, sp = layer_forward(xp, c_prompt, 0, None, ret0, wkv0, shift0, W)
        st_p.append(sp)
        k_past = cache_k[l][page_table].reshape(DB, past_len, FOX_H, FOX_DH)
        v_past = cache_v[l][page_table].reshape(DB, past_len, FOX_H, FOX_DH)
        lf_past = cache_logf[l][page_table].reshape(DB, past_len, FOX_H)
        xs, ss = layer_forward(xs, c_sample, past_len, (k_past, v_past, lf_past), state_ret[l], state_wkv[l], state_shift[l], W)
        st_s.append(ss)
    y_prompt = rmsnorm(xp, norm_final)
    y_sample = rmsnorm(xs, norm_final)
    k_p = jnp.stack([s[0] for s in st_p])
    v_p = jnp.stack([s[1] for s in st_p])
    lf_p = jnp.stack([s[2] for s in st_p])
    ret_p = jnp.stack([s[3] for s in st_p])
    wkv_p = jnp.stack([s[4] for s in st_p])
    sh_p = jnp.stack([s[5] for s in st_p])
    k_s = jnp.stack([s[0] for s in st_s])
    v_s = jnp.stack([s[1] for s in st_s])
    lf_s = jnp.stack([s[2] for s in st_s])
    ret_s = jnp.stack([s[3] for s in st_s])
    wkv_s = jnp.stack([s[4] for s in st_s])
    sh_s = jnp.stack([s[5] for s in st_s])
    return (y_prompt, y_sample, k_p, v_p, lf_p, ret_p, wkv_p, sh_p, k_s, v_s, lf_s, ret_s, wkv_s, sh_s)
```

```python
import functools

import jax
import jax.numpy as jnp
from jax import lax
from jax.experimental import pallas as pl
from jax.experimental.pallas import tpu as pltpu

F32 = jnp.float32
BF16 = jnp.bfloat16
HIGHEST = lax.Precision.HIGHEST

RET_H, RET_D = 4, 128
FOX_H, FOX_DH = 8, 64
RW_H, RW_N = 8, 64
RW_W = RW_H * RW_N
RW_LORA = 128
RW_GATE_R = 128
RW_PROJ = 3 * RW_W + RW_LORA + RW_GATE_R
RET_GN_EPS = 1e-6
RW_GN_EPS = 64e-5
NORM_EPS = 1e-6
ROPE_BASE = 10000.0
NEG = -1e30

LANE = 128
C_RW = 0
C_F = 1792
C_GATE = 2048
C_FQ, C_FK, C_FV = 5120, 5632, 6144
C_RQ, C_RK, C_RV, C_RG = 6656, 7168, 7680, 8192
PROJ_W = 8704

RW_CHUNK = 64
RET_CHUNK = 128


def _cparams(sem, vmem_mb):
    return pltpu.CompilerParams(dimension_semantics=sem, vmem_limit_bytes=vmem_mb << 20)


def _dot(a, b):
    return jnp.dot(a, b, preferred_element_type=F32)


def _dot_nt(a, b):
    return lax.dot_general(a, b, (((1,), (1,)), ((), ())), preferred_element_type=F32)


def _dot_tn(a, b):
    return lax.dot_general(a, b, (((0,), (0,)), ((), ())), preferred_element_type=F32)


def _split3(x):
    hi = x.astype(BF16)
    r1 = x - hi.astype(F32)
    mid = r1.astype(BF16)
    lo = (r1 - mid.astype(F32)).astype(BF16)
    return hi, mid, lo


def _dot3_l(m01, x):
    hi, mid, lo = _split3(x)
    return _dot(m01, hi) + _dot(m01, mid) + _dot(m01, lo)


def _dot3_r(x, m01):
    hi, mid, lo = _split3(x)
    return _dot(hi, m01) + _dot(mid, m01) + _dot(lo, m01)


def _sigmoid(x):
    return 1.0 / (1.0 + jnp.exp(-x))


def _softplus(x):
    return jnp.maximum(x, 0.0) + jnp.log(1.0 + jnp.exp(-jnp.abs(x)))


def _pad_rows(a, rows):
    if a.shape[0] == rows:
        return a
    return jnp.concatenate([a, jnp.zeros((rows - a.shape[0],) + a.shape[1:], a.dtype)], axis=0)


def _mod_kernel(c_ref, w_ref, b_ref, o_ref):
    c = c_ref[...]
    s = c * _sigmoid(c)
    o_ref[...] = jnp.dot(s, w_ref[...], precision=HIGHEST, preferred_element_type=F32) + b_ref[...]


def _modulation(c_all, ada_w, ada_b):
    depth, d, n = ada_w.shape
    rows = c_all.shape[0]
    tn = 1536
    return pl.pallas_call(
        _mod_kernel,
        grid=(depth, n // tn),
        in_specs=[
            pl.BlockSpec((rows, d), lambda l, j: (0, 0)),
            pl.BlockSpec((None, d, tn), lambda l, j: (l, 0, j)),
            pl.BlockSpec((None, 1, tn), lambda l, j: (l, 0, j)),
        ],
        out_specs=pl.BlockSpec((None, rows, tn), lambda l, j: (l, 0, j)),
        out_shape=jax.ShapeDtypeStruct((depth, rows, n), F32),
        compiler_params=_cparams(("parallel", "parallel"), 40),
        name="ada_mod",
    )(c_all, ada_w, ada_b.reshape(depth, 1, n))


def _norm_mod(x, sc, sh, g):
    ms = jnp.mean(x * x, axis=-1, keepdims=True)
    y = x * lax.rsqrt(ms + NORM_EPS) * g
    return y * (1.0 + sc) + sh


def _norm_mm_kernel(x_ref, sc_ref, sh_ref, g_ref, w_ref, o_ref, h_ref):
    @pl.when(pl.program_id(2) == 0)
    def _():
        h_ref[...] = _norm_mod(x_ref[...], sc_ref[...], sh_ref[...], g_ref[...]).astype(BF16)

    o_ref[...] = _dot(h_ref[...], w_ref[...])


def _row_spec(rm, tm, d):
    if rm == 1:
        return pl.BlockSpec((None, 1, d), lambda g, i, *_: (g, 0, 0))
    return pl.BlockSpec((None, tm, d), lambda g, i, *_: (g, i, 0))


def _norm_mm(x, sc, sh, gamma, w):
    g_, r, d = x.shape
    n = w.shape[1]
    tm = min(r, 1024)
    tn = 512
    rm = sc.shape[1]
    return pl.pallas_call(
        _norm_mm_kernel,
        grid=(g_, r // tm, n // tn),
        in_specs=[
            pl.BlockSpec((None, tm, d), lambda g, i, j: (g, i, 0)),
            _row_spec(rm, tm, d),
            _row_spec(rm, tm, d),
            pl.BlockSpec((1, d), lambda g, i, j: (0, 0)),
            pl.BlockSpec((d, tn), lambda g, i, j: (0, j)),
        ],
        out_specs=pl.BlockSpec((None, tm, tn), lambda g, i, j: (g, i, j)),
        out_shape=jax.ShapeDtypeStruct((g_, r, n), F32),
        scratch_shapes=[pltpu.VMEM((tm, d), BF16)],
        compiler_params=_cparams(("parallel", "parallel", "arbitrary"), 40),
        name="norm_in_proj",
    )(x, sc, sh, gamma, w)


def _fox_gate_kernel(f_ref, b_ref, tri_ref, lf_ref, cum_ref, carry_ref):
    @pl.when(pl.program_id(1) == 0)
    def _():
        carry_ref[...] = jnp.zeros_like(carry_ref)

    z = f_ref[...] + b_ref[...]
    lf = -_softplus(-z)
    lf_ref[...] = lf
    cum = _dot3_l(tri_ref[...], lf) + carry_ref[...]
    cum_ref[...] = cum
    tb = cum.shape[0]
    carry_ref[...] = cum[tb - 1:tb, :]


def _fox_gate(proj, bf_pad):
    g_, r, _ = proj.shape
    tb = min(r, 512)
    tri = jnp.tril(jnp.ones((tb, tb), F32)).astype(BF16)
    return pl.pallas_call(
        _fox_gate_kernel,
        grid=(g_, r // tb),
        in_specs=[
            pl.BlockSpec((None, tb, LANE), lambda g, i: (g, i, C_F // LANE)),
            pl.BlockSpec((1, LANE), lambda g, i: (0, 0)),
            pl.BlockSpec((tb, tb), lambda g, i: (0, 0)),
        ],
        out_specs=[
            pl.BlockSpec((None, tb, LANE), lambda g, i: (g, i, 0)),
            pl.BlockSpec((None, tb, LANE), lambda g, i: (g, i, 0)),
        ],
        out_shape=[jax.ShapeDtypeStruct((g_, r, LANE), F32)] * 2,
        scratch_shapes=[pltpu.VMEM((1, LANE), F32)],
        compiler_params=_cparams(("parallel", "arbitrary"), 32),
        name="fox_gate",
    )(proj, bf_pad, tri)


def _ret_kernel(q_ref, k_ref, v_ref, g_ref, cos_ref, sin_ref, lg_ref, r0_ref,
                o_ref, rout_ref, r_sc, *, ce):
    cp = RET_CHUNK
    c = pl.program_id(2)

    @pl.when(c == 0)
    def _():
        r_sc[...] = r0_ref[...]

    cosf = _pad_rows(cos_ref[...], cp)
    sinf = _pad_rows(sin_ref[...], cp)

    def rot(x):
        return x * cosf + pltpu.roll(x, RET_D // 2, axis=1) * sinf

    q = rot(_pad_rows(q_ref[...], cp))
    k = rot(_pad_rows(k_ref[...], cp)) * (RET_D ** -0.5)
    v = _pad_rows(v_ref[...], cp)
    lg = lg_ref[...]
    lg1 = lg[:, :1]
    row = lax.broadcasted_iota(jnp.int32, (cp, RET_D), 0).astype(F32)
    qdec = jnp.exp(lg * (row + 1.0))
    kdec = jnp.exp(lg * (ce - 1.0 - row))
    ri = lax.broadcasted_iota(jnp.int32, (cp, cp), 0)
    ci = lax.broadcasted_iota(jnp.int32, (cp, cp), 1)
    diff = (ri - ci).astype(F32)
    dmask = jnp.where(diff >= 0.0, jnp.exp(lg1 * jnp.maximum(diff, 0.0)), 0.0)

    qb = q.astype(BF16)
    vb = v.astype(BF16)
    s = _dot_nt(qb, k.astype(BF16)) * dmask
    inner = _dot(s.astype(BF16), vb)
    r_old = r_sc[...]
    cross = _dot(qb, r_old.astype(BF16)) * qdec
    r_new = jnp.exp(lg1 * float(ce)) * r_old + _dot_tn((k * kdec).astype(BF16), vb)
    r_sc[...] = r_new

    o = inner + cross
    mu = jnp.mean(o, axis=-1, keepdims=True)
    oc = o - mu
    var = jnp.mean(oc * oc, axis=-1, keepdims=True)
    gg = _pad_rows(g_ref[...], cp)
    out = oc * lax.rsqrt(var + RET_GN_EPS) * (gg * _sigmoid(gg))
    o_ref[...] = out[:ce]

    @pl.when(c == pl.num_programs(2) - 1)
    def _():
        rout_ref[...] = r_new


def _retention(proj, cosf, sinf, lg_tab, r0):
    b, t, _ = proj.shape
    ce = min(RET_CHUNK, t)
    n = t // ce

    def col(c0):
        return pl.BlockSpec((None, ce, RET_D), lambda bi, h, c, c0=c0: (bi, c, c0 // RET_D + h))

    return pl.pallas_call(
        functools.partial(_ret_kernel, ce=ce),
        grid=(b, RET_H, n),
        in_specs=[
            col(C_RQ), col(C_RK), col(C_RV), col(C_RG),
            pl.BlockSpec((ce, RET_D), lambda bi, h, c: (c, 0)),
            pl.BlockSpec((ce, RET_D), lambda bi, h, c: (c, 0)),
            pl.BlockSpec((None, 1, RET_D), lambda bi, h, c: (h, 0, 0)),
            pl.BlockSpec((None, None, RET_D, RET_D), lambda bi, h, c: (bi, h, 0, 0)),
        ],
        out_specs=[
            pl.BlockSpec((None, ce, RET_D), lambda bi, h, c: (bi, c, h)),
            pl.BlockSpec((None, None, RET_D, RET_D), lambda bi, h, c: (bi, h, 0, 0)),
        ],
        out_shape=[
            jax.ShapeDtypeStruct((b, t, RET_H * RET_D), F32),
            jax.ShapeDtypeStruct((b, RET_H, RET_D, RET_D), F32),
        ],
        scratch_shapes=[pltpu.VMEM((RET_D, RET_D), F32)],
        compiler_params=_cparams(("parallel", "parallel", "arbitrary"), 32),
        name="retention",
    )(proj, proj, proj, proj, cosf, sinf, lg_tab, r0)


def _fox_kernel(q_ref, k_ref, v_ref, cq_ref, ck_ref, o_ref, m_sc, l_sc, acc_sc, *, tq, tk):
    qi = pl.program_id(2)
    ki = pl.program_id(3)
    lane = lax.broadcasted_iota(jnp.int32, (1, LANE), 1)
    lo = lane < FOX_DH

    @pl.when(ki == 0)
    def _():
        m_sc[...] = jnp.full_like(m_sc, NEG)
        l_sc[...] = jnp.zeros_like(l_sc)
        acc_sc[...] = jnp.zeros_like(acc_sc)

    @pl.when(ki <= qi)
    def _():
        q = q_ref[...] * (FOX_DH ** -0.5)
        kb = k_ref[...].astype(BF16)
        vb = v_ref[...].astype(BF16)
        row = lax.broadcasted_iota(jnp.int32, (tq, tk), 0) + qi * tq
        colk = lax.broadcasted_iota(jnp.int32, (tq, tk), 1) + ki * tk
        causal = colk <= row
        pv, al = [], []
        for h in range(2):
            qh = jnp.where(lo if h == 0 else jnp.logical_not(lo), q, 0.0).astype(BF16)
            s = _dot_nt(qh, kb) + cq_ref[h] - ck_ref[h]
            s = jnp.where(causal, s, NEG)
            m_prev = m_sc[h]
            m_new = jnp.maximum(m_prev, jnp.max(s, axis=-1, keepdims=True))
            alpha = jnp.exp(m_prev - m_new)
            p = jnp.exp(s - m_new)
            l_sc[h] = alpha * l_sc[h] + jnp.sum(p, axis=-1, keepdims=True)
            m_sc[h] = m_new
            pv.append(_dot(p.astype(BF16), vb))
            al.append(alpha)
        acc = acc_sc[...]
        acc_sc[...] = jnp.where(lo, al[0] * acc + pv[0], al[1] * acc + pv[1])

    @pl.when(ki == qi)
    def _():
        linv = jnp.where(lo, 1.0 / l_sc[0], 1.0 / l_sc[1])
        o_ref[...] = acc_sc[...] * linv


def _fox_prompt(proj, cum_c, cum_r):
    b, t, _ = proj.shape
    tq = tk = min(t, 512)
    nq = t // tq
    hp = FOX_H // 2
    return pl.pallas_call(
        functools.partial(_fox_kernel, tq=tq, tk=tk),
        grid=(b, hp, nq, nq),
        in_specs=[
            pl.BlockSpec((None, tq, LANE), lambda bi, p, qi, ki: (bi, qi, C_FQ // LANE + p)),
            pl.BlockSpec((None, tk, LANE), lambda bi, p, qi, ki: (bi, jnp.minimum(ki, qi), C_FK // LANE + p)),
            pl.BlockSpec((None, tk, LANE), lambda bi, p, qi, ki: (bi, jnp.minimum(ki, qi), C_FV // LANE + p)),
            pl.BlockSpec((None, 2, tq, 1), lambda bi, p, qi, ki: (bi, p, qi, 0)),
            pl.BlockSpec((None, 2, 1, tk), lambda bi, p, qi, ki: (bi, p, 0, jnp.minimum(ki, qi))),
        ],
        out_specs=pl.BlockSpec((None, tq, LANE), lambda bi, p, qi, ki: (bi, qi, p)),
        out_shape=jax.ShapeDtypeStruct((b, t, FOX_H * FOX_DH), F32),
        scratch_shapes=[
            pltpu.VMEM((2, tq, 1), F32),
            pltpu.VMEM((2, tq, 1), F32),
            pltpu.VMEM((tq, LANE), F32),
        ],
        compiler_params=_cparams(("parallel", "parallel", "parallel", "arbitrary"), 40),
        name="fox_prompt",
    )(proj, proj, proj, cum_c, cum_r)


def _fox_dec_kernel(pt_ref, q_ref, kn_ref, vn_ref, lfn_ref, u_ref, *rest, pps, ts):
    k_refs = rest[:pps]
    v_refs = rest[pps:2 * pps]
    lf_refs = rest[2 * pps:3 * pps]
    o_ref, qbd, m_sc, l_sc, acc_sc, carry_sc = rest[3 * pps:]
    j = pl.program_id(1)
    rows = FOX_H * ts
    width = FOX_H * FOX_DH
    page = k_refs[0].shape[0]
    assert ts & (ts - 1) == 0 and FOX_DH & (FOX_DH - 1) == 0
    rowh = lax.broadcasted_iota(jnp.int32, (rows, width), 0) >> (ts.bit_length() - 1)
    colh = lax.broadcasted_iota(jnp.int32, (rows, width), 1) >> (FOX_DH.bit_length() - 1)
    own = rowh == colh

    @pl.when(j == 0)
    def _():
        q = q_ref[...] * (FOX_DH ** -0.5)
        qt = jnp.broadcast_to(q[None], (FOX_H, ts, width)).reshape(rows, width)
        qbd[...] = jnp.where(own, qt, 0.0).astype(BF16)
        m_sc[...] = jnp.full_like(m_sc, NEG)
        l_sc[...] = jnp.zeros_like(l_sc)
        acc_sc[...] = jnp.zeros_like(acc_sc)
        carry_sc[...] = jnp.zeros_like(carry_sc)

    def expand(cum):
        return jnp.broadcast_to(cum[:, None, :], (FOX_H, ts, page)).reshape(rows, page)

    def update(kp, vp, cum, valid):
        s = _dot_nt(qbd[...], kp.astype(BF16)) - expand(cum)
        if valid is not None:
            s = jnp.where(valid, s, NEG)
        m_prev = m_sc[...]
        m_new = jnp.maximum(m_prev, jnp.max(s, axis=-1, keepdims=True))
        alpha = jnp.exp(m_prev - m_new)
        p = jnp.exp(s - m_new)
        l_sc[...] = alpha * l_sc[...] + jnp.sum(p, axis=-1, keepdims=True)
        m_sc[...] = m_new
        acc_sc[...] = alpha * acc_sc[...] + _dot(p.astype(BF16), vp.astype(BF16))

    for p_ in range(pps):
        cum = _dot3_r(lf_refs[p_][...], u_ref[...]) + carry_sc[...]
        carry_sc[...] = cum[:, page - 1:page]
        update(k_refs[p_][...], v_refs[p_][...], cum, None)

    @pl.when(j == pl.num_programs(1) - 1)
    def _():
        cum = _dot3_r(lfn_ref[...], u_ref[...]) + carry_sc[...]
        kn = _pad_rows(kn_ref[...], page)
        vn = _pad_rows(vn_ref[...], page)
        colk = lax.broadcasted_iota(jnp.int32, (rows, page), 1)
        trow = lax.broadcasted_iota(jnp.int32, (rows, page), 0) & (ts - 1)
        update(kn, vn, cum, colk <= trow)
        accn = acc_sc[...] / l_sc[...]
        o_ref[...] = jnp.sum(jnp.where(own, accn, 0.0).reshape(FOX_H, ts, width), axis=0)


def _fox_sample(layer, proj, lfn_t, page_table, cache_k, cache_v, cache_lf_t):
    db, ts, _ = proj.shape
    npages = page_table.shape[1]
    page = cache_k.shape[2]
    width = FOX_H * FOX_DH
    pps = 8 if npages % 8 == 0 else 1
    upper = jnp.triu(jnp.ones((page, page), F32)).astype(BF16)
    rows = FOX_H * ts

    def page_spec(shape, p_):
        return pl.BlockSpec((None, None) + shape,
                            lambda bi, j, pt, p_=p_: (layer, pt[bi, j * pps + p_], 0, 0))

    in_specs = [
        pl.BlockSpec((None, ts, width), lambda bi, j, pt: (bi, 0, C_FQ // width)),
        pl.BlockSpec((None, ts, width), lambda bi, j, pt: (bi, 0, C_FK // width)),
        pl.BlockSpec((None, ts, width), lambda bi, j, pt: (bi, 0, C_FV // width)),
        pl.BlockSpec((None, FOX_H, page), lambda bi, j, pt: (bi, 0, 0)),
        pl.BlockSpec((page, page), lambda bi, j, pt: (0, 0)),
    ]
    in_specs += [page_spec((page, width), p_) for p_ in range(pps)]
    in_specs += [page_spec((page, width), p_) for p_ in range(pps)]
    in_specs += [page_spec((FOX_H, page), p_) for p_ in range(pps)]
    grid_spec = pltpu.PrefetchScalarGridSpec(
        num_scalar_prefetch=1,
        grid=(db, npages // pps),
        in_specs=in_specs,
        out_specs=pl.BlockSpec((None, ts, width), lambda bi, j, pt: (bi, 0, 0)),
        scratch_shapes=[
            pltpu.VMEM((rows, width), BF16),
            pltpu.VMEM((rows, 1), F32),
            pltpu.VMEM((rows, 1), F32),
            pltpu.VMEM((rows, width), F32),
            pltpu.VMEM((FOX_H, 1), F32),
        ],
    )
    args = [proj, proj, proj, lfn_t, upper] + [cache_k] * pps + [cache_v] * pps + [cache_lf_t] * pps
    return pl.pallas_call(
        functools.partial(_fox_dec_kernel, pps=pps, ts=ts),
        grid_spec=grid_spec,
        out_shape=jax.ShapeDtypeStruct((db, ts, width), F32),
        compiler_params=_cparams(("parallel", "arbitrary"), 40),
        name="fox_sample",
    )(page_table, *args)


def _rw_prep_kernel(p_ref, sh0_ref, mu_ref, w2_ref, w0_ref, g2_ref, kk_ref, ka_ref, rk_ref,
                    e_ref, bl_ref, bt_ref,
                    kt_o, bt_o, kq_o, rt_o, v_o, gc_o, ab_o, g_o, bon_o, carry, *, tr, tv):
    i = pl.program_id(1)
    trb = p_ref.shape[0]

    @pl.when(i == 0)
    def _():
        carry[...] = sh0_ref[...]

    p = _pad_rows(p_ref[...], tr)
    rowi = lax.broadcasted_iota(jnp.int32, (tr, 1), 0)
    prev = jnp.where(rowi == 0, carry[...], pltpu.roll(p, 1, axis=0))
    carry[...] = p_ref[pl.ds(trb - 1, 1), :]
    xm = p + (prev - p) * mu_ref[...]
    r = xm[:, 0:RW_W]
    k = xm[:, RW_W:2 * RW_W]
    v = xm[:, 2 * RW_W:3 * RW_W]
    z = xm[:, 3 * RW_W:3 * RW_W + RW_LORA]
    gl = xm[:, 3 * RW_W + RW_LORA:]
    lane = lax.broadcasted_iota(jnp.int32, (1, RW_LORA), 1)
    zz = jnp.where(lane < RW_LORA // 2, jnp.tanh(z), z)
    pre = jnp.dot(zz, w2_ref[...], precision=HIGHEST, preferred_element_type=F32) + w0_ref[...]
    w_log = -_softplus(-pre[:, :RW_W]) - 0.5
    logdec = -jnp.exp(w_log)
    a = _sigmoid(pre[:, RW_W:])
    g = _dot(_sigmoid(gl).astype(BF16), g2_ref[...])
    e = e_ref[...]
    kk = k * kk_ref[...]
    kk = kk / jnp.maximum(jnp.sqrt(_dot3_r(kk * kk, e)), 1e-12)
    k2 = k * (1.0 + (a - 1.0) * ka_ref[...])
    bonus = _dot3_r(r * k2 * rk_ref[...], e) * v
    if tv < tr:
        valid = rowi < tv
        logdec = jnp.where(valid, logdec, 0.0)
        kk = jnp.where(valid, kk, 0.0)
        k2 = jnp.where(valid, k2, 0.0)
        v = jnp.where(valid, v, 0.0)
        r = jnp.where(valid, r, 0.0)
    cl = _dot3_l(bl_ref[...], logdec)
    ct = _dot3_l(bt_ref[...], logdec)
    e_inv = jnp.exp(-cl)
    kt = kk * jnp.exp(cl - logdec)
    bt = kk * a * e_inv
    kt_o[...] = kt
    bt_o[...] = bt
    kq_o[...] = k2 * e_inv
    rt_o[...] = r * jnp.exp(cl)
    v_o[...] = v
    gc_o[...] = jnp.exp(ct)
    g_o[...] = g
    bon_o[...] = bonus
    lane2 = lax.broadcasted_iota(jnp.int32, (1, LANE), 1)
    for c in range(tr // RW_CHUNK):
        rs = slice(c * RW_CHUNK, (c + 1) * RW_CHUNK)
        for hp in range(RW_H // 2):
            ls = slice(hp * LANE, (hp + 1) * LANE)
            x = kt[rs, ls]
            y = bt[rs, ls].astype(BF16)
            for h in range(2):
                mh = (lane2 < RW_N) if h == 0 else (lane2 >= RW_N)
                xh = jnp.where(mh, x, 0.0).astype(BF16)
                ab_o[2 * hp + h, rs, :] = _dot_nt(xh, y)


def _rw_prep(proj, shift0, wl, tv):
    b, t, _ = proj.shape
    tp = max(t, RW_CHUNK)
    tr = min(tp, 256)
    trb = min(t, tr)
    n = tp // tr
    cidx = jnp.arange(tr) // RW_CHUNK
    same = cidx[:, None] == cidx[None, :]
    blk_tri = jnp.logical_and(same, jnp.arange(tr)[None, :] <= jnp.arange(tr)[:, None]).astype(BF16)
    blk_all = same.astype(BF16)
    hidx = jnp.arange(RW_W) // RW_N
    e512 = (hidx[:, None] == hidx[None, :]).astype(BF16)

    def const(shape):
        return pl.BlockSpec(shape, lambda bi, i: (0,) * len(shape))

    tok = pl.BlockSpec((None, tr, RW_W), lambda bi, i: (bi, i, 0))
    outs = [jax.ShapeDtypeStruct((b, tp, RW_W), F32)] * 6
    return pl.pallas_call(
        functools.partial(_rw_prep_kernel, tr=tr, tv=tv if t < tr else tr),
        grid=(b, n),
        in_specs=[
            pl.BlockSpec((None, trb, RW_PROJ), lambda bi, i: (bi, i, C_RW // RW_PROJ)),
            pl.BlockSpec((None, 1, RW_PROJ), lambda bi, i: (bi, 0, 0)),
            const((1, RW_PROJ)), const((RW_LORA, 2 * RW_W)), const((1, 2 * RW_W)),
            const((RW_GATE_R, RW_W)), const((1, RW_W)), const((1, RW_W)), const((1, RW_W)),
            const((RW_W, RW_W)), const((tr, tr)), const((tr, tr)),
        ],
        out_specs=[tok] * 6 + [pl.BlockSpec((None, RW_H, tr, RW_N), lambda bi, i: (bi, 0, i, 0)), tok, tok],
        out_shape=outs + [jax.ShapeDtypeStruct((b, RW_H, tp, RW_N), F32)] + outs[:2],
        scratch_shapes=[pltpu.VMEM((1, RW_PROJ), F32)],
        compiler_params=_cparams(("parallel", "arbitrary"), 48),
        name="rwkv_prep",
    )(proj, shift0, wl["mu"], wl["w2a2"], wl["w0a0"], wl["g2"], wl["kk"], wl["ka"], wl["rk"],
      e512, blk_tri, blk_all)


def _rw_solve_kernel(a_ref, t_ref):
    c = a_ref.shape[0]
    chb = a_ref.shape[2]
    rows = lax.broadcasted_iota(jnp.int32, (c, chb), 0)

    def body_t(t, carry_):
        def body_u(u, acc):
            return acc - a_ref[t, pl.ds(u, 1), :] * t_ref[u]

        acc = lax.fori_loop(0, t, body_u, jnp.where(rows == t, 1.0, 0.0))
        t_ref[t] = acc
        return carry_

    lax.fori_loop(0, c, body_t, 0)


def _rw_solve(a_t):
    c, _, nch = a_t.shape
    chb = min(nch, 256)
    return pl.pallas_call(
        _rw_solve_kernel,
        grid=(nch // chb,),
        in_specs=[pl.BlockSpec((c, c, chb), lambda i: (0, 0, i))],
        out_specs=pl.BlockSpec((c, c, chb), lambda i: (0, 0, i)),
        out_shape=jax.ShapeDtypeStruct((c, c, nch), F32),
        compiler_params=_cparams(("parallel",), 40),
        name="rwkv_solve",
    )(a_t)


def _rw_chain_kernel(kt_ref, bt_ref, kq_ref, rt_ref, v_ref, gc_ref, tm_ref, g_ref, bon_ref,
                     lnw_ref, lnb_ref, e_ref, h0_ref, o_ref, hout_ref, h_sc, y_sc, *, tt, to):
    i = pl.program_id(2)
    cn = RW_CHUNK

    @pl.when(i == 0)
    def _():
        h_sc[...] = h0_ref[...]

    lane = lax.broadcasted_iota(jnp.int32, (1, LANE), 1)
    ri = lax.broadcasted_iota(jnp.int32, (cn, cn), 0)
    ci = lax.broadcasted_iota(jnp.int32, (cn, cn), 1)
    strict = ci < ri
    incl = ci <= ri
    rd = lax.broadcasted_iota(jnp.int32, (LANE, LANE), 0)
    cd = lax.broadcasted_iota(jnp.int32, (LANE, LANE), 1)
    eye = rd == cd

    def chunk(c, carry_):
        rs = pl.ds(pl.multiple_of(c * cn, cn), cn)
        kt = kt_ref[rs, :]
        bt = bt_ref[rs, :]
        kq = kq_ref[rs, :]
        rt = rt_ref[rs, :]
        v = v_ref[rs, :]
        gc = gc_ref[rs, :]
        btb = bt.astype(BF16)
        kqb = kq.astype(BF16)
        bh = bt * gc
        kh = kq * gc
        gm = jnp.where(eye, gc[0:1, :], 0.0)
        nm = jnp.zeros((LANE, LANE), F32)
        qe = rt
        yv = jnp.zeros((cn, LANE), F32)
        for h in range(2):
            mh = (lane < RW_N) if h == 0 else (lane >= RW_N)
            kth = jnp.where(mh, kt, 0.0).astype(BF16)
            rth = jnp.where(mh, rt, 0.0).astype(BF16)
            vh = jnp.where(mh, v, 0.0).astype(BF16)
            bhh = jnp.where(mh, bh, 0.0).astype(BF16)
            khh = jnp.where(mh, kh, 0.0).astype(BF16)
            ak = jnp.where(strict, _dot_nt(kth, kqb), 0.0).astype(BF16)
            bb = jnp.where(incl, _dot_nt(rth, btb), 0.0).astype(BF16)
            bk = jnp.where(incl, _dot_nt(rth, kqb), 0.0).astype(BF16)
            th = tm_ref[h, rs, :].astype(BF16)
            kd = _dot(th, kth).astype(BF16)
            vd = _dot(th, _dot(ak, vh).astype(BF16)).astype(BF16)
            gm = gm - _dot_tn(bhh, kd)
            nm = nm + _dot_tn(khh, vh) - _dot_tn(bhh, vd)
            qe = qe - _dot(bb, kd)
            yv = yv + _dot(bk, vh) - _dot(bb, vd)
        hcur = h_sc[...]
        hb = hcur.astype(BF16)
        y_sc[rs, :] = _dot(qe.astype(BF16), hb) + yv
        h_sc[...] = _dot(gm.astype(BF16), hb) + nm
        return carry_

    lax.fori_loop(0, tt // cn, chunk, 0)

    y = y_sc[...]
    e = e_ref[...]
    mean = _dot3_r(y, e) * (1.0 / RW_N)
    yc = y - mean
    var = _dot3_r(yc * yc, e) * (1.0 / RW_N)
    out = (yc * lax.rsqrt(var + RW_GN_EPS) * lnw_ref[...] + lnb_ref[...] + bon_ref[...]) * g_ref[...]
    o_ref[...] = out[:to]

    @pl.when(i == pl.num_programs(2) - 1)
    def _():
        hout_ref[...] = h_sc[...]


def _rw_chain(prep, tmat, h0, lnw, lnb, t_out):
    kt, bt, kq, rt, v, gc, _, g, bon = prep
    b, tp, _ = kt.shape
    tt = min(tp, 512)
    to = min(t_out, tt)
    hp = RW_H // 2
    hidx = jnp.arange(LANE) // RW_N
    e128 = (hidx[:, None] == hidx[None, :]).astype(BF16)
    tok = pl.BlockSpec((None, tt, LANE), lambda bi, p, i: (bi, i, p))
    vec = pl.BlockSpec((1, LANE), lambda bi, p, i: (0, p))
    st = pl.BlockSpec((None, None, LANE, LANE), lambda bi, p, i: (bi, p, 0, 0))
    return pl.pallas_call(
        functools.partial(_rw_chain_kernel, tt=tt, to=to),
        grid=(b, hp, tp // tt),
        in_specs=[tok] * 6 + [
            pl.BlockSpec((None, 2, tt, RW_N), lambda bi, p, i: (bi, p, i, 0)),
            tok, tok, vec, vec,
            pl.BlockSpec((LANE, LANE), lambda bi, p, i: (0, 0)),
            st,
        ],
        out_specs=[pl.BlockSpec((None, to, LANE), lambda bi, p, i: (bi, i, p)), st],
        out_shape=[
            jax.ShapeDtypeStruct((b, t_out, RW_W), F32),
            jax.ShapeDtypeStruct((b, hp, LANE, LANE), F32),
        ],
        scratch_shapes=[pltpu.VMEM((LANE, LANE), F32), pltpu.VMEM((tt, LANE), F32)],
        compiler_params=_cparams(("parallel", "parallel", "arbitrary"), 40),
        name="rwkv_chain",
    )(kt, bt, kq, rt, v, gc, tmat, g, bon, lnw, lnb, e128, h0)


def _rwkv(proj, shift0, s0, wl, t):
    b = proj.shape[0]
    prep = _rw_prep(proj, shift0, wl, t)
    ab = prep[6]
    tp = ab.shape[2]
    nc = tp // RW_CHUNK
    a_t = ab.reshape(b, RW_H, nc, RW_CHUNK, RW_CHUNK).transpose(3, 4, 0, 1, 2).reshape(RW_CHUNK, RW_CHUNK, -1)
    t_t = _rw_solve(a_t)
    tmat = t_t.reshape(RW_CHUNK, RW_CHUNK, b, RW_H, nc).transpose(2, 3, 4, 0, 1).reshape(b, RW_H, tp, RW_N)
    st = jnp.swapaxes(s0, -1, -2).reshape(b, RW_H // 2, 2, RW_N, RW_N)
    z = jnp.zeros_like(st[:, :, 0])
    h0 = jnp.concatenate([jnp.concatenate([st[:, :, 0], z], -1), jnp.concatenate([z, st[:, :, 1]], -1)], -2)
    o, hout = _rw_chain(prep, tmat, h0, wl["lnw"], wl["lnb"], t)
    s_new = jnp.stack([hout[:, :, :RW_N, :RW_N], hout[:, :, RW_N:, RW_N:]], axis=2)
    s_new = jnp.swapaxes(s_new.reshape(b, RW_H, RW_N, RW_N), -1, -2)
    return o, s_new


def _merge_kernel(x_ref, gm_ref, oa_ref, ob_ref, oc_ref, ga_ref, gb_ref, gc_ref,
                  wa_ref, wb_ref, wc_ref, wo_ref, o_ref):
    merged = (_sigmoid(ga_ref[...]) * _dot(oa_ref[...].astype(BF16), wa_ref[...])
              + _sigmoid(gb_ref[...]) * _dot(ob_ref[...].astype(BF16), wb_ref[...])
              + _sigmoid(gc_ref[...]) * _dot(oc_ref[...].astype(BF16), wc_ref[...]))
    mix = _dot(merged.astype(BF16), wo_ref[...])
    o_ref[...] = x_ref[...] + gm_ref[...] * mix


def _merge(x, g1, o_ret, o_fox, o_rw, proj, wl):
    g_, r, d = x.shape
    tm = min(r, 512)
    rm = g1.shape[1]
    w = o_ret.shape[-1]

    def tok(width, cb=0):
        return pl.BlockSpec((None, tm, width), lambda g, i, cb=cb: (g, i, cb))

    def const(shape):
        return pl.BlockSpec(shape, lambda g, i: (0, 0))

    return pl.pallas_call(
        _merge_kernel,
        grid=(g_, r // tm),
        in_specs=[
            tok(d), _row_spec(rm, tm, d), tok(w), tok(w), tok(w),
            tok(d, C_GATE // d), tok(d, C_GATE // d + 1), tok(d, C_GATE // d + 2),
            const((w, d)), const((w, d)), const((w, d)), const((d, d)),
        ],
        out_specs=tok(d),
        out_shape=jax.ShapeDtypeStruct((g_, r, d), F32),
        compiler_params=_cparams(("parallel", "parallel"), 48),
        name="merge_out",
    )(x, g1, o_ret, o_fox, o_rw, proj, proj, proj, wl["w_br_ret"], wl["w_br_fox"], wl["w_br_rw"], wl["w_out"])


def _ffn_kernel(x_ref, sc_ref, sh_ref, gm_ref, g_ref, wu_ref, wd_ref, nf_ref, o_ref, h_sc, acc_sc, *, final):
    f = pl.program_id(2)

    @pl.when(f == 0)
    def _():
        h_sc[...] = _norm_mod(x_ref[...], sc_ref[...], sh_ref[...], g_ref[...]).astype(BF16)
        acc_sc[...] = jnp.zeros_like(acc_sc)

    u = jnp.maximum(_dot(h_sc[...], wu_ref[...]), 0.0)
    acc_sc[...] += _dot((u * u).astype(BF16), wd_ref[...])

    @pl.when(f == pl.num_programs(2) - 1)
    def _():
        xn = x_ref[...] + gm_ref[...] * acc_sc[...]
        if final:
            ms = jnp.mean(xn * xn, axis=-1, keepdims=True)
            xn = xn * lax.rsqrt(ms + NORM_EPS) * nf_ref[...]
        o_ref[...] = xn


def _ffn(x, sc, sh, g2, gamma, w_up, w_down, norm_final, final):
    g_, r, d = x.shape
    dff = w_up.shape[1]
    tm = min(r, 1024)
    tf = 1024
    rm = sc.shape[1]
    return pl.pallas_call(
        functools.partial(_ffn_kernel, final=final),
        grid=(g_, r // tm, dff // tf),
        in_specs=[
            pl.BlockSpec((None, tm, d), lambda g, i, f: (g, i, 0)),
            _row_spec(rm, tm, d), _row_spec(rm, tm, d), _row_spec(rm, tm, d),
            pl.BlockSpec((1, d), lambda g, i, f: (0, 0)),
            pl.BlockSpec((d, tf), lambda g, i, f: (0, f)),
            pl.BlockSpec((tf, d), lambda g, i, f: (f, 0)),
            pl.BlockSpec((1, d), lambda g, i, f: (0, 0)),
        ],
        out_specs=pl.BlockSpec((None, tm, d), lambda g, i, f: (g, i, 0)),
        out_shape=jax.ShapeDtypeStruct((g_, r, d), F32),
        scratch_shapes=[pltpu.VMEM((tm, d), BF16), pltpu.VMEM((tm, d), F32)],
        compiler_params=_cparams(("parallel", "parallel", "arbitrary"), 48),
        name="ffn",
    )(x, sc, sh, g2, gamma, w_up, w_down, norm_final)


def _rope_tables(q0, t):
    d = RET_D
    inv = 1.0 / (ROPE_BASE ** (jnp.arange(0, d, 2, dtype=F32) / d))
    ang = (q0 + jnp.arange(t)).astype(F32)[:, None] * inv[None, :]
    cos, sin = jnp.cos(ang), jnp.sin(ang)
    return jnp.concatenate([cos, cos], -1), jnp.concatenate([-sin, sin], -1)


def _prep_layer_weights(l, w):
    w_in = w["w_in"][l]
    d = w_in.shape[0]
    ret_cols, fox_qkv = 2048, 1536
    f0 = ret_cols + fox_qkv
    rw0 = f0 + FOX_H
    g0 = rw0 + RW_PROJ
    zpad = jnp.zeros((d, LANE - FOX_H), F32)
    packed = jnp.concatenate([
        w_in[:, rw0:g0], w_in[:, f0:rw0], zpad, jnp.zeros((d, LANE), F32),
        w_in[:, g0:], w_in[:, ret_cols:f0], w_in[:, :ret_cols]], axis=1).astype(BF16)
    zl = jnp.zeros((RW_LORA // 2, RW_W), F32)
    w2a2 = jnp.concatenate([jnp.concatenate([w["rw_w2"][l], zl], 1),
                            jnp.concatenate([zl, w["rw_a2"][l]], 1)], 0)
    row = lambda a: a.reshape(1, -1)
    return {
        "w_in": packed,
        "norm_mix": row(w["norm_mix"][l]), "norm_ffn": row(w["norm_ffn"][l]),
        "bf": jnp.pad(w["fox_bf"][l], (0, LANE - FOX_H)).reshape(1, LANE),
        "mu": row(w["rw_mu"][l]), "w2a2": w2a2,
        "w0a0": jnp.concatenate([w["rw_w0"][l], w["rw_a0"][l]]).reshape(1, -1),
        "g2": w["rw_g2"][l].astype(BF16),
        "kk": row(w["rw_kk"][l]), "ka": row(w["rw_ka"][l]), "rk": row(w["rw_rk"][l]),
        "lnw": row(w["rw_lnw"][l]), "lnb": row(w["rw_lnb"][l]),
        "w_br_ret": w["w_br_ret"][l].astype(BF16), "w_br_fox": w["w_br_fox"][l].astype(BF16),
        "w_br_rw": w["w_br_rw"][l].astype(BF16), "w_out": w["w_out"][l].astype(BF16),
        "w_up": w["w_up"][l].astype(BF16), "w_down": w["w_down"][l].astype(BF16),
    }


def _ret_log_gamma_table():
    lg = jnp.log1p(-jnp.exp2(-5.0 - jnp.arange(RET_H, dtype=F32)))
    return jnp.broadcast_to(lg[:, None, None], (RET_H, 1, RET_D))


def _layer(x, mod, q0, past, ret0, wkv0, shift0, wl, norm_final, final, flat):
    b, t, d = x.shape
    sh1, sc1, g1, sh2, sc2, g2 = [m[:, None, :] for m in jnp.split(mod, 6, axis=-1)]
    if flat:
        rep = lambda m: jnp.repeat(m, t, axis=1).reshape(1, b * t, d)
        sh1, sc1, g1, sh2, sc2, g2 = [rep(m) for m in (sh1, sc1, g1, sh2, sc2, g2)]
        xd = x.reshape(1, b * t, d)
    else:
        xd = x
    proj_d = _norm_mm(xd, sc1, sh1, wl["norm_mix"], wl["w_in"])
    proj = proj_d.reshape(b, t, PROJ_W)

    cosf, sinf = _rope_tables(q0, t)
    o_ret, r_new = _retention(proj, cosf, sinf, _ret_log_gamma_table(), ret0)

    lf, cum = _fox_gate(proj_d, wl["bf"])
    lf = lf.reshape(b, t, LANE)[:, :, :FOX_H]
    if past is None:
        cum = cum.reshape(b, t, LANE)[:, :, :FOX_H]
        cum_t = jnp.swapaxes(cum, 1, 2)
        o_fox = _fox_prompt(proj, cum_t[..., None], cum_t[:, :, None, :])
    else:
        layer, page_table, cache_k, cache_v, cache_lf_t = past
        page = cache_k.shape[2]
        lfn_t = jnp.pad(jnp.swapaxes(lf, 1, 2), ((0, 0), (0, 0), (0, page - t)))
        o_fox = _fox_sample(layer, proj, lfn_t, page_table, cache_k, cache_v, cache_lf_t)

    o_rw, s_new = _rwkv(proj, shift0[:, None, :], wkv0, wl, t)

    dn = lambda a: a.reshape(xd.shape[0], xd.shape[1], -1)
    x1 = _merge(xd, g1, dn(o_ret), dn(o_fox), dn(o_rw), proj_d, wl)
    x2 = _ffn(x1, sc2, sh2, g2, wl["norm_ffn"], wl["w_up"], wl["w_down"], norm_final, final)
    fk = proj[:, :, C_FK:C_FV].reshape(b, t, FOX_H, FOX_DH)
    fv = proj[:, :, C_FV:C_RQ].reshape(b, t, FOX_H, FOX_DH)
    shift = proj[:, t - 1, C_RW:C_RW + RW_PROJ]
    return x2.reshape(b, t, d), (fk, fv, lf, r_new, s_new, shift)


def kernel(x_prompt, x_sample, c_prompt, c_sample, cache_k, cache_v, cache_logf, page_table, state_ret, state_wkv, state_shift, ada_w, ada_b, norm_mix, norm_ffn, w_in, fox_bf, rw_mu, rw_w0, rw_w2, rw_a0, rw_a2, rw_g2, rw_kk, rw_ka, rw_rk, rw_lnw, rw_lnb, w_br_ret, w_br_fox, w_br_rw, w_out, w_up, w_down, norm_final):
    depth = w_in.shape[0]
    b, t, d = x_prompt.shape
    db, ts, _ = x_sample.shape
    n_pool, page = cache_k.shape[1], cache_k.shape[2]
    past_len = page_table.shape[1] * page
    w = dict(w_in=w_in, norm_mix=norm_mix, norm_ffn=norm_ffn, fox_bf=fox_bf, rw_mu=rw_mu, rw_w0=rw_w0,
             rw_w2=rw_w2, rw_a0=rw_a0, rw_a2=rw_a2, rw_g2=rw_g2, rw_kk=rw_kk, rw_ka=rw_ka, rw_rk=rw_rk,
             rw_lnw=rw_lnw, rw_lnb=rw_lnb, w_br_ret=w_br_ret, w_br_fox=w_br_fox, w_br_rw=w_br_rw,
             w_out=w_out, w_up=w_up, w_down=w_down)
    rows = b + db
    rows_p = -(-rows // 8) * 8
    c_all = jnp.pad(jnp.concatenate([c_prompt, c_sample], 0), ((0, rows_p - rows), (0, 0)))
    mod = _modulation(c_all, ada_w, ada_b)
    ck = cache_k.reshape(depth, n_pool, page, FOX_H * FOX_DH)
    cv = cache_v.reshape(depth, n_pool, page, FOX_H * FOX_DH)
    clf_t = jnp.swapaxes(cache_logf, 2, 3)
    nf = norm_final.reshape(1, d)
    xp, xs = x_prompt, x_sample
    st_p, st_s = [], []
    for l in range(depth):
        wl = _prep_layer_weights(l, w)
        final = l == depth - 1
        xp, sp = _layer(xp, mod[l, :b], 0, None, jnp.zeros((b, RET_H, RET_D, RET_D), F32),
                        jnp.zeros((b, RW_H, RW_N, RW_N), F32), jnp.zeros((b, RW_PROJ), F32),
                        wl, nf, final, False)
        st_p.append(sp)
        xs, ss = _layer(xs, mod[l, b:rows], past_len, (l, page_table, ck, cv, clf_t), state_ret[l],
                        state_wkv[l], state_shift[l], wl, nf, final, True)
        st_s.append(ss)
    outs = [xp, xs]
    for st in (st_p, st_s):
        for i in range(6):
            outs.append(jnp.stack([s[i] for s in st]))
    y_p, y_s, k_p, v_p, lf_p, ret_p, wkv_p, sh_p, k_s, v_s, lf_s, ret_s, wkv_s, sh_s = outs
    return (y_p, y_s, k_p, v_p, lf_p, ret_p, wkv_p, sh_p, k_s, v_s, lf_s, ret_s, wkv_s, sh_s)
```

```python
import functools

import jax
import jax.numpy as jnp
from jax import lax
from jax.experimental import pallas as pl
from jax.experimental.pallas import tpu as pltpu

F32 = jnp.float32
BF16 = jnp.bfloat16
HIGHEST = lax.Precision.HIGHEST

RET_H, RET_D = 4, 128
FOX_H, FOX_DH = 8, 64
RW_H, RW_N = 8, 64
RW_W = RW_H * RW_N
RW_LORA = 128
RW_GATE_R = 128
RW_PROJ = 3 * RW_W + RW_LORA + RW_GATE_R
RET_GN_EPS = 1e-6
RW_GN_EPS = 64e-5
NORM_EPS = 1e-6
ROPE_BASE = 10000.0
NEG = -1e30
LOG2E = 1.4426950408889634

LANE = 128
C_RW = 0
C_F = 1792
C_GATE = 2048
C_FQ, C_FK, C_FV = 5120, 5632, 6144
C_RQ, C_RK, C_RV, C_RG = 6656, 7168, 7680, 8192
PROJ_W = 8704

RW_CHUNK = 64
RET_CHUNK = 128


def _cparams(sem, vmem_mb):
    return pltpu.CompilerParams(dimension_semantics=sem, vmem_limit_bytes=vmem_mb << 20)


def _dot(a, b):
    return jnp.dot(a, b, preferred_element_type=F32)


def _dot_nt(a, b):
    return lax.dot_general(a, b, (((1,), (1,)), ((), ())), preferred_element_type=F32)


def _dot_tn(a, b):
    return lax.dot_general(a, b, (((0,), (0,)), ((), ())), preferred_element_type=F32)


def _split3(x):
    hi = x.astype(BF16)
    r1 = x - hi.astype(F32)
    mid = r1.astype(BF16)
    lo = (r1 - mid.astype(F32)).astype(BF16)
    return hi, mid, lo


def _dot3_l(m01, x):
    hi, mid, lo = _split3(x)
    return _dot(m01, hi) + _dot(m01, mid) + _dot(m01, lo)


def _dot3_r(x, m01):
    hi, mid, lo = _split3(x)
    return _dot(hi, m01) + _dot(mid, m01) + _dot(lo, m01)


def _sigmoid(x):
    return 1.0 / (1.0 + jnp.exp(-x))


def _softplus(x):
    return jnp.maximum(x, 0.0) + jnp.log(1.0 + jnp.exp(-jnp.abs(x)))


def _pad_rows(a, rows):
    if a.shape[0] == rows:
        return a
    return jnp.concatenate([a, jnp.zeros((rows - a.shape[0],) + a.shape[1:], a.dtype)], axis=0)


def _mod_kernel(c_ref, w_ref, b_ref, o_ref):
    c = c_ref[...]
    s = c * _sigmoid(c)
    o_ref[...] = jnp.dot(s, w_ref[...], precision=HIGHEST, preferred_element_type=F32) + b_ref[...]


def _modulation(c_all, ada_w, ada_b):
    depth, d, n = ada_w.shape
    rows = c_all.shape[0]
    tn = 1536
    return pl.pallas_call(
        _mod_kernel,
        grid=(depth, n // tn),
        in_specs=[
            pl.BlockSpec((rows, d), lambda l, j: (0, 0)),
            pl.BlockSpec((None, d, tn), lambda l, j: (l, 0, j)),
            pl.BlockSpec((None, 1, tn), lambda l, j: (l, 0, j)),
        ],
        out_specs=pl.BlockSpec((None, rows, tn), lambda l, j: (l, 0, j)),
        out_shape=jax.ShapeDtypeStruct((depth, rows, n), F32),
        compiler_params=_cparams(("parallel", "parallel"), 40),
        name="ada_mod",
    )(c_all, ada_w, ada_b.reshape(depth, 1, n))


def _norm_mod(x, sc, sh, g):
    ms = jnp.mean(x * x, axis=-1, keepdims=True)
    y = x * lax.rsqrt(ms + NORM_EPS) * g
    return y * (1.0 + sc) + sh


def _norm_mm_kernel(x_ref, sc_ref, sh_ref, g_ref, w_ref, o_ref, h_ref):
    @pl.when(pl.program_id(2) == 0)
    def _():
        h_ref[...] = _norm_mod(x_ref[...], sc_ref[...], sh_ref[...], g_ref[...]).astype(BF16)

    o_ref[...] = _dot(h_ref[...], w_ref[...])


def _row_spec(rm, tm, d):
    if rm == 1:
        return pl.BlockSpec((None, 1, d), lambda g, i, *_: (g, 0, 0))
    return pl.BlockSpec((None, tm, d), lambda g, i, *_: (g, i, 0))


def _norm_mm(x, sc, sh, gamma, w):
    g_, r, d = x.shape
    n = w.shape[1]
    tm = min(r, 1024)
    tn = 512
    rm = sc.shape[1]
    return pl.pallas_call(
        _norm_mm_kernel,
        grid=(g_, r // tm, n // tn),
        in_specs=[
            pl.BlockSpec((None, tm, d), lambda g, i, j: (g, i, 0)),
            _row_spec(rm, tm, d),
            _row_spec(rm, tm, d),
            pl.BlockSpec((1, d), lambda g, i, j: (0, 0)),
            pl.BlockSpec((d, tn), lambda g, i, j: (0, j)),
        ],
        out_specs=pl.BlockSpec((None, tm, tn), lambda g, i, j: (g, i, j)),
        out_shape=jax.ShapeDtypeStruct((g_, r, n), F32),
        scratch_shapes=[pltpu.VMEM((tm, d), BF16)],
        compiler_params=_cparams(("parallel", "parallel", "arbitrary"), 40),
        name="norm_in_proj",
    )(x, sc, sh, gamma, w)


def _fox_gate_kernel(f_ref, b_ref, tri_ref, lf_ref, cum_ref, carry_ref):
    @pl.when(pl.program_id(1) == 0)
    def _():
        carry_ref[...] = jnp.zeros_like(carry_ref)

    z = f_ref[...] + b_ref[...]
    lf = -_softplus(-z)
    lf_ref[...] = lf
    cum = _dot3_l(tri_ref[...], lf) + carry_ref[...]
    cum_ref[...] = cum
    tb = cum.shape[0]
    carry_ref[...] = cum[tb - 1:tb, :]


def _fox_gate(proj, bf_pad):
    g_, r, _ = proj.shape
    tb = min(r, 512)
    tri = jnp.tril(jnp.ones((tb, tb), F32)).astype(BF16)
    return pl.pallas_call(
        _fox_gate_kernel,
        grid=(g_, r // tb),
        in_specs=[
            pl.BlockSpec((None, tb, LANE), lambda g, i: (g, i, C_F // LANE)),
            pl.BlockSpec((1, LANE), lambda g, i: (0, 0)),
            pl.BlockSpec((tb, tb), lambda g, i: (0, 0)),
        ],
        out_specs=[
            pl.BlockSpec((None, tb, LANE), lambda g, i: (g, i, 0)),
            pl.BlockSpec((None, tb, LANE), lambda g, i: (g, i, 0)),
        ],
        out_shape=[jax.ShapeDtypeStruct((g_, r, LANE), F32)] * 2,
        scratch_shapes=[pltpu.VMEM((1, LANE), F32)],
        compiler_params=_cparams(("parallel", "arbitrary"), 32),
        name="fox_gate",
    )(proj, bf_pad, tri)


def _ret_kernel(q_ref, k_ref, v_ref, g_ref, cos_ref, sin_ref, lg_ref, r0_ref,
                o_ref, rout_ref, r_sc, *, ce, nb):
    cp = RET_CHUNK
    c = pl.program_id(2)

    @pl.when(c == 0)
    def _():
        r_sc[...] = r0_ref[...]

    rows = max(nb * ce, cp)
    cosf = _pad_rows(cos_ref[...], rows)
    sinf = _pad_rows(sin_ref[...], rows)

    def rot(x):
        return x * cosf + pltpu.roll(x, RET_D // 2, axis=1) * sinf

    q = rot(_pad_rows(q_ref[...], rows)).astype(BF16)
    kf = rot(_pad_rows(k_ref[...], rows)) * (RET_D ** -0.5)
    vb = _pad_rows(v_ref[...], rows).astype(BF16)
    lg = lg_ref[...]
    lg1 = lg[:, :1]
    row = lax.broadcasted_iota(jnp.int32, (cp, RET_D), 0).astype(F32)
    qdec = jnp.exp(lg * (row + 1.0))
    kdec = jnp.exp(lg * (ce - 1.0 - row))
    ri = lax.broadcasted_iota(jnp.int32, (cp, cp), 0)
    ci = lax.broadcasted_iota(jnp.int32, (cp, cp), 1)
    diff = (ri - ci).astype(F32)
    dmask = jnp.where(diff >= 0.0, jnp.exp(lg1 * jnp.maximum(diff, 0.0)), 0.0)
    gdec = jnp.exp(lg1 * float(ce))

    r_cur = r_sc[...]
    outs = []
    for j in range(nb):
        rs = slice(j * cp, (j + 1) * cp)
        qb = q[rs]
        s = _dot_nt(qb, kf[rs].astype(BF16)) * dmask
        inner = _dot(s.astype(BF16), vb[rs])
        cross = _dot(qb, r_cur.astype(BF16)) * qdec
        r_cur = gdec * r_cur + _dot_tn((kf[rs] * kdec).astype(BF16), vb[rs])
        outs.append(inner + cross)
    r_sc[...] = r_cur
    o = outs[0] if nb == 1 else jnp.concatenate(outs, axis=0)
    mu = jnp.mean(o, axis=-1, keepdims=True)
    oc = o - mu
    var = jnp.mean(oc * oc, axis=-1, keepdims=True)
    gg = _pad_rows(g_ref[...], rows)
    out = oc * lax.rsqrt(var + RET_GN_EPS) * (gg * _sigmoid(gg))
    o_ref[...] = out[:nb * ce]

    @pl.when(c == pl.num_programs(2) - 1)
    def _():
        rout_ref[...] = r_cur


def _retention(proj, cosf, sinf, lg_tab, r0):
    b, t, _ = proj.shape
    ce = min(RET_CHUNK, t)
    n = t // ce
    nb = 4 if (ce == RET_CHUNK and n % 4 == 0) else 1
    tr = nb * ce

    def col(c0):
        return pl.BlockSpec((None, tr, RET_D), lambda bi, h, c, c0=c0: (bi, c, c0 // RET_D + h))

    return pl.pallas_call(
        functools.partial(_ret_kernel, ce=ce, nb=nb),
        grid=(b, RET_H, n // nb),
        in_specs=[
            col(C_RQ), col(C_RK), col(C_RV), col(C_RG),
            pl.BlockSpec((tr, RET_D), lambda bi, h, c: (c, 0)),
            pl.BlockSpec((tr, RET_D), lambda bi, h, c: (c, 0)),
            pl.BlockSpec((None, 1, RET_D), lambda bi, h, c: (h, 0, 0)),
            pl.BlockSpec((None, None, RET_D, RET_D), lambda bi, h, c: (bi, h, 0, 0)),
        ],
        out_specs=[
            pl.BlockSpec((None, tr, RET_D), lambda bi, h, c: (bi, c, h)),
            pl.BlockSpec((None, None, RET_D, RET_D), lambda bi, h, c: (bi, h, 0, 0)),
        ],
        out_shape=[
            jax.ShapeDtypeStruct((b, t, RET_H * RET_D), F32),
            jax.ShapeDtypeStruct((b, RET_H, RET_D, RET_D), F32),
        ],
        scratch_shapes=[pltpu.VMEM((RET_D, RET_D), F32)],
        compiler_params=_cparams(("parallel", "parallel", "arbitrary"), 32),
        name="retention",
    )(proj, proj, proj, proj, cosf, sinf, lg_tab, r0)


def _fox_kernel(q_ref, k_ref, v_ref, ck_ref, o_ref, qt_sc, m_sc, l_sc, acc_sc, *, tq, tk):
    qi = pl.program_id(2)
    ki = pl.program_id(3)
    head0 = lax.broadcasted_iota(jnp.int32, (LANE, 1), 0) < FOX_DH

    @pl.when(ki == 0)
    def _():
        q = q_ref[...] * (FOX_DH ** -0.5 * LOG2E)
        qt_sc[...] = q.T.astype(BF16)
        m_sc[...] = jnp.full_like(m_sc, NEG)
        l_sc[...] = jnp.zeros_like(l_sc)
        acc_sc[...] = jnp.zeros_like(acc_sc)

    def step(diagonal):
        kb = k_ref[...].astype(BF16)
        vt = v_ref[...].T.astype(BF16)
        qt = qt_sc[...]
        if diagonal:
            key = lax.broadcasted_iota(jnp.int32, (tk, tq), 0)
            qry = lax.broadcasted_iota(jnp.int32, (tk, tq), 1)
            causal = key <= qry
        for h in range(2):
            hs = slice(h * FOX_DH, (h + 1) * FOX_DH)
            qth = jnp.where(head0 if h == 0 else jnp.logical_not(head0), qt, jnp.zeros_like(qt))
            s = _dot(kb, qth) - ck_ref[h] * LOG2E
            if diagonal:
                s = jnp.where(causal, s, NEG)
            m_prev = m_sc[h:h + 1, :]
            m_new = jnp.maximum(m_prev, jnp.max(s, axis=0, keepdims=True))
            alpha = jnp.exp2(m_prev - m_new)
            p = jnp.exp2(s - m_new)
            l_sc[h:h + 1, :] = alpha * l_sc[h:h + 1, :] + jnp.sum(p, axis=0, keepdims=True)
            m_sc[h:h + 1, :] = m_new
            acc_sc[hs, :] = alpha * acc_sc[hs, :] + _dot(vt[hs, :], p.astype(BF16))

    @pl.when(ki < qi)
    def _():
        step(False)

    @pl.when(ki == qi)
    def _():
        step(True)
        linv = jnp.where(head0, 1.0 / l_sc[0:1, :], 1.0 / l_sc[1:2, :])
        o_ref[...] = (acc_sc[...] * linv).T


def _fox_prompt(proj, cum_c):
    b, t, _ = proj.shape
    tq = tk = min(t, 512)
    nq = t // tq
    hp = FOX_H // 2
    return pl.pallas_call(
        functools.partial(_fox_kernel, tq=tq, tk=tk),
        grid=(b, hp, nq, nq),
        in_specs=[
            pl.BlockSpec((None, tq, LANE), lambda bi, p, qi, ki: (bi, qi, C_FQ // LANE + p)),
            pl.BlockSpec((None, tk, LANE), lambda bi, p, qi, ki: (bi, jnp.minimum(ki, qi), C_FK // LANE + p)),
            pl.BlockSpec((None, tk, LANE), lambda bi, p, qi, ki: (bi, jnp.minimum(ki, qi), C_FV // LANE + p)),
            pl.BlockSpec((None, 2, tk, 1), lambda bi, p, qi, ki: (bi, p, jnp.minimum(ki, qi), 0)),
        ],
        out_specs=pl.BlockSpec((None, tq, LANE), lambda bi, p, qi, ki: (bi, qi, p)),
        out_shape=jax.ShapeDtypeStruct((b, t, FOX_H * FOX_DH), F32),
        scratch_shapes=[
            pltpu.VMEM((LANE, tq), BF16),
            pltpu.VMEM((2, tq), F32),
            pltpu.VMEM((2, tq), F32),
            pltpu.VMEM((LANE, tq), F32),
        ],
        compiler_params=_cparams(("parallel", "parallel", "parallel", "arbitrary"), 40),
        name="fox_prompt",
    )(proj, proj, proj, cum_c)


def _fox_dec_kernel(pt_ref, q_ref, kn_ref, vn_ref, lfn_ref, u_ref, pm_ref, *rest, pps, ts):
    k_refs = rest[:pps]
    v_refs = rest[pps:2 * pps]
    lf_refs = rest[2 * pps:3 * pps]
    o_ref, qbd, m_sc, l_sc, acc_sc, carry_sc = rest[3 * pps:]
    j = pl.program_id(1)
    rows = FOX_H * ts
    width = FOX_H * FOX_DH
    page = k_refs[0].shape[0]
    assert ts & (ts - 1) == 0 and FOX_DH & (FOX_DH - 1) == 0
    rowh = lax.broadcasted_iota(jnp.int32, (rows, width), 0) >> (ts.bit_length() - 1)
    colh = lax.broadcasted_iota(jnp.int32, (rows, width), 1) >> (FOX_DH.bit_length() - 1)
    own = rowh == colh

    @pl.when(j == 0)
    def _():
        q = q_ref[...] * (FOX_DH ** -0.5 * LOG2E)
        qt = jnp.broadcast_to(q[None], (FOX_H, ts, width)).reshape(rows, width)
        qbd[...] = jnp.where(own, qt, 0.0).astype(BF16)
        m_sc[...] = jnp.full_like(m_sc, NEG)
        l_sc[...] = jnp.zeros_like(l_sc)
        acc_sc[...] = jnp.zeros_like(acc_sc)
        carry_sc[...] = jnp.zeros_like(carry_sc)

    def expand(cum):
        return jnp.broadcast_to(cum[:, None, :], (FOX_H, ts, page)).reshape(rows, page)

    def update(k_list, v_list, cum_list, valid):
        qb = qbd[...]
        s = [_dot_nt(qb, kp.astype(BF16)) - expand(cm) * LOG2E for kp, cm in zip(k_list, cum_list)]
        s = s[0] if len(s) == 1 else jnp.concatenate(s, axis=1)
        if valid is not None:
            s = jnp.where(valid, s, NEG)
        m_prev = m_sc[...]
        m_new = jnp.maximum(m_prev, jnp.max(s, axis=-1, keepdims=True))
        alpha = jnp.exp2(m_prev - m_new)
        p = jnp.exp2(s - m_new)
        l_sc[...] = alpha * l_sc[...] + jnp.sum(p, axis=-1, keepdims=True)
        m_sc[...] = m_new
        pv = None
        for i, vp in enumerate(v_list):
            t_ = _dot(p[:, i * page:(i + 1) * page].astype(BF16), vp.astype(BF16))
            pv = t_ if pv is None else pv + t_
        acc_sc[...] = alpha * acc_sc[...] + pv

    lf_all = jnp.concatenate([r[...] for r in lf_refs], axis=0) if pps > 1 else lf_refs[0][...]
    cin = _dot3_r(lf_all, u_ref[...])
    carry = carry_sc[...]
    cum = cin + jnp.concatenate([carry] * pps, axis=0) if pps > 1 else cin + carry
    if pps > 1:
        cum = cum + _dot3_l(pm_ref[...], cin)[:, page - 1:page]
    carry = cum[(pps - 1) * FOX_H:pps * FOX_H, page - 1:page]
    carry_sc[...] = carry
    update([r[...] for r in k_refs], [r[...] for r in v_refs],
           [cum[i * FOX_H:(i + 1) * FOX_H] for i in range(pps)], None)

    @pl.when(j == pl.num_programs(1) - 1)
    def _():
        cumn = _dot3_r(lfn_ref[...], u_ref[...]) + carry
        kn = _pad_rows(kn_ref[...], page)
        vn = _pad_rows(vn_ref[...], page)
        colk = lax.broadcasted_iota(jnp.int32, (rows, page), 1)
        trow = lax.broadcasted_iota(jnp.int32, (rows, page), 0) & (ts - 1)
        update([kn], [vn], [cumn], colk <= trow)
        accn = acc_sc[...] / l_sc[...]
        o_ref[...] = jnp.sum(jnp.where(own, accn, 0.0).reshape(FOX_H, ts, width), axis=0)


def _fox_sample(layer, proj, lfn_t, page_table, cache_k, cache_v, cache_lf_t):
    db, ts, _ = proj.shape
    npages = page_table.shape[1]
    page = cache_k.shape[2]
    width = FOX_H * FOX_DH
    pps = 8 if npages % 8 == 0 else 1
    upper = jnp.triu(jnp.ones((page, page), F32)).astype(BF16)
    ridx = jnp.arange(pps * FOX_H)
    pmat = jnp.logical_and((ridx[:, None] % FOX_H) == (ridx[None, :] % FOX_H),
                           (ridx[None, :] // FOX_H) < (ridx[:, None] // FOX_H)).astype(BF16)
    rows = FOX_H * ts

    def page_spec(shape, p_):
        return pl.BlockSpec((None, None) + shape,
                            lambda bi, j, pt, p_=p_: (layer, pt[bi, j * pps + p_], 0, 0))

    in_specs = [
        pl.BlockSpec((None, ts, width), lambda bi, j, pt: (bi, 0, C_FQ // width)),
        pl.BlockSpec((None, ts, width), lambda bi, j, pt: (bi, 0, C_FK // width)),
        pl.BlockSpec((None, ts, width), lambda bi, j, pt: (bi, 0, C_FV // width)),
        pl.BlockSpec((None, FOX_H, page), lambda bi, j, pt: (bi, 0, 0)),
        pl.BlockSpec((page, page), lambda bi, j, pt: (0, 0)),
        pl.BlockSpec((pps * FOX_H, pps * FOX_H), lambda bi, j, pt: (0, 0)),
    ]
    in_specs += [page_spec((page, width), p_) for p_ in range(pps)]
    in_specs += [page_spec((page, width), p_) for p_ in range(pps)]
    in_specs += [page_spec((FOX_H, page), p_) for p_ in range(pps)]
    grid_spec = pltpu.PrefetchScalarGridSpec(
        num_scalar_prefetch=1,
        grid=(db, npages // pps),
        in_specs=in_specs,
        out_specs=pl.BlockSpec((None, ts, width), lambda bi, j, pt: (bi, 0, 0)),
        scratch_shapes=[
            pltpu.VMEM((rows, width), BF16),
            pltpu.VMEM((rows, 1), F32),
            pltpu.VMEM((rows, 1), F32),
            pltpu.VMEM((rows, width), F32),
            pltpu.VMEM((FOX_H, 1), F32),
        ],
    )
    args = [proj, proj, proj, lfn_t, upper, pmat] + [cache_k] * pps + [cache_v] * pps + [cache_lf_t] * pps
    return pl.pallas_call(
        functools.partial(_fox_dec_kernel, pps=pps, ts=ts),
        grid_spec=grid_spec,
        out_shape=jax.ShapeDtypeStruct((db, ts, width), F32),
        compiler_params=_cparams(("parallel", "arbitrary"), 40),
        name="fox_sample",
    )(page_table, *args)


def _rw_prep_kernel(p_ref, sh0_ref, mu_ref, w2_ref, w0_ref, g2_ref, kk_ref, ka_ref, rk_ref,
                    e_ref, bl_ref, bt_ref,
                    kt_o, bt_o, kq_o, rt_o, v_o, gc_o, ab_o, g_o, bon_o, carry, *, tr, tv):
    i = pl.program_id(1)
    trb = p_ref.shape[0]

    @pl.when(i == 0)
    def _():
        carry[...] = sh0_ref[...]

    p = _pad_rows(p_ref[...], tr)
    rowi = lax.broadcasted_iota(jnp.int32, (tr, 1), 0)
    prev = jnp.where(rowi == 0, carry[...], pltpu.roll(p, 1, axis=0))
    carry[...] = p_ref[pl.ds(trb - 1, 1), :]
    xm = p + (prev - p) * mu_ref[...]
    r = xm[:, 0:RW_W]
    k = xm[:, RW_W:2 * RW_W]
    v = xm[:, 2 * RW_W:3 * RW_W]
    z = xm[:, 3 * RW_W:3 * RW_W + RW_LORA]
    gl = xm[:, 3 * RW_W + RW_LORA:]
    lane = lax.broadcasted_iota(jnp.int32, (1, RW_LORA), 1)
    zz = jnp.where(lane < RW_LORA // 2, jnp.tanh(z), z)
    pre = jnp.dot(zz, w2_ref[...], precision=HIGHEST, preferred_element_type=F32) + w0_ref[...]
    w_log = -_softplus(-pre[:, :RW_W]) - 0.5
    logdec = -jnp.exp(w_log)
    a = _sigmoid(pre[:, RW_W:])
    g = _dot(_sigmoid(gl).astype(BF16), g2_ref[...])
    e = e_ref[...]
    kk = k * kk_ref[...]
    kk = kk / jnp.maximum(jnp.sqrt(_dot3_r(kk * kk, e)), 1e-12)
    k2 = k * (1.0 + (a - 1.0) * ka_ref[...])
    bonus = _dot3_r(r * k2 * rk_ref[...], e) * v
    if tv < tr:
        valid = rowi < tv
        logdec = jnp.where(valid, logdec, 0.0)
        kk = jnp.where(valid, kk, 0.0)
        k2 = jnp.where(valid, k2, 0.0)
        v = jnp.where(valid, v, 0.0)
        r = jnp.where(valid, r, 0.0)
    cl = _dot3_l(bl_ref[...], logdec)
    ct = _dot3_l(bt_ref[...], logdec)
    e_inv = jnp.exp(-cl)
    kt = kk * jnp.exp(cl - logdec)
    bt = kk * a * e_inv
    kt_o[...] = kt
    bt_o[...] = bt
    kq_o[...] = k2 * e_inv
    rt_o[...] = r * jnp.exp(cl)
    v_o[...] = v
    gc_o[...] = jnp.exp(ct)
    g_o[...] = g
    bon_o[...] = bonus
    lane2 = lax.broadcasted_iota(jnp.int32, (1, LANE), 1)
    for c in range(tr // RW_CHUNK):
        rs = slice(c * RW_CHUNK, (c + 1) * RW_CHUNK)
        for hp in range(RW_H // 2):
            ls = slice(hp * LANE, (hp + 1) * LANE)
            x = kt[rs, ls]
            y = bt[rs, ls].astype(BF16)
            for h in range(2):
                mh = (lane2 < RW_N) if h == 0 else (lane2 >= RW_N)
                xh = jnp.where(mh, x, 0.0).astype(BF16)
                ab_o[2 * hp + h, rs, :] = _dot_nt(xh, y)


def _rw_prep(proj, shift0, wl, tv):
    b, t, _ = proj.shape
    tp = max(t, RW_CHUNK)
    tr = min(tp, 256)
    trb = min(t, tr)
    n = tp // tr
    cidx = jnp.arange(tr) // RW_CHUNK
    same = cidx[:, None] == cidx[None, :]
    blk_tri = jnp.logical_and(same, jnp.arange(tr)[None, :] <= jnp.arange(tr)[:, None]).astype(BF16)
    blk_all = same.astype(BF16)
    hidx = jnp.arange(RW_W) // RW_N
    e512 = (hidx[:, None] == hidx[None, :]).astype(BF16)

    def const(shape):
        return pl.BlockSpec(shape, lambda bi, i: (0,) * len(shape))

    tok = pl.BlockSpec((None, tr, RW_W), lambda bi, i: (bi, i, 0))
    outs = [jax.ShapeDtypeStruct((b, tp, RW_W), F32)] * 6
    return pl.pallas_call(
        functools.partial(_rw_prep_kernel, tr=tr, tv=tv if t < tr else tr),
        grid=(b, n),
        in_specs=[
            pl.BlockSpec((None, trb, RW_PROJ), lambda bi, i: (bi, i, C_RW // RW_PROJ)),
            pl.BlockSpec((None, 1, RW_PROJ), lambda bi, i: (bi, 0, 0)),
            const((1, RW_PROJ)), const((RW_LORA, 2 * RW_W)), const((1, 2 * RW_W)),
            const((RW_GATE_R, RW_W)), const((1, RW_W)), const((1, RW_W)), const((1, RW_W)),
            const((RW_W, RW_W)), const((tr, tr)), const((tr, tr)),
        ],
        out_specs=[tok] * 6 + [pl.BlockSpec((None, RW_H, tr, RW_N), lambda bi, i: (bi, 0, i, 0)), tok, tok],
        out_shape=outs + [jax.ShapeDtypeStruct((b, RW_H, tp, RW_N), F32)] + outs[:2],
        scratch_shapes=[pltpu.VMEM((1, RW_PROJ), F32)],
        compiler_params=_cparams(("parallel", "arbitrary"), 48),
        name="rwkv_prep",
    )(proj, shift0, wl["mu"], wl["w2a2"], wl["w0a0"], wl["g2"], wl["kk"], wl["ka"], wl["rk"],
      e512, blk_tri, blk_all)


def _rw_solve_kernel(a_ref, t_ref, at_sc, tt_sc):
    c = RW_CHUNK
    per = LANE // c
    for jb in range(c * c // LANE):
        blk = a_ref[:, jb * LANE:(jb + 1) * LANE].T
        at_sc[jb * per:(jb + 1) * per] = blk.reshape(per, c, LANE)
    rows = lax.broadcasted_iota(jnp.int32, (c, LANE), 0)

    def body_t(t, carry_):
        def body_u(u, acc):
            return acc - at_sc[t, pl.ds(u, 1), :] * tt_sc[u]

        tt_sc[t] = lax.fori_loop(0, t, body_u, jnp.where(rows == t, 1.0, 0.0))
        return carry_

    lax.fori_loop(0, c, body_t, 0)
    for jb in range(c * c // LANE):
        blk = tt_sc[jb * per:(jb + 1) * per].reshape(LANE, LANE)
        t_ref[:, jb * LANE:(jb + 1) * LANE] = blk.T


def _rw_solve(a2d):
    nch, cc = a2d.shape
    return pl.pallas_call(
        _rw_solve_kernel,
        grid=(nch // LANE,),
        in_specs=[pl.BlockSpec((LANE, cc), lambda i: (i, 0))],
        out_specs=pl.BlockSpec((LANE, cc), lambda i: (i, 0)),
        out_shape=jax.ShapeDtypeStruct((nch, cc), F32),
        scratch_shapes=[pltpu.VMEM((RW_CHUNK, RW_CHUNK, LANE), F32)] * 2,
        compiler_params=_cparams(("parallel",), 40),
        name="rwkv_solve",
    )(a2d)


def _rw_chain_kernel(kt_ref, bt_ref, kq_ref, rt_ref, v_ref, gc_ref, tm_ref, g_ref, bon_ref,
                     lnw_ref, lnb_ref, e_ref, h0_ref, o_ref, hout_ref,
                     h_sc, gm_sc, nm_sc, qe_sc, y_sc, *, tt, to):
    i = pl.program_id(2)
    cn = RW_CHUNK
    nck = tt // cn

    @pl.when(i == 0)
    def _():
        h_sc[...] = h0_ref[...]

    lane = lax.broadcasted_iota(jnp.int32, (1, LANE), 1)
    lo = lane < RW_N
    ri = lax.broadcasted_iota(jnp.int32, (cn, LANE), 0)
    ci = lax.broadcasted_iota(jnp.int32, (cn, LANE), 1) & (RW_N - 1)
    strict = ci < ri
    incl = ci <= ri
    rd = lax.broadcasted_iota(jnp.int32, (LANE, LANE), 0)
    cd = lax.broadcasted_iota(jnp.int32, (LANE, LANE), 1)
    eye = rd == cd
    same_head = (rd >> 6) == (cd >> 6)

    def stack(x):
        return jnp.concatenate([jnp.where(lo, x, 0.0), jnp.where(lo, 0.0, x)], axis=0).astype(BF16)

    for c in range(nck):
        rs = slice(c * cn, (c + 1) * cn)
        kt = kt_ref[rs, :]
        bt = bt_ref[rs, :]
        kq = kq_ref[rs, :]
        rt = rt_ref[rs, :]
        v = v_ref[rs, :]
        gc = gc_ref[rs, :]
        ks = stack(kq)
        vs = stack(v)
        ktb = kt.astype(BF16)
        rtb = rt.astype(BF16)
        ak = jnp.where(strict, _dot_nt(ktb, ks), 0.0).astype(BF16)
        bb = jnp.where(incl, _dot_nt(rtb, stack(bt)), 0.0).astype(BF16)
        bk = jnp.where(incl, _dot_nt(rtb, ks), 0.0).astype(BF16)
        tcat = jnp.concatenate([tm_ref[0, rs, :], tm_ref[1, rs, :]], axis=1).astype(BF16)
        kd = _dot(tcat, stack(kt))
        vd = _dot(tcat, stack(_dot(ak, vs)))
        bh = (bt * gc).astype(BF16)
        kh = (kq * gc).astype(BF16)
        kdb = kd.astype(BF16)
        nvd = (-vd).astype(BF16)
        gm = jnp.where(eye, gc[0:1, :], 0.0) - jnp.where(same_head, _dot_tn(bh, kdb), 0.0)
        nm = jnp.where(same_head, _dot_tn(jnp.concatenate([kh, bh], axis=0),
                                          jnp.concatenate([v.astype(BF16), nvd], axis=0)), 0.0)
        gm_sc[c] = gm
        nm_sc[c] = nm
        qe_sc[rs, :] = rt - _dot(bb, stack(kd))
        y_sc[rs, :] = _dot(jnp.concatenate([bk, bb], axis=1), jnp.concatenate([vs, stack(-vd)], axis=0))

    hcur = h_sc[...]
    for c in range(nck):
        rs = slice(c * cn, (c + 1) * cn)
        hb = hcur.astype(BF16)
        y_sc[rs, :] = y_sc[rs, :] + _dot(qe_sc[rs, :].astype(BF16), hb)
        hcur = _dot(gm_sc[c].astype(BF16), hb) + nm_sc[c]
    h_sc[...] = hcur

    y = y_sc[...]
    e = e_ref[...]
    mean = _dot3_r(y, e) * (1.0 / RW_N)
    yc = y - mean
    var = _dot3_r(yc * yc, e) * (1.0 / RW_N)
    out = (yc * lax.rsqrt(var + RW_GN_EPS) * lnw_ref[...] + lnb_ref[...] + bon_ref[...]) * g_ref[...]
    o_ref[...] = out[:to]

    @pl.when(i == pl.num_programs(2) - 1)
    def _():
        hout_ref[...] = hcur


def _rw_chain(prep, tmat, h0, lnw, lnb, t_out):
    kt, bt, kq, rt, v, gc, _, g, bon = prep
    b, tp, _ = kt.shape
    tt = min(tp, 256)
    to = min(t_out, tt)
    hp = RW_H // 2
    hidx = jnp.arange(LANE) // RW_N
    e128 = (hidx[:, None] == hidx[None, :]).astype(BF16)
    tok = pl.BlockSpec((None, tt, LANE), lambda bi, p, i: (bi, i, p))
    vec = pl.BlockSpec((1, LANE), lambda bi, p, i: (0, p))
    st = pl.BlockSpec((None, None, LANE, LANE), lambda bi, p, i: (bi, p, 0, 0))
    nck = tt // RW_CHUNK
    return pl.pallas_call(
        functools.partial(_rw_chain_kernel, tt=tt, to=to),
        grid=(b, hp, tp // tt),
        in_specs=[tok] * 6 + [
            pl.BlockSpec((None, 2, tt, RW_N), lambda bi, p, i: (bi, p, i, 0)),
            tok, tok, vec, vec,
            pl.BlockSpec((LANE, LANE), lambda bi, p, i: (0, 0)),
            st,
        ],
        out_specs=[pl.BlockSpec((None, to, LANE), lambda bi, p, i: (bi, i, p)), st],
        out_shape=[
            jax.ShapeDtypeStruct((b, t_out, RW_W), F32),
            jax.ShapeDtypeStruct((b, hp, LANE, LANE), F32),
        ],
        scratch_shapes=[
            pltpu.VMEM((LANE, LANE), F32),
            pltpu.VMEM((nck, LANE, LANE), F32),
            pltpu.VMEM((nck, LANE, LANE), F32),
            pltpu.VMEM((tt, LANE), F32),
            pltpu.VMEM((tt, LANE), F32),
        ],
        compiler_params=_cparams(("parallel", "parallel", "arbitrary"), 40),
        name="rwkv_chain",
    )(kt, bt, kq, rt, v, gc, tmat, g, bon, lnw, lnb, e128, h0)


def _rwkv(proj, shift0, s0, wl, t):
    b = proj.shape[0]
    prep = _rw_prep(proj, shift0, wl, t)
    ab = prep[6]
    tp = ab.shape[2]
    cc = RW_CHUNK * RW_CHUNK
    a2d = ab.reshape(-1, cc)
    nch = a2d.shape[0]
    nch_p = -(-nch // LANE) * LANE
    if nch_p != nch:
        a2d = jnp.pad(a2d, ((0, nch_p - nch), (0, 0)))
    tmat = _rw_solve(a2d)[:nch].reshape(b, RW_H, tp, RW_N)
    st = jnp.swapaxes(s0, -1, -2).reshape(b, RW_H // 2, 2, RW_N, RW_N)
    z = jnp.zeros_like(st[:, :, 0])
    h0 = jnp.concatenate([jnp.concatenate([st[:, :, 0], z], -1), jnp.concatenate([z, st[:, :, 1]], -1)], -2)
    o, hout = _rw_chain(prep, tmat, h0, wl["lnw"], wl["lnb"], t)
    s_new = jnp.stack([hout[:, :, :RW_N, :RW_N], hout[:, :, RW_N:, RW_N:]], axis=2)
    s_new = jnp.swapaxes(s_new.reshape(b, RW_H, RW_N, RW_N), -1, -2)
    return o, s_new


def _merge_kernel(x_ref, gm_ref, oa_ref, ob_ref, oc_ref, ga_ref, gb_ref, gc_ref,
                  wa_ref, wb_ref, wc_ref, wo_ref, o_ref):
    merged = (_sigmoid(ga_ref[...]) * _dot(oa_ref[...].astype(BF16), wa_ref[...])
              + _sigmoid(gb_ref[...]) * _dot(ob_ref[...].astype(BF16), wb_ref[...])
              + _sigmoid(gc_ref[...]) * _dot(oc_ref[...].astype(BF16), wc_ref[...]))
    mix = _dot(merged.astype(BF16), wo_ref[...])
    o_ref[...] = x_ref[...] + gm_ref[...] * mix


def _merge(x, g1, o_ret, o_fox, o_rw, proj, wl):
    g_, r, d = x.shape
    tm = min(r, 512)
    rm = g1.shape[1]
    w = o_ret.shape[-1]

    def tok(width, cb=0):
        return pl.BlockSpec((None, tm, width), lambda g, i, cb=cb: (g, i, cb))

    def const(shape):
        return pl.BlockSpec(shape, lambda g, i: (0, 0))

    return pl.pallas_call(
        _merge_kernel,
        grid=(g_, r // tm),
        in_specs=[
            tok(d), _row_spec(rm, tm, d), tok(w), tok(w), tok(w),
            tok(d, C_GATE // d), tok(d, C_GATE // d + 1), tok(d, C_GATE // d + 2),
            const((w, d)), const((w, d)), const((w, d)), const((d, d)),
        ],
        out_specs=tok(d),
        out_shape=jax.ShapeDtypeStruct((g_, r, d), F32),
        compiler_params=_cparams(("parallel", "parallel"), 48),
        name="merge_out",
    )(x, g1, o_ret, o_fox, o_rw, proj, proj, proj, wl["w_br_ret"], wl["w_br_fox"], wl["w_br_rw"], wl["w_out"])


def _ffn_kernel(x_ref, sc_ref, sh_ref, gm_ref, g_ref, wu_ref, wd_ref, nf_ref, o_ref, h_sc, acc_sc, *, final):
    f = pl.program_id(2)

    @pl.when(f == 0)
    def _():
        h_sc[...] = _norm_mod(x_ref[...], sc_ref[...], sh_ref[...], g_ref[...]).astype(BF16)
        acc_sc[...] = jnp.zeros_like(acc_sc)

    u = jnp.maximum(_dot(h_sc[...], wu_ref[...]), 0.0)
    acc_sc[...] += _dot((u * u).astype(BF16), wd_ref[...])

    @pl.when(f == pl.num_programs(2) - 1)
    def _():
        xn = x_ref[...] + gm_ref[...] * acc_sc[...]
        if final:
            ms = jnp.mean(xn * xn, axis=-1, keepdims=True)
            xn = xn * lax.rsqrt(ms + NORM_EPS) * nf_ref[...]
        o_ref[...] = xn


def _ffn(x, sc, sh, g2, gamma, w_up, w_down, norm_final, final):
    g_, r, d = x.shape
    dff = w_up.shape[1]
    tm = min(r, 1024)
    tf = 1024
    rm = sc.shape[1]
    return pl.pallas_call(
        functools.partial(_ffn_kernel, final=final),
        grid=(g_, r // tm, dff // tf),
        in_specs=[
            pl.BlockSpec((None, tm, d), lambda g, i, f: (g, i, 0)),
            _row_spec(rm, tm, d), _row_spec(rm, tm, d), _row_spec(rm, tm, d),
            pl.BlockSpec((1, d), lambda g, i, f: (0, 0)),
            pl.BlockSpec((d, tf), lambda g, i, f: (0, f)),
            pl.BlockSpec((tf, d), lambda g, i, f: (f, 0)),
            pl.BlockSpec((1, d), lambda g, i, f: (0, 0)),
        ],
        out_specs=pl.BlockSpec((None, tm, d), lambda g, i, f: (g, i, 0)),
        out_shape=jax.ShapeDtypeStruct((g_, r, d), F32),
        scratch_shapes=[pltpu.VMEM((tm, d), BF16), pltpu.VMEM((tm, d), F32)],
        compiler_params=_cparams(("parallel", "parallel", "arbitrary"), 48),
        name="ffn",
    )(x, sc, sh, g2, gamma, w_up, w_down, norm_final)


def _rope_tables(q0, t):
    d = RET_D
    inv = 1.0 / (ROPE_BASE ** (jnp.arange(0, d, 2, dtype=F32) / d))
    ang = (q0 + jnp.arange(t)).astype(F32)[:, None] * inv[None, :]
    cos, sin = jnp.cos(ang), jnp.sin(ang)
    return jnp.concatenate([cos, cos], -1), jnp.concatenate([-sin, sin], -1)


def _prep_layer_weights(l, w):
    w_in = w["w_in"][l]
    d = w_in.shape[0]
    ret_cols, fox_qkv = 2048, 1536
    f0 = ret_cols + fox_qkv
    rw0 = f0 + FOX_H
    g0 = rw0 + RW_PROJ
    zpad = jnp.zeros((d, LANE - FOX_H), F32)
    packed = jnp.concatenate([
        w_in[:, rw0:g0], w_in[:, f0:rw0], zpad, jnp.zeros((d, LANE), F32),
        w_in[:, g0:], w_in[:, ret_cols:f0], w_in[:, :ret_cols]], axis=1).astype(BF16)
    zl = jnp.zeros((RW_LORA // 2, RW_W), F32)
    w2a2 = jnp.concatenate([jnp.concatenate([w["rw_w2"][l], zl], 1),
                            jnp.concatenate([zl, w["rw_a2"][l]], 1)], 0)
    row = lambda a: a.reshape(1, -1)
    return {
        "w_in": packed,
        "norm_mix": row(w["norm_mix"][l]), "norm_ffn": row(w["norm_ffn"][l]),
        "bf": jnp.pad(w["fox_bf"][l], (0, LANE - FOX_H)).reshape(1, LANE),
        "mu": row(w["rw_mu"][l]), "w2a2": w2a2,
        "w0a0": jnp.concatenate([w["rw_w0"][l], w["rw_a0"][l]]).reshape(1, -1),
        "g2": w["rw_g2"][l].astype(BF16),
        "kk": row(w["rw_kk"][l]), "ka": row(w["rw_ka"][l]), "rk": row(w["rw_rk"][l]),
        "lnw": row(w["rw_lnw"][l]), "lnb": row(w["rw_lnb"][l]),
        "w_br_ret": w["w_br_ret"][l].astype(BF16), "w_br_fox": w["w_br_fox"][l].astype(BF16),
        "w_br_rw": w["w_br_rw"][l].astype(BF16), "w_out": w["w_out"][l].astype(BF16),
        "w_up": w["w_up"][l].astype(BF16), "w_down": w["w_down"][l].astype(BF16),
    }


def _ret_log_gamma_table():
    lg = jnp.log1p(-jnp.exp2(-5.0 - jnp.arange(RET_H, dtype=F32)))
    return jnp.broadcast_to(lg[:, None, None], (RET_H, 1, RET_D))


def _layer(x, mod, q0, past, ret0, wkv0, shift0, wl, norm_final, final, flat):
    b, t, d = x.shape
    sh1, sc1, g1, sh2, sc2, g2 = [m[:, None, :] for m in jnp.split(mod, 6, axis=-1)]
    if flat:
        rep = lambda m: jnp.repeat(m, t, axis=1).reshape(1, b * t, d)
        sh1, sc1, g1, sh2, sc2, g2 = [rep(m) for m in (sh1, sc1, g1, sh2, sc2, g2)]
        xd = x.reshape(1, b * t, d)
    else:
        xd = x
    proj_d = _norm_mm(xd, sc1, sh1, wl["norm_mix"], wl["w_in"])
    proj = proj_d.reshape(b, t, PROJ_W)

    cosf, sinf = _rope_tables(q0, t)
    o_ret, r_new = _retention(proj, cosf, sinf, _ret_log_gamma_table(), ret0)

    lf, cum = _fox_gate(proj_d, wl["bf"])
    lf = lf.reshape(b, t, LANE)[:, :, :FOX_H]
    if past is None:
        cum = cum.reshape(b, t, LANE)[:, :, :FOX_H]
        o_fox = _fox_prompt(proj, jnp.swapaxes(cum, 1, 2)[..., None])
    else:
        layer, page_table, cache_k, cache_v, cache_lf_t = past
        page = cache_k.shape[2]
        lfn_t = jnp.pad(jnp.swapaxes(lf, 1, 2), ((0, 0), (0, 0), (0, page - t)))
        o_fox = _fox_sample(layer, proj, lfn_t, page_table, cache_k, cache_v, cache_lf_t)

    o_rw, s_new = _rwkv(proj, shift0[:, None, :], wkv0, wl, t)

    dn = lambda a: a.reshape(xd.shape[0], xd.shape[1], -1)
    x1 = _merge(xd, g1, dn(o_ret), dn(o_fox), dn(o_rw), proj_d, wl)
    x2 = _ffn(x1, sc2, sh2, g2, wl["norm_ffn"], wl["w_up"], wl["w_down"], norm_final, final)
    fk = proj[:, :, C_FK:C_FV].reshape(b, t, FOX_H, FOX_DH)
    fv = proj[:, :, C_FV:C_RQ].reshape(b, t, FOX_H, FOX_DH)
    shift = proj[:, t - 1, C_RW:C_RW + RW_PROJ]
    return x2.reshape(b, t, d), (fk, fv, lf, r_new, s_new, shift)


def kernel(x_prompt, x_sample, c_prompt, c_sample, cache_k, cache_v, cache_logf, page_table, state_ret, state_wkv, state_shift, ada_w, ada_b, norm_mix, norm_ffn, w_in, fox_bf, rw_mu, rw_w0, rw_w2, rw_a0, rw_a2, rw_g2, rw_kk, rw_ka, rw_rk, rw_lnw, rw_lnb, w_br_ret, w_br_fox, w_br_rw, w_out, w_up, w_down, norm_final):
    depth = w_in.shape[0]
    b, t, d = x_prompt.shape
    db, ts, _ = x_sample.shape
    n_pool, page = cache_k.shape[1], cache_k.shape[2]
    past_len = page_table.shape[1] * page
    w = dict(w_in=w_in, norm_mix=norm_mix, norm_ffn=norm_ffn, fox_bf=fox_bf, rw_mu=rw_mu, rw_w0=rw_w0,
             rw_w2=rw_w2, rw_a0=rw_a0, rw_a2=rw_a2, rw_g2=rw_g2, rw_kk=rw_kk, rw_ka=rw_ka, rw_rk=rw_rk,
             rw_lnw=rw_lnw, rw_lnb=rw_lnb, w_br_ret=w_br_ret, w_br_fox=w_br_fox, w_br_rw=w_br_rw,
             w_out=w_out, w_up=w_up, w_down=w_down)
    rows = b + db
    rows_p = -(-rows // 8) * 8
    c_all = jnp.pad(jnp.concatenate([c_prompt, c_sample], 0), ((0, rows_p - rows), (0, 0)))
    mod = _modulation(c_all, ada_w, ada_b)
    ck = cache_k.reshape(depth, n_pool, page, FOX_H * FOX_DH)
    cv = cache_v.reshape(depth, n_pool, page, FOX_H * FOX_DH)
    clf_t = jnp.swapaxes(cache_logf, 2, 3)
    nf = norm_final.reshape(1, d)
    xp, xs = x_prompt, x_sample
    st_p, st_s = [], []
    for l in range(depth):
        wl = _prep_layer_weights(l, w)
        final = l == depth - 1
        xp, sp = _layer(xp, mod[l, :b], 0, None, jnp.zeros((b, RET_H, RET_D, RET_D), F32),
                        jnp.zeros((b, RW_H, RW_N, RW_N), F32), jnp.zeros((b, RW_PROJ), F32),
                        wl, nf, final, False)
        st_p.append(sp)
        xs, ss = _layer(xs, mod[l, b:rows], past_len, (l, page_table, ck, cv, clf_t), state_ret[l],
                        state_wkv[l], state_shift[l], wl, nf, final, True)
        st_s.append(ss)
    outs = [xp, xs]
    for st in (st_p, st_s):
        for i in range(6):
            outs.append(jnp.stack([s[i] for s in st]))
    y_p, y_s, k_p, v_p, lf_p, ret_p, wkv_p, sh_p, k_s, v_s, lf_s, ret_s, wkv_s, sh_s = outs
    return (y_p, y_s, k_p, v_p, lf_p, ret_p, wkv_p, sh_p, k_s, v_s, lf_s, ret_s, wkv_s, sh_s)
```

```python
import functools

import jax
import jax.numpy as jnp
from jax import lax
from jax.experimental import pallas as pl
from jax.experimental.pallas import tpu as pltpu

F32 = jnp.float32
BF16 = jnp.bfloat16
HIGHEST = lax.Precision.HIGHEST

RET_H, RET_D = 4, 128
FOX_H, FOX_DH = 8, 64
RW_H, RW_N = 8, 64
RW_W = RW_H * RW_N
RW_LORA = 128
RW_GATE_R = 128
RW_PROJ = 3 * RW_W + RW_LORA + RW_GATE_R
RET_GN_EPS = 1e-6
RW_GN_EPS = 64e-5
NORM_EPS = 1e-6
ROPE_BASE = 10000.0
NEG = -1e30
LOG2E = 1.4426950408889634

LANE = 128
C_RW = 0
C_F = 1792
C_GATE = 2048
C_FQ, C_FK, C_FV = 5120, 5632, 6144
C_RQ, C_RK, C_RV, C_RG = 6656, 7168, 7680, 8192
PROJ_W = 8704

RW_CHUNK = 64
RET_CHUNK = 128


def _cparams(sem, vmem_mb):
    return pltpu.CompilerParams(dimension_semantics=sem, vmem_limit_bytes=vmem_mb << 20)


def _dot(a, b):
    return jnp.dot(a, b, preferred_element_type=F32)


def _dot_nt(a, b):
    return lax.dot_general(a, b, (((1,), (1,)), ((), ())), preferred_element_type=F32)


def _dot_tn(a, b):
    return lax.dot_general(a, b, (((0,), (0,)), ((), ())), preferred_element_type=F32)


def _split3(x):
    hi = x.astype(BF16)
    r1 = x - hi.astype(F32)
    mid = r1.astype(BF16)
    lo = (r1 - mid.astype(F32)).astype(BF16)
    return hi, mid, lo


def _dot3_l(m01, x):
    hi, mid, lo = _split3(x)
    return _dot(m01, hi) + _dot(m01, mid) + _dot(m01, lo)


def _dot3_r(x, m01):
    hi, mid, lo = _split3(x)
    return _dot(hi, m01) + _dot(mid, m01) + _dot(lo, m01)


def _sigmoid(x):
    return 1.0 / (1.0 + jnp.exp(-x))


def _softplus(x):
    return jnp.maximum(x, 0.0) + jnp.log(1.0 + jnp.exp(-jnp.abs(x)))


def _pad_rows(a, rows):
    if a.shape[0] == rows:
        return a
    return jnp.concatenate([a, jnp.zeros((rows - a.shape[0],) + a.shape[1:], a.dtype)], axis=0)


def _mod_kernel(c_ref, w_ref, b_ref, o_ref):
    c = c_ref[...]
    s = c * _sigmoid(c)
    o_ref[...] = jnp.dot(s, w_ref[...], precision=HIGHEST, preferred_element_type=F32) + b_ref[...]


def _modulation(c_all, ada_w, ada_b):
    depth, d, n = ada_w.shape
    rows = c_all.shape[0]
    tn = 1536
    return pl.pallas_call(
        _mod_kernel,
        grid=(depth, n // tn),
        in_specs=[
            pl.BlockSpec((rows, d), lambda l, j: (0, 0)),
            pl.BlockSpec((None, d, tn), lambda l, j: (l, 0, j)),
            pl.BlockSpec((None, 1, tn), lambda l, j: (l, 0, j)),
        ],
        out_specs=pl.BlockSpec((None, rows, tn), lambda l, j: (l, 0, j)),
        out_shape=jax.ShapeDtypeStruct((depth, rows, n), F32),
        compiler_params=_cparams(("parallel", "parallel"), 40),
        name="ada_mod",
    )(c_all, ada_w, ada_b.reshape(depth, 1, n))


def _norm_mod(x, sc, sh, g):
    ms = jnp.mean(x * x, axis=-1, keepdims=True)
    y = x * lax.rsqrt(ms + NORM_EPS) * g
    return y * (1.0 + sc) + sh


def _norm_mm_kernel(x_ref, sc_ref, sh_ref, g_ref, w_ref, o_ref, h_ref):
    @pl.when(pl.program_id(2) == 0)
    def _():
        h_ref[...] = _norm_mod(x_ref[...], sc_ref[...], sh_ref[...], g_ref[...]).astype(BF16)

    o_ref[...] = _dot(h_ref[...], w_ref[...])


def _row_spec(rm, tm, d):
    if rm == 1:
        return pl.BlockSpec((None, 1, d), lambda g, i, *_: (g, 0, 0))
    return pl.BlockSpec((None, tm, d), lambda g, i, *_: (g, i, 0))


def _norm_mm(x, sc, sh, gamma, w):
    g_, r, d = x.shape
    n = w.shape[1]
    tm = min(r, 2048)
    tn = 512
    rm = sc.shape[1]
    return pl.pallas_call(
        _norm_mm_kernel,
        grid=(g_, r // tm, n // tn),
        in_specs=[
            pl.BlockSpec((None, tm, d), lambda g, i, j: (g, i, 0)),
            _row_spec(rm, tm, d),
            _row_spec(rm, tm, d),
            pl.BlockSpec((1, d), lambda g, i, j: (0, 0)),
            pl.BlockSpec((d, tn), lambda g, i, j: (0, j)),
        ],
        out_specs=pl.BlockSpec((None, tm, tn), lambda g, i, j: (g, i, j)),
        out_shape=jax.ShapeDtypeStruct((g_, r, n), F32),
        scratch_shapes=[pltpu.VMEM((tm, d), BF16)],
        compiler_params=_cparams(("parallel", "parallel", "arbitrary"), 56),
        name="norm_in_proj",
    )(x, sc, sh, gamma, w)


def _fox_gate_kernel(f_ref, b_ref, tri_ref, lf_ref, cum_ref, carry_ref):
    @pl.when(pl.program_id(1) == 0)
    def _():
        carry_ref[...] = jnp.zeros_like(carry_ref)

    z = f_ref[...] + b_ref[...]
    lf = -_softplus(-z)
    lf_ref[...] = lf
    cum = _dot3_l(tri_ref[...], lf) + carry_ref[...]
    cum_ref[...] = cum * LOG2E
    tb = cum.shape[0]
    carry_ref[...] = cum[tb - 1:tb, :]


def _fox_gate(proj, bf_pad):
    g_, r, _ = proj.shape
    tb = min(r, 512)
    tri = jnp.tril(jnp.ones((tb, tb), F32)).astype(BF16)
    return pl.pallas_call(
        _fox_gate_kernel,
        grid=(g_, r // tb),
        in_specs=[
            pl.BlockSpec((None, tb, LANE), lambda g, i: (g, i, C_F // LANE)),
            pl.BlockSpec((1, LANE), lambda g, i: (0, 0)),
            pl.BlockSpec((tb, tb), lambda g, i: (0, 0)),
        ],
        out_specs=[
            pl.BlockSpec((None, tb, LANE), lambda g, i: (g, i, 0)),
            pl.BlockSpec((None, tb, LANE), lambda g, i: (g, i, 0)),
        ],
        out_shape=[jax.ShapeDtypeStruct((g_, r, LANE), F32)] * 2,
        scratch_shapes=[pltpu.VMEM((1, LANE), F32)],
        compiler_params=_cparams(("parallel", "arbitrary"), 32),
        name="fox_gate",
    )(proj, bf_pad, tri)


def _ret_kernel(q_ref, k_ref, v_ref, g_ref, cos_ref, sin_ref, lg_ref, r0_ref,
                o_ref, rout_ref, r_sc, *, ce, nb):
    cp = RET_CHUNK
    c = pl.program_id(2)

    @pl.when(c == 0)
    def _():
        r_sc[...] = r0_ref[...]

    rows = max(nb * ce, cp)
    cosf = _pad_rows(cos_ref[...], rows)
    sinf = _pad_rows(sin_ref[...], rows)

    def rot(x):
        return x * cosf + pltpu.roll(x, RET_D // 2, axis=1) * sinf

    q = rot(_pad_rows(q_ref[...], rows)).astype(BF16)
    kf = rot(_pad_rows(k_ref[...], rows)) * (RET_D ** -0.5)
    vb = _pad_rows(v_ref[...], rows).astype(BF16)
    lg = lg_ref[...]
    lg1 = lg[:, :1]
    row = lax.broadcasted_iota(jnp.int32, (cp, RET_D), 0).astype(F32)
    qdec = jnp.exp(lg * (row + 1.0))
    kdec = jnp.exp(lg * (ce - 1.0 - row))
    ri = lax.broadcasted_iota(jnp.int32, (cp, cp), 0)
    ci = lax.broadcasted_iota(jnp.int32, (cp, cp), 1)
    diff = (ri - ci).astype(F32)
    dmask = jnp.where(diff >= 0.0, jnp.exp(lg1 * jnp.maximum(diff, 0.0)), 0.0)
    gdec = jnp.exp(lg1 * float(ce))

    r_cur = r_sc[...]
    outs = []
    for j in range(nb):
        rs = slice(j * cp, (j + 1) * cp)
        qb = q[rs]
        s = _dot_nt(qb, kf[rs].astype(BF16)) * dmask
        inner = _dot(s.astype(BF16), vb[rs])
        cross = _dot(qb, r_cur.astype(BF16)) * qdec
        r_cur = gdec * r_cur + _dot_tn((kf[rs] * kdec).astype(BF16), vb[rs])
        outs.append(inner + cross)
    r_sc[...] = r_cur
    o = outs[0] if nb == 1 else jnp.concatenate(outs, axis=0)
    mu = jnp.mean(o, axis=-1, keepdims=True)
    oc = o - mu
    var = jnp.mean(oc * oc, axis=-1, keepdims=True)
    gg = _pad_rows(g_ref[...], rows)
    out = oc * lax.rsqrt(var + RET_GN_EPS) * (gg * _sigmoid(gg))
    o_ref[...] = out[:nb * ce]

    @pl.when(c == pl.num_programs(2) - 1)
    def _():
        rout_ref[...] = r_cur


def _retention(proj, cosf, sinf, lg_tab, r0):
    b, t, _ = proj.shape
    ce = min(RET_CHUNK, t)
    n = t // ce
    nb = 4 if (ce == RET_CHUNK and n % 4 == 0) else 1
    tr = nb * ce

    def col(c0):
        return pl.BlockSpec((None, tr, RET_D), lambda bi, h, c, c0=c0: (bi, c, c0 // RET_D + h))

    return pl.pallas_call(
        functools.partial(_ret_kernel, ce=ce, nb=nb),
        grid=(b, RET_H, n // nb),
        in_specs=[
            col(C_RQ), col(C_RK), col(C_RV), col(C_RG),
            pl.BlockSpec((tr, RET_D), lambda bi, h, c: (c, 0)),
            pl.BlockSpec((tr, RET_D), lambda bi, h, c: (c, 0)),
            pl.BlockSpec((None, 1, RET_D), lambda bi, h, c: (h, 0, 0)),
            pl.BlockSpec((None, None, RET_D, RET_D), lambda bi, h, c: (bi, h, 0, 0)),
        ],
        out_specs=[
            pl.BlockSpec((None, tr, RET_D), lambda bi, h, c: (bi, c, h)),
            pl.BlockSpec((None, None, RET_D, RET_D), lambda bi, h, c: (bi, h, 0, 0)),
        ],
        out_shape=[
            jax.ShapeDtypeStruct((b, t, RET_H * RET_D), F32),
            jax.ShapeDtypeStruct((b, RET_H, RET_D, RET_D), F32),
        ],
        scratch_shapes=[pltpu.VMEM((RET_D, RET_D), F32)],
        compiler_params=_cparams(("parallel", "parallel", "arbitrary"), 32),
        name="retention",
    )(proj, proj, proj, proj, cosf, sinf, lg_tab, r0)


def _fox_kernel(qi_ref, ki_ref, q_ref, k_ref, v_ref, ck_ref, o_ref, qt_sc, m_sc, l_sc, acc_sc, *, tq, tk):
    qi = qi_ref[pl.program_id(2)]
    ki = ki_ref[pl.program_id(2)]
    head0 = lax.broadcasted_iota(jnp.int32, (LANE, 1), 0) < FOX_DH

    @pl.when(ki == 0)
    def _():
        q = q_ref[...] * (FOX_DH ** -0.5 * LOG2E)
        qt_sc[...] = q.T.astype(BF16)
        m_sc[...] = jnp.full_like(m_sc, NEG)
        l_sc[...] = jnp.zeros_like(l_sc)
        acc_sc[...] = jnp.zeros_like(acc_sc)

    def step(diagonal):
        kb = k_ref[...].astype(BF16)
        vt = v_ref[...].T.astype(BF16)
        qt = qt_sc[...]
        if diagonal:
            key = lax.broadcasted_iota(jnp.int32, (tk, tq), 0)
            qry = lax.broadcasted_iota(jnp.int32, (tk, tq), 1)
            causal = key <= qry
        for h in range(2):
            hs = slice(h * FOX_DH, (h + 1) * FOX_DH)
            qth = jnp.where(head0 if h == 0 else jnp.logical_not(head0), qt, jnp.zeros_like(qt))
            s = _dot(kb, qth) - ck_ref[h]
            if diagonal:
                s = jnp.where(causal, s, NEG)
            m_prev = m_sc[h:h + 1, :]
            m_new = jnp.maximum(m_prev, jnp.max(s, axis=0, keepdims=True))
            alpha = jnp.exp2(m_prev - m_new)
            p = jnp.exp2(s - m_new)
            l_sc[h:h + 1, :] = alpha * l_sc[h:h + 1, :] + jnp.sum(p, axis=0, keepdims=True)
            m_sc[h:h + 1, :] = m_new
            acc_sc[hs, :] = alpha * acc_sc[hs, :] + _dot(vt[hs, :], p.astype(BF16))

    @pl.when(ki < qi)
    def _():
        step(False)

    @pl.when(ki == qi)
    def _():
        step(True)
        linv = jnp.where(head0, 1.0 / l_sc[0:1, :], 1.0 / l_sc[1:2, :])
        o_ref[...] = (acc_sc[...] * linv).T


def _fox_prompt(proj, cum_c):
    b, t, _ = proj.shape
    tq = tk = min(t, 512)
    nq = t // tq
    hp = FOX_H // 2
    pairs = [(qi, ki) for qi in range(nq) for ki in range(qi + 1)]
    qi_tab = jnp.array([p_[0] for p_ in pairs], jnp.int32)
    ki_tab = jnp.array([p_[1] for p_ in pairs], jnp.int32)
    grid_spec = pltpu.PrefetchScalarGridSpec(
        num_scalar_prefetch=2,
        grid=(b, hp, len(pairs)),
        in_specs=[
            pl.BlockSpec((None, tq, LANE), lambda bi, p, s, qt, kt: (bi, qt[s], C_FQ // LANE + p)),
            pl.BlockSpec((None, tk, LANE), lambda bi, p, s, qt, kt: (bi, kt[s], C_FK // LANE + p)),
            pl.BlockSpec((None, tk, LANE), lambda bi, p, s, qt, kt: (bi, kt[s], C_FV // LANE + p)),
            pl.BlockSpec((None, 2, tk, 1), lambda bi, p, s, qt, kt: (bi, p, kt[s], 0)),
        ],
        out_specs=pl.BlockSpec((None, tq, LANE), lambda bi, p, s, qt, kt: (bi, qt[s], p)),
        scratch_shapes=[
            pltpu.VMEM((LANE, tq), BF16),
            pltpu.VMEM((2, tq), F32),
            pltpu.VMEM((2, tq), F32),
            pltpu.VMEM((LANE, tq), F32),
        ],
    )
    return pl.pallas_call(
        functools.partial(_fox_kernel, tq=tq, tk=tk),
        grid_spec=grid_spec,
        out_shape=jax.ShapeDtypeStruct((b, t, FOX_H * FOX_DH), F32),
        compiler_params=_cparams(("parallel", "parallel", "arbitrary"), 40),
        name="fox_prompt",
    )(qi_tab, ki_tab, proj, proj, proj, cum_c)


def _fox_dec_kernel(pt_ref, q_ref, kn_ref, vn_ref, lfn_ref, u_ref, pm_ref, *rest, pps, ts):
    k_refs = rest[:pps]
    v_refs = rest[pps:2 * pps]
    lf_refs = rest[2 * pps:3 * pps]
    o_ref, qbd, m_sc, l_sc, acc_sc, carry_sc = rest[3 * pps:]
    j = pl.program_id(1)
    rows = FOX_H * ts
    width = FOX_H * FOX_DH
    page = k_refs[0].shape[-1]
    assert ts & (ts - 1) == 0 and FOX_DH & (FOX_DH - 1) == 0
    rowh = lax.broadcasted_iota(jnp.int32, (rows, width), 0) >> (ts.bit_length() - 1)
    colh = lax.broadcasted_iota(jnp.int32, (rows, width), 1) >> (FOX_DH.bit_length() - 1)
    own = rowh == colh

    @pl.when(j == 0)
    def _():
        q = q_ref[...] * (FOX_DH ** -0.5 * LOG2E)
        qt = jnp.broadcast_to(q[None], (FOX_H, ts, width)).reshape(rows, width)
        qbd[...] = jnp.where(own, qt, 0.0).astype(BF16)
        m_sc[...] = jnp.full_like(m_sc, NEG)
        l_sc[...] = jnp.zeros_like(l_sc)
        acc_sc[...] = jnp.zeros_like(acc_sc)
        carry_sc[...] = jnp.zeros_like(carry_sc)

    def expand(cum):
        return jnp.broadcast_to(cum[:, None, :], (FOX_H, ts, page)).reshape(rows, page)

    def update(k_list, v_list, cum_list, valid):
        qb = qbd[...]
        s = [_dot(qb, kp.astype(BF16)) - expand(cm) * LOG2E for kp, cm in zip(k_list, cum_list)]
        s = s[0] if len(s) == 1 else jnp.concatenate(s, axis=1)
        if valid is not None:
            s = jnp.where(valid, s, NEG)
        m_prev = m_sc[...]
        m_new = jnp.maximum(m_prev, jnp.max(s, axis=-1, keepdims=True))
        alpha = jnp.exp2(m_prev - m_new)
        p = jnp.exp2(s - m_new)
        l_sc[...] = alpha * l_sc[...] + jnp.sum(p, axis=-1, keepdims=True)
        m_sc[...] = m_new
        pv = None
        for i, vp in enumerate(v_list):
            t_ = _dot_nt(p[:, i * page:(i + 1) * page].astype(BF16), vp.astype(BF16))
            pv = t_ if pv is None else pv + t_
        acc_sc[...] = alpha * acc_sc[...] + pv

    lf_all = jnp.concatenate([r[...] for r in lf_refs], axis=0) if pps > 1 else lf_refs[0][...]
    cin = _dot3_r(lf_all, u_ref[...])
    carry = carry_sc[...]
    cum = cin + jnp.concatenate([carry] * pps, axis=0) if pps > 1 else cin + carry
    if pps > 1:
        cum = cum + _dot3_l(pm_ref[...], cin)[:, page - 1:page]
    carry = cum[(pps - 1) * FOX_H:pps * FOX_H, page - 1:page]
    carry_sc[...] = carry
    update([r[...].reshape(width, page) for r in k_refs], [r[...].reshape(width, page) for r in v_refs],
           [cum[i * FOX_H:(i + 1) * FOX_H] for i in range(pps)], None)

    @pl.when(j == pl.num_programs(1) - 1)
    def _():
        cumn = _dot3_r(lfn_ref[...], u_ref[...]) + carry
        kn = _pad_rows(kn_ref[...], page).T
        vn = _pad_rows(vn_ref[...], page).T
        colk = lax.broadcasted_iota(jnp.int32, (rows, page), 1)
        trow = lax.broadcasted_iota(jnp.int32, (rows, page), 0) & (ts - 1)
        update([kn], [vn], [cumn], colk <= trow)
        accn = acc_sc[...] / l_sc[...]
        o_ref[...] = jnp.sum(jnp.where(own, accn, 0.0).reshape(FOX_H, ts, width), axis=0)


def _fox_sample(layer, proj, lfn_t, page_table, cache_k, cache_v, cache_lf_t):
    db, ts, _ = proj.shape
    npages = page_table.shape[1]
    page = cache_k.shape[-1]
    width = FOX_H * FOX_DH
    pps = 8 if npages % 8 == 0 else 1
    upper = jnp.triu(jnp.ones((page, page), F32)).astype(BF16)
    ridx = jnp.arange(pps * FOX_H)
    pmat = jnp.logical_and((ridx[:, None] % FOX_H) == (ridx[None, :] % FOX_H),
                           (ridx[None, :] // FOX_H) < (ridx[:, None] // FOX_H)).astype(BF16)
    rows = FOX_H * ts

    def page_spec(shape, p_):
        return pl.BlockSpec((None, None) + shape,
                            lambda bi, j, pt, p_=p_: (layer, pt[bi, j * pps + p_]) + (0,) * len(shape))

    in_specs = [
        pl.BlockSpec((None, ts, width), lambda bi, j, pt: (bi, 0, C_FQ // width)),
        pl.BlockSpec((None, ts, width), lambda bi, j, pt: (bi, 0, C_FK // width)),
        pl.BlockSpec((None, ts, width), lambda bi, j, pt: (bi, 0, C_FV // width)),
        pl.BlockSpec((None, FOX_H, page), lambda bi, j, pt: (bi, 0, 0)),
        pl.BlockSpec((page, page), lambda bi, j, pt: (0, 0)),
        pl.BlockSpec((pps * FOX_H, pps * FOX_H), lambda bi, j, pt: (0, 0)),
    ]
    in_specs += [page_spec((FOX_H, FOX_DH, page), p_) for p_ in range(pps)]
    in_specs += [page_spec((FOX_H, FOX_DH, page), p_) for p_ in range(pps)]
    in_specs += [page_spec((FOX_H, page), p_) for p_ in range(pps)]
    grid_spec = pltpu.PrefetchScalarGridSpec(
        num_scalar_prefetch=1,
        grid=(db, npages // pps),
        in_specs=in_specs,
        out_specs=pl.BlockSpec((None, ts, width), lambda bi, j, pt: (bi, 0, 0)),
        scratch_shapes=[
            pltpu.VMEM((rows, width), BF16),
            pltpu.VMEM((rows, 1), F32),
            pltpu.VMEM((rows, 1), F32),
            pltpu.VMEM((rows, width), F32),
            pltpu.VMEM((FOX_H, 1), F32),
        ],
    )
    args = [proj, proj, proj, lfn_t, upper, pmat] + [cache_k] * pps + [cache_v] * pps + [cache_lf_t] * pps
    return pl.pallas_call(
        functools.partial(_fox_dec_kernel, pps=pps, ts=ts),
        grid_spec=grid_spec,
        out_shape=jax.ShapeDtypeStruct((db, ts, width), F32),
        compiler_params=_cparams(("parallel", "arbitrary"), 40),
        name="fox_sample",
    )(page_table, *args)


def _rw_prep_kernel(p_ref, sh0_ref, mu_ref, w2_ref, w0_ref, g2_ref, kk_ref, ka_ref, rk_ref,
                    e_ref, bl_ref, bt_ref,
                    kt_o, bt_o, kq_o, rt_o, v_o, gc_o, ab_o, g_o, bon_o, carry, *, tr, tv):
    i = pl.program_id(1)
    trb = p_ref.shape[0]

    @pl.when(i == 0)
    def _():
        carry[...] = sh0_ref[...]

    p = _pad_rows(p_ref[...], tr)
    rowi = lax.broadcasted_iota(jnp.int32, (tr, 1), 0)
    prev = jnp.where(rowi == 0, carry[...], pltpu.roll(p, 1, axis=0))
    carry[...] = p_ref[pl.ds(trb - 1, 1), :]
    xm = p + (prev - p) * mu_ref[...]
    r = xm[:, 0:RW_W]
    k = xm[:, RW_W:2 * RW_W]
    v = xm[:, 2 * RW_W:3 * RW_W]
    z = xm[:, 3 * RW_W:3 * RW_W + RW_LORA]
    gl = xm[:, 3 * RW_W + RW_LORA:]
    lane = lax.broadcasted_iota(jnp.int32, (1, RW_LORA), 1)
    zz = jnp.where(lane < RW_LORA // 2, jnp.tanh(z), z)
    pre = jnp.dot(zz, w2_ref[...], precision=HIGHEST, preferred_element_type=F32) + w0_ref[...]
    w_log = -_softplus(-pre[:, :RW_W]) - 0.5
    logdec = -jnp.exp(w_log)
    a = _sigmoid(pre[:, RW_W:])
    g = _dot(_sigmoid(gl).astype(BF16), g2_ref[...])
    e = e_ref[...]
    kk = k * kk_ref[...]
    kk = kk / jnp.maximum(jnp.sqrt(_dot3_r(kk * kk, e)), 1e-12)
    k2 = k * (1.0 + (a - 1.0) * ka_ref[...])
    bonus = _dot3_r(r * k2 * rk_ref[...], e) * v
    if tv < tr:
        valid = rowi < tv
        logdec = jnp.where(valid, logdec, 0.0)
        kk = jnp.where(valid, kk, 0.0)
        k2 = jnp.where(valid, k2, 0.0)
        v = jnp.where(valid, v, 0.0)
        r = jnp.where(valid, r, 0.0)
    cl = _dot3_l(bl_ref[...], logdec)
    ct = _dot3_l(bt_ref[...], logdec)
    e_inv = jnp.exp(-cl)
    kt = kk * jnp.exp(cl - logdec)
    bt = kk * a * e_inv
    kt_o[...] = kt
    bt_o[...] = bt
    kq_o[...] = k2 * e_inv
    rt_o[...] = r * jnp.exp(cl)
    v_o[...] = v
    gc_o[...] = jnp.exp(ct)
    g_o[...] = g
    bon_o[...] = bonus
    lane2 = lax.broadcasted_iota(jnp.int32, (1, LANE), 1)
    for c in range(tr // RW_CHUNK):
        rs = slice(c * RW_CHUNK, (c + 1) * RW_CHUNK)
        for hp in range(RW_H // 2):
            ls = slice(hp * LANE, (hp + 1) * LANE)
            x = kt[rs, ls]
            y = bt[rs, ls].astype(BF16)
            for h in range(2):
                mh = (lane2 < RW_N) if h == 0 else (lane2 >= RW_N)
                xh = jnp.where(mh, x, 0.0).astype(BF16)
                ab_o[2 * hp + h, rs, :] = _dot_nt(xh, y)


def _rw_prep(proj, shift0, wl, tv):
    b, t, _ = proj.shape
    tp = max(t, RW_CHUNK)
    tr = min(tp, 256)
    trb = min(t, tr)
    n = tp // tr
    cidx = jnp.arange(tr) // RW_CHUNK
    same = cidx[:, None] == cidx[None, :]
    blk_tri = jnp.logical_and(same, jnp.arange(tr)[None, :] <= jnp.arange(tr)[:, None]).astype(BF16)
    blk_all = same.astype(BF16)
    hidx = jnp.arange(RW_W) // RW_N
    e512 = (hidx[:, None] == hidx[None, :]).astype(BF16)

    def const(shape):
        return pl.BlockSpec(shape, lambda bi, i: (0,) * len(shape))

    tok = pl.BlockSpec((None, tr, RW_W), lambda bi, i: (bi, i, 0))
    outs = [jax.ShapeDtypeStruct((b, tp, RW_W), F32)] * 6
    return pl.pallas_call(
        functools.partial(_rw_prep_kernel, tr=tr, tv=tv if t < tr else tr),
        grid=(b, n),
        in_specs=[
            pl.BlockSpec((None, trb, RW_PROJ), lambda bi, i: (bi, i, C_RW // RW_PROJ)),
            pl.BlockSpec((None, 1, RW_PROJ), lambda bi, i: (bi, 0, 0)),
            const((1, RW_PROJ)), const((RW_LORA, 2 * RW_W)), const((1, 2 * RW_W)),
            const((RW_GATE_R, RW_W)), const((1, RW_W)), const((1, RW_W)), const((1, RW_W)),
            const((RW_W, RW_W)), const((tr, tr)), const((tr, tr)),
        ],
        out_specs=[tok] * 6 + [pl.BlockSpec((None, RW_H, tr, RW_N), lambda bi, i: (bi, 0, i, 0)), tok, tok],
        out_shape=outs + [jax.ShapeDtypeStruct((b, RW_H, tp, RW_N), F32)] + outs[:2],
        scratch_shapes=[pltpu.VMEM((1, RW_PROJ), F32)],
        compiler_params=_cparams(("parallel", "arbitrary"), 48),
        name="rwkv_prep",
    )(proj, shift0, wl["mu"], wl["w2a2"], wl["w0a0"], wl["g2"], wl["kk"], wl["ka"], wl["rk"],
      e512, blk_tri, blk_all)


def _rw_solve_kernel(a_ref, t_ref, at_sc, tt_sc):
    c = RW_CHUNK
    per = LANE // c
    for jb in range(c * c // LANE):
        blk = a_ref[:, jb * LANE:(jb + 1) * LANE].T
        at_sc[jb * per:(jb + 1) * per] = blk.reshape(per, c, LANE)
    rows = lax.broadcasted_iota(jnp.int32, (c, LANE), 0)
    rb = 4

    def body_t(tb, carry_):
        t0 = tb * rb

        def body_u(u, accs):
            tu = tt_sc[u]
            return tuple(acc - at_sc[t0 + r, pl.ds(u, 1), :] * tu for r, acc in enumerate(accs))

        accs = lax.fori_loop(0, t0, body_u,
                             tuple(jnp.where(rows == t0 + r, 1.0, 0.0) for r in range(rb)))
        done = []
        for r in range(rb):
            acc = accs[r]
            for r2 in range(r):
                acc = acc - at_sc[t0 + r, pl.ds(t0 + r2, 1), :] * done[r2]
            tt_sc[t0 + r] = acc
            done.append(acc)
        return carry_

    lax.fori_loop(0, c // rb, body_t, 0)
    for jb in range(c * c // LANE):
        blk = tt_sc[jb * per:(jb + 1) * per].reshape(LANE, LANE)
        t_ref[:, jb * LANE:(jb + 1) * LANE] = blk.T


def _rw_solve(a2d):
    nch, cc = a2d.shape
    return pl.pallas_call(
        _rw_solve_kernel,
        grid=(nch // LANE,),
        in_specs=[pl.BlockSpec((LANE, cc), lambda i: (i, 0))],
        out_specs=pl.BlockSpec((LANE, cc), lambda i: (i, 0)),
        out_shape=jax.ShapeDtypeStruct((nch, cc), F32),
        scratch_shapes=[pltpu.VMEM((RW_CHUNK, RW_CHUNK, LANE), F32)] * 2,
        compiler_params=_cparams(("parallel",), 40),
        name="rwkv_solve",
    )(a2d)


def _rw_chain_kernel(kt_ref, bt_ref, kq_ref, rt_ref, v_ref, gc_ref, tm_ref, g_ref, bon_ref,
                     lnw_ref, lnb_ref, e_ref, h0_ref, o_ref, hout_ref,
                     h_sc, gm_sc, nm_sc, qe_sc, y_sc, *, tt, to):
    i = pl.program_id(2)
    cn = RW_CHUNK
    nck = tt // cn

    @pl.when(i == 0)
    def _():
        h_sc[...] = h0_ref[...]

    lane = lax.broadcasted_iota(jnp.int32, (1, LANE), 1)
    lo = lane < RW_N
    ri = lax.broadcasted_iota(jnp.int32, (cn, LANE), 0)
    ci = lax.broadcasted_iota(jnp.int32, (cn, LANE), 1) & (RW_N - 1)
    strict = ci < ri
    incl = ci <= ri
    rd = lax.broadcasted_iota(jnp.int32, (LANE, LANE), 0)
    cd = lax.broadcasted_iota(jnp.int32, (LANE, LANE), 1)
    eye = rd == cd
    same_head = (rd >> 6) == (cd >> 6)

    def stack(x):
        return jnp.concatenate([jnp.where(lo, x, 0.0), jnp.where(lo, 0.0, x)], axis=0).astype(BF16)

    cs = range(nck)
    rss = [slice(c * cn, (c + 1) * cn) for c in cs]
    kt = [kt_ref[rs, :] for rs in rss]
    bt = [bt_ref[rs, :] for rs in rss]
    kq = [kq_ref[rs, :] for rs in rss]
    rt = [rt_ref[rs, :] for rs in rss]
    v = [v_ref[rs, :] for rs in rss]
    gc = [gc_ref[rs, :] for rs in rss]
    ks = [stack(kq[c]) for c in cs]
    vs = [stack(v[c]) for c in cs]
    tcat = [jnp.concatenate([tm_ref[0, rs, :], tm_ref[1, rs, :]], axis=1).astype(BF16) for rs in rss]
    ak = [jnp.where(strict, _dot_nt(kt[c].astype(BF16), ks[c]), 0.0).astype(BF16) for c in cs]
    kd = [_dot(tcat[c], stack(kt[c])) for c in cs]
    bb = [jnp.where(incl, _dot_nt(rt[c].astype(BF16), stack(bt[c])), 0.0).astype(BF16) for c in cs]
    bk = [jnp.where(incl, _dot_nt(rt[c].astype(BF16), ks[c]), 0.0).astype(BF16) for c in cs]
    akv = [_dot(ak[c], vs[c]) for c in cs]
    vd = [_dot(tcat[c], stack(akv[c])) for c in cs]
    bh = [(bt[c] * gc[c]).astype(BF16) for c in cs]
    for c in cs:
        gm_sc[c] = (jnp.where(eye, gc[c][0:1, :], 0.0)
                    - jnp.where(same_head, _dot_tn(bh[c], kd[c].astype(BF16)), 0.0))
        qe_sc[rss[c], :] = rt[c] - _dot(bb[c], stack(kd[c]))
    for c in cs:
        kh = (kq[c] * gc[c]).astype(BF16)
        nm_sc[c] = jnp.where(same_head, _dot_tn(
            jnp.concatenate([kh, bh[c]], axis=0),
            jnp.concatenate([v[c].astype(BF16), (-vd[c]).astype(BF16)], axis=0)), 0.0)
        y_sc[rss[c], :] = _dot(jnp.concatenate([bk[c], bb[c]], axis=1),
                               jnp.concatenate([vs[c], stack(-vd[c])], axis=0))

    hcur = h_sc[...]
    for c in range(nck):
        rs = slice(c * cn, (c + 1) * cn)
        hb = hcur.astype(BF16)
        y_sc[rs, :] = y_sc[rs, :] + _dot(qe_sc[rs, :].astype(BF16), hb)
        hcur = _dot(gm_sc[c].astype(BF16), hb) + nm_sc[c]
    h_sc[...] = hcur

    y = y_sc[...]
    e = e_ref[...]
    mean = _dot3_r(y, e) * (1.0 / RW_N)
    yc = y - mean
    var = _dot3_r(yc * yc, e) * (1.0 / RW_N)
    out = (yc * lax.rsqrt(var + RW_GN_EPS) * lnw_ref[...] + lnb_ref[...] + bon_ref[...]) * g_ref[...]
    o_ref[...] = out[:to]

    @pl.when(i == pl.num_programs(2) - 1)
    def _():
        hout_ref[...] = hcur


def _rw_chain(prep, tmat, h0, lnw, lnb, t_out):
    kt, bt, kq, rt, v, gc, _, g, bon = prep
    b, tp, _ = kt.shape
    tt = min(tp, 256)
    to = min(t_out, tt)
    hp = RW_H // 2
    hidx = jnp.arange(LANE) // RW_N
    e128 = (hidx[:, None] == hidx[None, :]).astype(BF16)
    tok = pl.BlockSpec((None, tt, LANE), lambda bi, p, i: (bi, i, p))
    vec = pl.BlockSpec((1, LANE), lambda bi, p, i: (0, p))
    st = pl.BlockSpec((None, None, LANE, LANE), lambda bi, p, i: (bi, p, 0, 0))
    nck = tt // RW_CHUNK
    return pl.pallas_call(
        functools.partial(_rw_chain_kernel, tt=tt, to=to),
        grid=(b, hp, tp // tt),
        in_specs=[tok] * 6 + [
            pl.BlockSpec((None, 2, tt, RW_N), lambda bi, p, i: (bi, p, i, 0)),
            tok, tok, vec, vec,
            pl.BlockSpec((LANE, LANE), lambda bi, p, i: (0, 0)),
            st,
        ],
        out_specs=[pl.BlockSpec((None, to, LANE), lambda bi, p, i: (bi, i, p)), st],
        out_shape=[
            jax.ShapeDtypeStruct((b, t_out, RW_W), F32),
            jax.ShapeDtypeStruct((b, hp, LANE, LANE), F32),
        ],
        scratch_shapes=[
            pltpu.VMEM((LANE, LANE), F32),
            pltpu.VMEM((nck, LANE, LANE), F32),
            pltpu.VMEM((nck, LANE, LANE), F32),
            pltpu.VMEM((tt, LANE), F32),
            pltpu.VMEM((tt, LANE), F32),
        ],
        compiler_params=_cparams(("parallel", "parallel", "arbitrary"), 40),
        name="rwkv_chain",
    )(kt, bt, kq, rt, v, gc, tmat, g, bon, lnw, lnb, e128, h0)


def _rwkv(proj, shift0, s0, wl, t):
    b = proj.shape[0]
    prep = _rw_prep(proj, shift0, wl, t)
    ab = prep[6]
    tp = ab.shape[2]
    cc = RW_CHUNK * RW_CHUNK
    a2d = ab.reshape(-1, cc)
    nch = a2d.shape[0]
    nch_p = -(-nch // LANE) * LANE
    if nch_p != nch:
        a2d = jnp.pad(a2d, ((0, nch_p - nch), (0, 0)))
    tmat = _rw_solve(a2d)[:nch].reshape(b, RW_H, tp, RW_N)
    st = jnp.swapaxes(s0, -1, -2).reshape(b, RW_H // 2, 2, RW_N, RW_N)
    z = jnp.zeros_like(st[:, :, 0])
    h0 = jnp.concatenate([jnp.concatenate([st[:, :, 0], z], -1), jnp.concatenate([z, st[:, :, 1]], -1)], -2)
    o, hout = _rw_chain(prep, tmat, h0, wl["lnw"], wl["lnb"], t)
    s_new = jnp.stack([hout[:, :, :RW_N, :RW_N], hout[:, :, RW_N:, RW_N:]], axis=2)
    s_new = jnp.swapaxes(s_new.reshape(b, RW_H, RW_N, RW_N), -1, -2)
    return o, s_new


def _merge_kernel(x_ref, gm_ref, oa_ref, ob_ref, oc_ref, ga_ref, gb_ref, gc_ref,
                  wa_ref, wb_ref, wc_ref, wo_ref, o_ref):
    merged = (_sigmoid(ga_ref[...]) * _dot(oa_ref[...].astype(BF16), wa_ref[...])
              + _sigmoid(gb_ref[...]) * _dot(ob_ref[...].astype(BF16), wb_ref[...])
              + _sigmoid(gc_ref[...]) * _dot(oc_ref[...].astype(BF16), wc_ref[...]))
    mix = _dot(merged.astype(BF16), wo_ref[...])
    o_ref[...] = x_ref[...] + gm_ref[...] * mix


def _merge(x, g1, o_ret, o_fox, o_rw, proj, wl):
    g_, r, d = x.shape
    tm = min(r, 512)
    rm = g1.shape[1]
    w = o_ret.shape[-1]

    def tok(width, cb=0):
        return pl.BlockSpec((None, tm, width), lambda g, i, cb=cb: (g, i, cb))

    def const(shape):
        return pl.BlockSpec(shape, lambda g, i: (0, 0))

    return pl.pallas_call(
        _merge_kernel,
        grid=(g_, r // tm),
        in_specs=[
            tok(d), _row_spec(rm, tm, d), tok(w), tok(w), tok(w),
            tok(d, C_GATE // d), tok(d, C_GATE // d + 1), tok(d, C_GATE // d + 2),
            const((w, d)), const((w, d)), const((w, d)), const((d, d)),
        ],
        out_specs=tok(d),
        out_shape=jax.ShapeDtypeStruct((g_, r, d), F32),
        compiler_params=_cparams(("parallel", "parallel"), 48),
        name="merge_out",
    )(x, g1, o_ret, o_fox, o_rw, proj, proj, proj, wl["w_br_ret"], wl["w_br_fox"], wl["w_br_rw"], wl["w_out"])


def _ffn_kernel(x_ref, sc_ref, sh_ref, gm_ref, g_ref, wu_ref, wd_ref, nf_ref, o_ref, h_sc, acc_sc, *, final):
    f = pl.program_id(2)

    @pl.when(f == 0)
    def _():
        h_sc[...] = _norm_mod(x_ref[...], sc_ref[...], sh_ref[...], g_ref[...]).astype(BF16)
        acc_sc[...] = jnp.zeros_like(acc_sc)

    u = jnp.maximum(_dot(h_sc[...], wu_ref[...]), 0.0)
    acc_sc[...] += _dot((u * u).astype(BF16), wd_ref[...])

    @pl.when(f == pl.num_programs(2) - 1)
    def _():
        xn = x_ref[...] + gm_ref[...] * acc_sc[...]
        if final:
            ms = jnp.mean(xn * xn, axis=-1, keepdims=True)
            xn = xn * lax.rsqrt(ms + NORM_EPS) * nf_ref[...]
        o_ref[...] = xn


def _ffn(x, sc, sh, g2, gamma, w_up, w_down, norm_final, final):
    g_, r, d = x.shape
    dff = w_up.shape[1]
    tm = min(r, 1024)
    tf = 1024
    rm = sc.shape[1]
    return pl.pallas_call(
        functools.partial(_ffn_kernel, final=final),
        grid=(g_, r // tm, dff // tf),
        in_specs=[
            pl.BlockSpec((None, tm, d), lambda g, i, f: (g, i, 0)),
            _row_spec(rm, tm, d), _row_spec(rm, tm, d), _row_spec(rm, tm, d),
            pl.BlockSpec((1, d), lambda g, i, f: (0, 0)),
            pl.BlockSpec((d, tf), lambda g, i, f: (0, f)),
            pl.BlockSpec((tf, d), lambda g, i, f: (f, 0)),
            pl.BlockSpec((1, d), lambda g, i, f: (0, 0)),
        ],
        out_specs=pl.BlockSpec((None, tm, d), lambda g, i, f: (g, i, 0)),
        out_shape=jax.ShapeDtypeStruct((g_, r, d), F32),
        scratch_shapes=[pltpu.VMEM((tm, d), BF16), pltpu.VMEM((tm, d), F32)],
        compiler_params=_cparams(("parallel", "parallel", "arbitrary"), 48),
        name="ffn",
    )(x, sc, sh, g2, gamma, w_up, w_down, norm_final)


def _rope_tables(q0, t):
    d = RET_D
    inv = 1.0 / (ROPE_BASE ** (jnp.arange(0, d, 2, dtype=F32) / d))
    ang = (q0 + jnp.arange(t)).astype(F32)[:, None] * inv[None, :]
    cos, sin = jnp.cos(ang), jnp.sin(ang)
    return jnp.concatenate([cos, cos], -1), jnp.concatenate([-sin, sin], -1)


def _prep_layer_weights(l, w):
    w_in = w["w_in"][l]
    d = w_in.shape[0]
    ret_cols, fox_qkv = 2048, 1536
    f0 = ret_cols + fox_qkv
    rw0 = f0 + FOX_H
    g0 = rw0 + RW_PROJ
    zpad = jnp.zeros((d, LANE - FOX_H), F32)
    packed = jnp.concatenate([
        w_in[:, rw0:g0], w_in[:, f0:rw0], zpad, jnp.zeros((d, LANE), F32),
        w_in[:, g0:], w_in[:, ret_cols:f0], w_in[:, :ret_cols]], axis=1).astype(BF16)
    zl = jnp.zeros((RW_LORA // 2, RW_W), F32)
    w2a2 = jnp.concatenate([jnp.concatenate([w["rw_w2"][l], zl], 1),
                            jnp.concatenate([zl, w["rw_a2"][l]], 1)], 0)
    row = lambda a: a.reshape(1, -1)
    return {
        "w_in": packed,
        "norm_mix": row(w["norm_mix"][l]), "norm_ffn": row(w["norm_ffn"][l]),
        "bf": jnp.pad(w["fox_bf"][l], (0, LANE - FOX_H)).reshape(1, LANE),
        "mu": row(w["rw_mu"][l]), "w2a2": w2a2,
        "w0a0": jnp.concatenate([w["rw_w0"][l], w["rw_a0"][l]]).reshape(1, -1),
        "g2": w["rw_g2"][l].astype(BF16),
        "kk": row(w["rw_kk"][l]), "ka": row(w["rw_ka"][l]), "rk": row(w["rw_rk"][l]),
        "lnw": row(w["rw_lnw"][l]), "lnb": row(w["rw_lnb"][l]),
        "w_br_ret": w["w_br_ret"][l].astype(BF16), "w_br_fox": w["w_br_fox"][l].astype(BF16),
        "w_br_rw": w["w_br_rw"][l].astype(BF16), "w_out": w["w_out"][l].astype(BF16),
        "w_up": w["w_up"][l].astype(BF16), "w_down": w["w_down"][l].astype(BF16),
    }


def _ret_log_gamma_table():
    lg = jnp.log1p(-jnp.exp2(-5.0 - jnp.arange(RET_H, dtype=F32)))
    return jnp.broadcast_to(lg[:, None, None], (RET_H, 1, RET_D))


def _layer(x, mod, q0, past, ret0, wkv0, shift0, wl, norm_final, final, flat):
    b, t, d = x.shape
    sh1, sc1, g1, sh2, sc2, g2 = [m[:, None, :] for m in jnp.split(mod, 6, axis=-1)]
    if flat:
        rep = lambda m: jnp.repeat(m, t, axis=1).reshape(1, b * t, d)
        sh1, sc1, g1, sh2, sc2, g2 = [rep(m) for m in (sh1, sc1, g1, sh2, sc2, g2)]
        xd = x.reshape(1, b * t, d)
    else:
        xd = x
    proj_d = _norm_mm(xd, sc1, sh1, wl["norm_mix"], wl["w_in"])
    proj = proj_d.reshape(b, t, PROJ_W)

    cosf, sinf = _rope_tables(q0, t)
    o_ret, r_new = _retention(proj, cosf, sinf, _ret_log_gamma_table(), ret0)

    lf, cum = _fox_gate(proj_d, wl["bf"])
    lf = lf.reshape(b, t, LANE)[:, :, :FOX_H]
    if past is None:
        cum = cum.reshape(b, t, LANE)[:, :, :FOX_H]
        o_fox = _fox_prompt(proj, jnp.swapaxes(cum, 1, 2)[..., None])
    else:
        layer, page_table, cache_k, cache_v, cache_lf_t = past
        page = cache_k.shape[-1]
        lfn_t =jnp.pad(jnp.swapaxes(lf, 1, 2), ((0, 0), (0, 0), (0, page - t)))
        o_fox = _fox_sample(layer, proj, lfn_t, page_table, cache_k, cache_v, cache_lf_t)

    o_rw, s_new = _rwkv(proj, shift0[:, None, :], wkv0, wl, t)

    dn = lambda a: a.reshape(xd.shape[0], xd.shape[1], -1)
    x1 = _merge(xd, g1, dn(o_ret), dn(o_fox), dn(o_rw), proj_d, wl)
    x2 = _ffn(x1, sc2, sh2, g2, wl["norm_ffn"], wl["w_up"], wl["w_down"], norm_final, final)
    fk = proj[:, :, C_FK:C_FV].reshape(b, t, FOX_H, FOX_DH)
    fv = proj[:, :, C_FV:C_RQ].reshape(b, t, FOX_H, FOX_DH)
    shift = proj[:, t - 1, C_RW:C_RW + RW_PROJ]
    return x2.reshape(b, t, d), (fk, fv, lf, r_new, s_new, shift)


def kernel(x_prompt, x_sample, c_prompt, c_sample, cache_k, cache_v, cache_logf, page_table, state_ret, state_wkv, state_shift, ada_w, ada_b, norm_mix, norm_ffn, w_in, fox_bf, rw_mu, rw_w0, rw_w2, rw_a0, rw_a2, rw_g2, rw_kk, rw_ka, rw_rk, rw_lnw, rw_lnb, w_br_ret, w_br_fox, w_br_rw, w_out, w_up, w_down, norm_final):
    depth = w_in.shape[0]
    b, t, d = x_prompt.shape
    db, ts, _ = x_sample.shape
    n_pool, page = cache_k.shape[1], cache_k.shape[2]
    past_len = page_table.shape[1] * page
    w = dict(w_in=w_in, norm_mix=norm_mix, norm_ffn=norm_ffn, fox_bf=fox_bf, rw_mu=rw_mu, rw_w0=rw_w0,
             rw_w2=rw_w2, rw_a0=rw_a0, rw_a2=rw_a2, rw_g2=rw_g2, rw_kk=rw_kk, rw_ka=rw_ka, rw_rk=rw_rk,
             rw_lnw=rw_lnw, rw_lnb=rw_lnb, w_br_ret=w_br_ret, w_br_fox=w_br_fox, w_br_rw=w_br_rw,
             w_out=w_out, w_up=w_up, w_down=w_down)
    rows = b + db
    rows_p = -(-rows // 8) * 8
    c_all = jnp.pad(jnp.concatenate([c_prompt, c_sample], 0), ((0, rows_p - rows), (0, 0)))
    mod = _modulation(c_all, ada_w, ada_b)
    ck = jnp.transpose(cache_k, (0, 1, 3, 4, 2))
    cv = jnp.transpose(cache_v, (0, 1, 3, 4, 2))
    clf_t = jnp.swapaxes(cache_logf, 2, 3)
    nf = norm_final.reshape(1, d)
    xp, xs = x_prompt, x_sample
    st_p, st_s = [], []
    for l in range(depth):
        wl = _prep_layer_weights(l, w)
        final = l == depth - 1
        xp, sp = _layer(xp, mod[l, :b], 0, None, jnp.zeros((b, RET_H, RET_D, RET_D), F32),
                        jnp.zeros((b, RW_H, RW_N, RW_N), F32), jnp.zeros((b, RW_PROJ), F32),
                        wl, nf, final, False)
        st_p.append(sp)
        xs, ss = _layer(xs, mod[l, b:rows], past_len, (l, page_table, ck, cv, clf_t), state_ret[l],
                        state_wkv[l], state_shift[l], wl, nf, final, True)
        st_s.append(ss)
    outs = [xp, xs]
    for st in (st_p, st_s):
        for i in range(6):
            outs.append(jnp.stack([s[i] for s in st]))
    y_p, y_s, k_p, v_p, lf_p, ret_p, wkv_p, sh_p, k_s, v_s, lf_s, ret_s, wkv_s, sh_s = outs
    return (y_p, y_s, k_p, v_p, lf_p, ret_p, wkv_p, sh_p, k_s, v_s, lf_s, ret_s, wkv_s, sh_s)
```

```python
import functools

import jax
import jax.numpy as jnp
from jax import lax
from jax.experimental import pallas as pl
from jax.experimental.pallas import tpu as pltpu

F32 = jnp.float32
BF16 = jnp.bfloat16
HIGHEST = lax.Precision.HIGHEST

RET_H, RET_D = 4, 128
FOX_H, FOX_DH = 8, 64
RW_H, RW_N = 8, 64
RW_W = RW_H * RW_N
RW_LORA = 128
RW_GATE_R = 128
RW_PROJ = 3 * RW_W + RW_LORA + RW_GATE_R
RET_GN_EPS = 1e-6
RW_GN_EPS = 64e-5
NORM_EPS = 1e-6
ROPE_BASE = 10000.0
NEG = -1e30
LOG2E = 1.4426950408889634

LANE = 128
C_RW = 0
C_F = 1792
C_GATE = 2048
C_FQ, C_FK, C_FV = 5120, 5632, 6144
C_RQ, C_RK, C_RV, C_RG = 6656, 7168, 7680, 8192
PROJ_W = 8704

RW_CHUNK = 64
RET_CHUNK = 128


def _cparams(sem, vmem_mb):
    return pltpu.CompilerParams(dimension_semantics=sem, vmem_limit_bytes=vmem_mb << 20)


def _dot(a, b):
    return jnp.dot(a, b, preferred_element_type=F32)


def _dot_nt(a, b):
    return lax.dot_general(a, b, (((1,), (1,)), ((), ())), preferred_element_type=F32)


def _dot_tn(a, b):
    return lax.dot_general(a, b, (((0,), (0,)), ((), ())), preferred_element_type=F32)


def _split3(x):
    hi = x.astype(BF16)
    r1 = x - hi.astype(F32)
    mid = r1.astype(BF16)
    lo = (r1 - mid.astype(F32)).astype(BF16)
    return hi, mid, lo


def _dot3_l(m01, x):
    hi, mid, lo = _split3(x)
    return _dot(m01, hi) + _dot(m01, mid) + _dot(m01, lo)


def _dot3_r(x, m01):
    hi, mid, lo = _split3(x)
    return _dot(hi, m01) + _dot(mid, m01) + _dot(lo, m01)


def _sigmoid(x):
    return 1.0 / (1.0 + jnp.exp(-x))


def _softplus(x):
    return jnp.maximum(x, 0.0) + jnp.log(1.0 + jnp.exp(-jnp.abs(x)))


def _pad_rows(a, rows):
    if a.shape[0] == rows:
        return a
    return jnp.concatenate([a, jnp.zeros((rows - a.shape[0],) + a.shape[1:], a.dtype)], axis=0)


def _mod_kernel(c_ref, w_ref, b_ref, o_ref):
    c = c_ref[...]
    s = c * _sigmoid(c)
    o_ref[...] = jnp.dot(s, w_ref[...], precision=HIGHEST, preferred_element_type=F32) + b_ref[...]


def _modulation(c_all, ada_w, ada_b):
    depth, d, n = ada_w.shape
    rows = c_all.shape[0]
    tn = 1536
    return pl.pallas_call(
        _mod_kernel,
        grid=(depth, n // tn),
        in_specs=[
            pl.BlockSpec((rows, d), lambda l, j: (0, 0)),
            pl.BlockSpec((None, d, tn), lambda l, j: (l, 0, j)),
            pl.BlockSpec((None, 1, tn), lambda l, j: (l, 0, j)),
        ],
        out_specs=pl.BlockSpec((None, rows, tn), lambda l, j: (l, 0, j)),
        out_shape=jax.ShapeDtypeStruct((depth, rows, n), F32),
        compiler_params=_cparams(("parallel", "parallel"), 40),
        name="ada_mod",
    )(c_all, ada_w, ada_b.reshape(depth, 1, n))


def _norm_mod(x, sc, sh, g):
    ms = jnp.mean(x * x, axis=-1, keepdims=True)
    y = x * lax.rsqrt(ms + NORM_EPS) * g
    return y * (1.0 + sc) + sh


def _norm_mm_kernel(x_ref, sc_ref, sh_ref, g_ref, w_ref, o_ref, h_ref):
    @pl.when(pl.program_id(2) == 0)
    def _():
        h_ref[...] = _norm_mod(x_ref[...], sc_ref[...], sh_ref[...], g_ref[...]).astype(BF16)

    o_ref[...] = _dot(h_ref[...], w_ref[...])


def _row_spec(rm, tm, d):
    if rm == 1:
        return pl.BlockSpec((None, 1, d), lambda g, i, *_: (g, 0, 0))
    return pl.BlockSpec((None, tm, d), lambda g, i, *_: (g, i, 0))


def _norm_mm(x, sc, sh, gamma, w):
    g_, r, d = x.shape
    n = w.shape[1]
    tm = min(r, 2048)
    tn = 512
    rm = sc.shape[1]
    return pl.pallas_call(
        _norm_mm_kernel,
        grid=(g_, r // tm, n // tn),
        in_specs=[
            pl.BlockSpec((None, tm, d), lambda g, i, j: (g, i, 0)),
            _row_spec(rm, tm, d),
            _row_spec(rm, tm, d),
            pl.BlockSpec((1, d), lambda g, i, j: (0, 0)),
            pl.BlockSpec((d, tn), lambda g, i, j: (0, j)),
        ],
        out_specs=pl.BlockSpec((None, tm, tn), lambda g, i, j: (g, i, j)),
        out_shape=jax.ShapeDtypeStruct((g_, r, n), F32),
        scratch_shapes=[pltpu.VMEM((tm, d), BF16)],
        compiler_params=_cparams(("parallel", "parallel", "arbitrary"), 56),
        name="norm_in_proj",
    )(x, sc, sh, gamma, w)


def _fox_gate_kernel(f_ref, b_ref, tri_ref, lf_ref, cum_ref, carry_ref):
    @pl.when(pl.program_id(1) == 0)
    def _():
        carry_ref[...] = jnp.zeros_like(carry_ref)

    z = f_ref[...] + b_ref[...]
    lf = -_softplus(-z)
    lf_ref[...] = lf
    cum = _dot3_l(tri_ref[...], lf) + carry_ref[...]
    tb = cum.shape[0]
    carry_ref[...] = cum[tb - 1:tb, :]
    cum2 = cum * LOG2E
    for h in range(FOX_H):
        cum_ref[h] = cum2[:, h:h + 1]


def _fox_gate(proj, bf_pad):
    g_, r, _ = proj.shape
    tb = min(r, 512)
    tri = jnp.tril(jnp.ones((tb, tb), F32)).astype(BF16)
    return pl.pallas_call(
        _fox_gate_kernel,
        grid=(g_, r // tb),
        in_specs=[
            pl.BlockSpec((None, tb, LANE), lambda g, i: (g, i, C_F // LANE)),
            pl.BlockSpec((1, LANE), lambda g, i: (0, 0)),
            pl.BlockSpec((tb, tb), lambda g, i: (0, 0)),
        ],
        out_specs=[
            pl.BlockSpec((None, tb, LANE), lambda g, i: (g, i, 0)),
            pl.BlockSpec((None, FOX_H, tb, 1), lambda g, i: (g, 0, i, 0)),
        ],
        out_shape=[jax.ShapeDtypeStruct((g_, r, LANE), F32),
                   jax.ShapeDtypeStruct((g_, FOX_H, r, 1), F32)],
        scratch_shapes=[pltpu.VMEM((1, LANE), F32)],
        compiler_params=_cparams(("parallel", "arbitrary"), 32),
        name="fox_gate",
    )(proj, bf_pad, tri)


def _ret_kernel(q_ref, k_ref, v_ref, g_ref, cos_ref, sin_ref, lg_ref, r0_ref,
                o_ref, rout_ref, r_sc, *, ce, nb):
    cp = RET_CHUNK
    c = pl.program_id(1)
    heads = range(RET_H)

    @pl.when(c == 0)
    def _():
        r_sc[...] = r0_ref[...]

    rows = max(nb * ce, cp)
    cosf = _pad_rows(cos_ref[...], rows)
    sinf = _pad_rows(sin_ref[...], rows)

    def rot(x):
        return x * cosf + pltpu.roll(x, RET_D // 2, axis=1) * sinf

    row = lax.broadcasted_iota(jnp.int32, (cp, RET_D), 0).astype(F32)
    ri = lax.broadcasted_iota(jnp.int32, (cp, cp), 0)
    ci = lax.broadcasted_iota(jnp.int32, (cp, cp), 1)
    diff = (ri - ci).astype(F32)
    lss = [slice(h * RET_D, (h + 1) * RET_D) for h in heads]
    q = [rot(_pad_rows(q_ref[:, ls], rows)).astype(BF16) for ls in lss]
    kf = [rot(_pad_rows(k_ref[:, ls], rows)) * (RET_D ** -0.5) for ls in lss]
    vb = [_pad_rows(v_ref[:, ls], rows).astype(BF16) for ls in lss]
    lgs = [lg_ref[h] for h in heads]
    qdec = [jnp.exp(lg * (row + 1.0)) for lg in lgs]
    kdec = [jnp.exp(lg * (ce - 1.0 - row)) for lg in lgs]
    dmask = [jnp.where(diff >= 0.0, jnp.exp(lg[:, :1] * jnp.maximum(diff, 0.0)), 0.0) for lg in lgs]
    gdec = [jnp.exp(lg[:, :1] * float(ce)) for lg in lgs]

    r_cur = [r_sc[h] for h in heads]
    outs = [[] for _ in heads]
    for j in range(nb):
        rs = slice(j * cp, (j + 1) * cp)
        s = [_dot_nt(q[h][rs], kf[h][rs].astype(BF16)) * dmask[h] for h in heads]
        cross = [_dot(q[h][rs], r_cur[h].astype(BF16)) * qdec[h] for h in heads]
        upd = [_dot_tn((kf[h][rs] * kdec[h]).astype(BF16), vb[h][rs]) for h in heads]
        inner = [_dot(s[h].astype(BF16), vb[h][rs]) for h in heads]
        for h in heads:
            r_cur[h] = gdec[h] * r_cur[h] + upd[h]
            outs[h].append(inner[h] + cross[h])
    for h in heads:
        r_sc[h] = r_cur[h]
        o = outs[h][0] if nb == 1 else jnp.concatenate(outs[h], axis=0)
        mu = jnp.mean(o, axis=-1, keepdims=True)
        oc = o - mu
        var = jnp.mean(oc * oc, axis=-1, keepdims=True)
        gg = _pad_rows(g_ref[:, lss[h]], rows)
        out = oc * lax.rsqrt(var + RET_GN_EPS) * (gg * _sigmoid(gg))
        o_ref[:, lss[h]] = out[:nb * ce]

    @pl.when(c == pl.num_programs(1) - 1)
    def _():
        for h in heads:
            rout_ref[h] = r_cur[h]


def _retention(proj, cosf, sinf, lg_tab, r0):
    b, t, _ = proj.shape
    ce = min(RET_CHUNK, t)
    n = t // ce
    nb = 4 if (ce == RET_CHUNK and n % 4 == 0) else 1
    tr = nb * ce

    w = RET_H * RET_D

    def col(c0):
        return pl.BlockSpec((None, tr, w), lambda bi, c, c0=c0: (bi, c, c0 // w))

    state = pl.BlockSpec((None, RET_H, RET_D, RET_D), lambda bi, c: (bi, 0, 0, 0))
    return pl.pallas_call(
        functools.partial(_ret_kernel, ce=ce, nb=nb),
        grid=(b, n // nb),
        in_specs=[
            col(C_RQ), col(C_RK), col(C_RV), col(C_RG),
            pl.BlockSpec((tr, RET_D), lambda bi, c: (c, 0)),
            pl.BlockSpec((tr, RET_D), lambda bi, c: (c, 0)),
            pl.BlockSpec((RET_H, 1, RET_D), lambda bi, c: (0, 0, 0)),
            state,
        ],
        out_specs=[pl.BlockSpec((None, tr, w), lambda bi, c: (bi, c, 0)), state],
        out_shape=[
            jax.ShapeDtypeStruct((b, t, w), F32),
            jax.ShapeDtypeStruct((b, RET_H, RET_D, RET_D), F32),
        ],
        scratch_shapes=[pltpu.VMEM((RET_H, RET_D, RET_D), F32)],
        compiler_params=_cparams(("parallel", "arbitrary"), 40),
        name="retention",
    )(proj, proj, proj, proj, cosf, sinf, lg_tab, r0)


def _fox_kernel(qi_ref, ki_ref, q_ref, k_ref, v_ref, ck_ref, o_ref, qt_sc, m_sc, l_sc, acc_sc, *, tq, tk):
    qi = qi_ref[pl.program_id(2)]
    ki = ki_ref[pl.program_id(2)]
    head0 = lax.broadcasted_iota(jnp.int32, (LANE, 1), 0) < FOX_DH

    @pl.when(ki == 0)
    def _():
        q = q_ref[...] * (FOX_DH ** -0.5 * LOG2E)
        qt_sc[...] = q.T.astype(BF16)
        m_sc[...] = jnp.full_like(m_sc, NEG)
        l_sc[...] = jnp.zeros_like(l_sc)
        acc_sc[...] = jnp.zeros_like(acc_sc)

    def step(diagonal):
        kb = k_ref[...].astype(BF16)
        vt = v_ref[...].T.astype(BF16)
        qt = qt_sc[...]
        if diagonal:
            key = lax.broadcasted_iota(jnp.int32, (tk, tq), 0)
            qry = lax.broadcasted_iota(jnp.int32, (tk, tq), 1)
            causal = key <= qry
        for h in range(2):
            hs = slice(h * FOX_DH, (h + 1) * FOX_DH)
            qth = jnp.where(head0 if h == 0 else jnp.logical_not(head0), qt, jnp.zeros_like(qt))
            s = _dot(kb, qth) - ck_ref[h]
            if diagonal:
                s = jnp.where(causal, s, NEG)
            m_prev = m_sc[h:h + 1, :]
            m_new = jnp.maximum(m_prev, jnp.max(s, axis=0, keepdims=True))
            alpha = jnp.exp2(m_prev - m_new)
            p = jnp.exp2(s - m_new)
            l_sc[h:h + 1, :] = alpha * l_sc[h:h + 1, :] + jnp.sum(p, axis=0, keepdims=True)
            m_sc[h:h + 1, :] = m_new
            acc_sc[hs, :] = alpha * acc_sc[hs, :] + _dot(vt[hs, :], p.astype(BF16))

    @pl.when(ki < qi)
    def _():
        step(False)

    @pl.when(ki == qi)
    def _():
        step(True)
        linv = jnp.where(head0, 1.0 / l_sc[0:1, :], 1.0 / l_sc[1:2, :])
        o_ref[...] = (acc_sc[...] * linv).T


def _fox_prompt(proj, cum_c):
    b, t, _ = proj.shape
    tq = tk = min(t, 512)
    nq = t // tq
    hp = FOX_H // 2
    pairs = [(qi, ki) for qi in range(nq) for ki in range(qi + 1)]
    qi_tab = jnp.array([p_[0] for p_ in pairs], jnp.int32)
    ki_tab = jnp.array([p_[1] for p_ in pairs], jnp.int32)
    grid_spec = pltpu.PrefetchScalarGridSpec(
        num_scalar_prefetch=2,
        grid=(b, hp, len(pairs)),
        in_specs=[
            pl.BlockSpec((None, tq, LANE), lambda bi, p, s, qt, kt: (bi, qt[s], C_FQ // LANE + p)),
            pl.BlockSpec((None, tk, LANE), lambda bi, p, s, qt, kt: (bi, kt[s], C_FK // LANE + p)),
            pl.BlockSpec((None, tk, LANE), lambda bi, p, s, qt, kt: (bi, kt[s], C_FV // LANE + p)),
            pl.BlockSpec((None, 2, tk, 1), lambda bi, p, s, qt, kt: (bi, p, kt[s], 0)),
        ],
        out_specs=pl.BlockSpec((None, tq, LANE), lambda bi, p, s, qt, kt: (bi, qt[s], p)),
        scratch_shapes=[
            pltpu.VMEM((LANE, tq), BF16),
            pltpu.VMEM((2, tq), F32),
            pltpu.VMEM((2, tq), F32),
            pltpu.VMEM((LANE, tq), F32),
        ],
    )
    return pl.pallas_call(
        functools.partial(_fox_kernel, tq=tq, tk=tk),
        grid_spec=grid_spec,
        out_shape=jax.ShapeDtypeStruct((b, t, FOX_H * FOX_DH), F32),
        compiler_params=_cparams(("parallel", "parallel", "arbitrary"), 40),
        name="fox_prompt",
    )(qi_tab, ki_tab, proj, proj, proj, cum_c)


def _fox_dec_kernel(pt_ref, q_ref, kn_ref, vn_ref, lfn_ref, u_ref, pm_ref, *rest, pps, ts):
    k_refs = rest[:pps]
    v_refs = rest[pps:2 * pps]
    lf_refs = rest[2 * pps:3 * pps]
    o_ref, qbd, m_sc, l_sc, acc_sc, carry_sc = rest[3 * pps:]
    j = pl.program_id(1)
    rows = FOX_H * ts
    width = FOX_H * FOX_DH
    page = k_refs[0].shape[-1]
    assert ts & (ts - 1) == 0 and FOX_DH & (FOX_DH - 1) == 0
    rowh = lax.broadcasted_iota(jnp.int32, (rows, width), 0) >> (ts.bit_length() - 1)
    colh = lax.broadcasted_iota(jnp.int32, (rows, width), 1) >> (FOX_DH.bit_length() - 1)
    own = rowh == colh

    @pl.when(j == 0)
    def _():
        q = q_ref[...] * (FOX_DH ** -0.5 * LOG2E)
        qt = jnp.broadcast_to(q[None], (FOX_H, ts, width)).reshape(rows, width)
        qbd[...] = jnp.where(own, qt, 0.0).astype(BF16)
        m_sc[...] = jnp.full_like(m_sc, NEG)
        l_sc[...] = jnp.zeros_like(l_sc)
        acc_sc[...] = jnp.zeros_like(acc_sc)
        carry_sc[...] = jnp.zeros_like(carry_sc)

    def expand(cum):
        return jnp.broadcast_to(cum[:, None, :], (FOX_H, ts, page)).reshape(rows, page)

    def update(k_list, v_list, cum_list, valid):
        qb = qbd[...]
        s = [_dot(qb, kp.astype(BF16)) - expand(cm) * LOG2E for kp, cm in zip(k_list, cum_list)]
        s = s[0] if len(s) == 1 else jnp.concatenate(s, axis=1)
        if valid is not None:
            s = jnp.where(valid, s, NEG)
        m_prev = m_sc[...]
        m_new = jnp.maximum(m_prev, jnp.max(s, axis=-1, keepdims=True))
        alpha = jnp.exp2(m_prev - m_new)
        p = jnp.exp2(s - m_new)
        l_sc[...] = alpha * l_sc[...] + jnp.sum(p, axis=-1, keepdims=True)
        m_sc[...] = m_new
        pv = None
        for i, vp in enumerate(v_list):
            t_ = _dot_nt(p[:, i * page:(i + 1) * page].astype(BF16), vp.astype(BF16))
            pv = t_ if pv is None else pv + t_
        acc_sc[...] = alpha * acc_sc[...] + pv

    lf_all = jnp.concatenate([r[...] for r in lf_refs], axis=0) if pps > 1 else lf_refs[0][...]
    cin = _dot3_r(lf_all, u_ref[...])
    carry = carry_sc[...]
    cum = cin + jnp.concatenate([carry] * pps, axis=0) if pps > 1 else cin + carry
    if pps > 1:
        cum = cum + _dot3_l(pm_ref[...], cin)[:, page - 1:page]
    carry = cum[(pps - 1) * FOX_H:pps * FOX_H, page - 1:page]
    carry_sc[...] = carry
    update([r[...].reshape(width, page) for r in k_refs], [r[...].reshape(width, page) for r in v_refs],
           [cum[i * FOX_H:(i + 1) * FOX_H] for i in range(pps)], None)

    @pl.when(j == pl.num_programs(1) - 1)
    def _():
        cumn = _dot3_r(lfn_ref[...], u_ref[...]) + carry
        kn = _pad_rows(kn_ref[...], page).T
        vn = _pad_rows(vn_ref[...], page).T
        colk = lax.broadcasted_iota(jnp.int32, (rows, page), 1)
        trow = lax.broadcasted_iota(jnp.int32, (rows, page), 0) & (ts - 1)
        update([kn], [vn], [cumn], colk <= trow)
        accn = acc_sc[...] / l_sc[...]
        o_ref[...] = jnp.sum(jnp.where(own, accn, 0.0).reshape(FOX_H, ts, width), axis=0)


def _fox_sample(layer, proj, lfn_t, page_table, cache_k, cache_v, cache_lf_t):
    db, ts, _ = proj.shape
    npages = page_table.shape[1]
    page = cache_k.shape[-1]
    width = FOX_H * FOX_DH
    pps = 8 if npages % 8 == 0 else 1
    upper = jnp.triu(jnp.ones((page, page), F32)).astype(BF16)
    ridx = jnp.arange(pps * FOX_H)
    pmat = jnp.logical_and((ridx[:, None] % FOX_H) == (ridx[None, :] % FOX_H),
                           (ridx[None, :] // FOX_H) < (ridx[:, None] // FOX_H)).astype(BF16)
    rows = FOX_H * ts

    def page_spec(shape, p_):
        return pl.BlockSpec((None, None) + shape,
                            lambda bi, j, pt, p_=p_: (layer, pt[bi, j * pps + p_]) + (0,) * len(shape))

    in_specs = [
        pl.BlockSpec((None, ts, width), lambda bi, j, pt: (bi, 0, C_FQ // width)),
        pl.BlockSpec((None, ts, width), lambda bi, j, pt: (bi, 0, C_FK // width)),
        pl.BlockSpec((None, ts, width), lambda bi, j, pt: (bi, 0, C_FV // width)),
        pl.BlockSpec((None, FOX_H, page), lambda bi, j, pt: (bi, 0, 0)),
        pl.BlockSpec((page, page), lambda bi, j, pt: (0, 0)),
        pl.BlockSpec((pps * FOX_H, pps * FOX_H), lambda bi, j, pt: (0, 0)),
    ]
    in_specs += [page_spec((FOX_H, FOX_DH, page), p_) for p_ in range(pps)]
    in_specs += [page_spec((FOX_H, FOX_DH, page), p_) for p_ in range(pps)]
    in_specs += [page_spec((FOX_H, page), p_) for p_ in range(pps)]
    grid_spec = pltpu.PrefetchScalarGridSpec(
        num_scalar_prefetch=1,
        grid=(db, npages // pps),
        in_specs=in_specs,
        out_specs=pl.BlockSpec((None, ts, width), lambda bi, j, pt: (bi, 0, 0)),
        scratch_shapes=[
            pltpu.VMEM((rows, width), BF16),
            pltpu.VMEM((rows, 1), F32),
            pltpu.VMEM((rows, 1), F32),
            pltpu.VMEM((rows, width), F32),
            pltpu.VMEM((FOX_H, 1), F32),
        ],
    )
    args = [proj, proj, proj, lfn_t, upper, pmat] + [cache_k] * pps + [cache_v] * pps + [cache_lf_t] * pps
    return pl.pallas_call(
        functools.partial(_fox_dec_kernel, pps=pps, ts=ts),
        grid_spec=grid_spec,
        out_shape=jax.ShapeDtypeStruct((db, ts, width), F32),
        compiler_params=_cparams(("parallel", "arbitrary"), 40),
        name="fox_sample",
    )(page_table, *args)


def _rw_prep_kernel(p_ref, sh0_ref, mu_ref, w2_ref, w0_ref, g2_ref, kk_ref, ka_ref, rk_ref,
                    e_ref, bl_ref,
                    kt_o, bt_o, kq_o, rt_o, v_o, gc_o, ab_o, g_o, bon_o, carry, *, tr, tv):
    i = pl.program_id(1)
    trb = p_ref.shape[0]

    @pl.when(i == 0)
    def _():
        carry[...] = sh0_ref[...]

    p = _pad_rows(p_ref[...], tr)
    rowi = lax.broadcasted_iota(jnp.int32, (tr, 1), 0)
    prev = jnp.where(rowi == 0, carry[...], pltpu.roll(p, 1, axis=0))
    carry[...] = p_ref[pl.ds(trb - 1, 1), :]
    xm = p + (prev - p) * mu_ref[...]
    r = xm[:, 0:RW_W]
    k = xm[:, RW_W:2 * RW_W]
    v = xm[:, 2 * RW_W:3 * RW_W]
    z = xm[:, 3 * RW_W:3 * RW_W + RW_LORA]
    gl = xm[:, 3 * RW_W + RW_LORA:]
    lane = lax.broadcasted_iota(jnp.int32, (1, RW_LORA), 1)
    zz = jnp.where(lane < RW_LORA // 2, jnp.tanh(z), z)
    pre = jnp.dot(zz, w2_ref[...], precision=HIGHEST, preferred_element_type=F32) + w0_ref[...]
    w_log = -_softplus(-pre[:, :RW_W]) - 0.5
    logdec = -jnp.exp(w_log)
    a = _sigmoid(pre[:, RW_W:])
    g = _dot(_sigmoid(gl).astype(BF16), g2_ref[...])
    e = e_ref[...]
    kk = k * kk_ref[...]
    kk = kk / jnp.maximum(jnp.sqrt(_dot3_r(kk * kk, e)), 1e-12)
    k2 = k * (1.0 + (a - 1.0) * ka_ref[...])
    bonus = _dot3_r(r * k2 * rk_ref[...], e) * v
    if tv < tr:
        valid = rowi < tv
        logdec = jnp.where(valid, logdec, 0.0)
        kk = jnp.where(valid, kk, 0.0)
        k2 = jnp.where(valid, k2, 0.0)
        v = jnp.where(valid, v, 0.0)
        r = jnp.where(valid, r, 0.0)
    cl = _dot3_l(bl_ref[...], logdec)
    ct = jnp.concatenate(
        [jnp.broadcast_to(cl[c0 + RW_CHUNK - 1:c0 + RW_CHUNK, :], (RW_CHUNK, RW_W))
         for c0 in range(0, tr, RW_CHUNK)], axis=0)
    e_inv = jnp.exp(-cl)
    kt = kk * jnp.exp(cl - logdec)
    bt = kk * a * e_inv
    kt_o[...] = kt
    bt_o[...] = bt
    kq_o[...] = k2 * e_inv
    rt_o[...] = r * jnp.exp(cl)
    v_o[...] = v
    gc_o[...] = jnp.exp(ct)
    g_o[...] = g
    bon_o[...] = bonus
    lane2 = lax.broadcasted_iota(jnp.int32, (1, LANE), 1)
    for c in range(tr // RW_CHUNK):
        rs = slice(c * RW_CHUNK, (c + 1) * RW_CHUNK)
        for hp in range(RW_H // 2):
            ls = slice(hp * LANE, (hp + 1) * LANE)
            x = kt[rs, ls]
            y = bt[rs, ls].astype(BF16)
            for h in range(2):
                mh = (lane2 < RW_N) if h == 0 else (lane2 >= RW_N)
                xh = jnp.where(mh, x, 0.0).astype(BF16)
                ab_o[2 * hp + h, rs, :] = _dot_nt(xh, y)


def _rw_prep(proj, shift0, wl, tv):
    b, t, _ = proj.shape
    tp = max(t, RW_CHUNK)
    tr = min(tp, 256)
    trb = min(t, tr)
    n = tp // tr
    cidx = jnp.arange(tr) // RW_CHUNK
    same = cidx[:, None] == cidx[None, :]
    blk_tri = jnp.logical_and(same, jnp.arange(tr)[None, :] <= jnp.arange(tr)[:, None]).astype(BF16)
    hidx = jnp.arange(RW_W) // RW_N
    e512 = (hidx[:, None] == hidx[None, :]).astype(BF16)

    def const(shape):
        return pl.BlockSpec(shape, lambda bi, i: (0,) * len(shape))

    tok = pl.BlockSpec((None, tr, RW_W), lambda bi, i: (bi, i, 0))
    outs = [jax.ShapeDtypeStruct((b, tp, RW_W), F32)] * 6
    return pl.pallas_call(
        functools.partial(_rw_prep_kernel, tr=tr, tv=tv if t < tr else tr),
        grid=(b, n),
        in_specs=[
            pl.BlockSpec((None, trb, RW_PROJ), lambda bi, i: (bi, i, C_RW // RW_PROJ)),
            pl.BlockSpec((None, 1, RW_PROJ), lambda bi, i: (bi, 0, 0)),
            const((1, RW_PROJ)), const((RW_LORA, 2 * RW_W)), const((1, 2 * RW_W)),
            const((RW_GATE_R, RW_W)), const((1, RW_W)), const((1, RW_W)), const((1, RW_W)),
            const((RW_W, RW_W)), const((tr, tr)),
        ],
        out_specs=[tok] * 6 + [pl.BlockSpec((None, RW_H, tr, RW_N), lambda bi, i: (bi, 0, i, 0)), tok, tok],
        out_shape=outs + [jax.ShapeDtypeStruct((b, RW_H, tp, RW_N), F32)] + outs[:2],
        scratch_shapes=[pltpu.VMEM((1, RW_PROJ), F32)],
        compiler_params=_cparams(("parallel", "arbitrary"), 48),
        name="rwkv_prep",
    )(proj, shift0, wl["mu"], wl["w2a2"], wl["w0a0"], wl["g2"], wl["kk"], wl["ka"], wl["rk"],
      e512, blk_tri)


def _rw_solve_kernel(a_ref, t_ref, at_sc, tt_sc):
    c = RW_CHUNK
    per = LANE // c
    for jb in range(c * c // LANE):
        blk = a_ref[:, jb * LANE:(jb + 1) * LANE].T
        at_sc[jb * per:(jb + 1) * per] = blk.reshape(per, c, LANE)
    rows = lax.broadcasted_iota(jnp.int32, (c, LANE), 0)
    rb = 4

    def body_t(tb, carry_):
        t0 = tb * rb

        def body_u(u, accs):
            tu = tt_sc[u]
            return tuple(acc - at_sc[t0 + r, pl.ds(u, 1), :] * tu for r, acc in enumerate(accs))

        accs = lax.fori_loop(0, t0, body_u,
                             tuple(jnp.where(rows == t0 + r, 1.0, 0.0) for r in range(rb)))
        done = []
        for r in range(rb):
            acc = accs[r]
            for r2 in range(r):
                acc = acc - at_sc[t0 + r, pl.ds(t0 + r2, 1), :] * done[r2]
            tt_sc[t0 + r] = acc
            done.append(acc)
        return carry_

    lax.fori_loop(0, c // rb, body_t, 0)
    for jb in range(c * c // LANE):
        blk = tt_sc[jb * per:(jb + 1) * per].reshape(LANE, LANE)
        t_ref[:, jb * LANE:(jb + 1) * LANE] = blk.T


def _rw_solve(a2d):
    nch, cc = a2d.shape
    return pl.pallas_call(
        _rw_solve_kernel,
        grid=(nch // LANE,),
        in_specs=[pl.BlockSpec((LANE, cc), lambda i: (i, 0))],
        out_specs=pl.BlockSpec((LANE, cc), lambda i: (i, 0)),
        out_shape=jax.ShapeDtypeStruct((nch, cc), F32),
        scratch_shapes=[pltpu.VMEM((RW_CHUNK, RW_CHUNK, LANE), F32)] * 2,
        compiler_params=_cparams(("parallel",), 40),
        name="rwkv_solve",
    )(a2d)


def _rw_chain_kernel(kt_ref, bt_ref, kq_ref, rt_ref, v_ref, gc_ref, tm_ref, g_ref, bon_ref,
                     lnw_ref, lnb_ref, e_ref, h0_ref, o_ref, hout_ref,
                     h_sc, gm_sc, nm_sc, qe_sc, y_sc, *, tt, to):
    i = pl.program_id(1)
    cn = RW_CHUNK
    nck = tt // cn
    npair = RW_H // 2

    @pl.when(i == 0)
    def _():
        h_sc[...] = h0_ref[...]

    lane = lax.broadcasted_iota(jnp.int32, (1, LANE), 1)
    lo = lane < RW_N
    ri = lax.broadcasted_iota(jnp.int32, (cn, LANE), 0)
    ci = lax.broadcasted_iota(jnp.int32, (cn, LANE), 1) & (RW_N - 1)
    strict = ci < ri
    incl = ci <= ri
    rd = lax.broadcasted_iota(jnp.int32, (LANE, LANE), 0)
    cd = lax.broadcasted_iota(jnp.int32, (LANE, LANE), 1)
    eye = rd == cd
    same_head = (rd >> 6) == (cd >> 6)

    def stack(x):
        return jnp.concatenate([jnp.where(lo, x, 0.0), jnp.where(lo, 0.0, x)], axis=0).astype(BF16)

    units = [(c, p) for c in range(nck) for p in range(npair)]
    cs = range(len(units))
    rss = [slice(c * cn, (c + 1) * cn) for c, _ in units]
    lss = [slice(p * LANE, (p + 1) * LANE) for _, p in units]
    kt = [kt_ref[rs, ls] for rs, ls in zip(rss, lss)]
    bt = [bt_ref[rs, ls] for rs, ls in zip(rss, lss)]
    kq = [kq_ref[rs, ls] for rs, ls in zip(rss, lss)]
    rt = [rt_ref[rs, ls] for rs, ls in zip(rss, lss)]
    v = [v_ref[rs, ls] for rs, ls in zip(rss, lss)]
    gc = [gc_ref[rs, ls] for rs, ls in zip(rss, lss)]
    ks = [stack(kq[c]) for c in cs]
    vs = [stack(v[c]) for c in cs]
    tcat = [jnp.concatenate([tm_ref[2 * p, rs, :], tm_ref[2 * p + 1, rs, :]], axis=1).astype(BF16)
            for (_, p), rs in zip(units, rss)]
    ak = [jnp.where(strict, _dot_nt(kt[c].astype(BF16), ks[c]), 0.0).astype(BF16) for c in cs]
    kd = [_dot(tcat[c], stack(kt[c])) for c in cs]
    bb = [jnp.where(incl, _dot_nt(rt[c].astype(BF16), stack(bt[c])), 0.0).astype(BF16) for c in cs]
    bk = [jnp.where(incl, _dot_nt(rt[c].astype(BF16), ks[c]), 0.0).astype(BF16) for c in cs]
    akv = [_dot(ak[c], vs[c]) for c in cs]
    vd = [_dot(tcat[c], stack(akv[c])) for c in cs]
    bh = [(bt[c] * gc[c]).astype(BF16) for c in cs]
    for c in cs:
        gm_sc[c] = (jnp.where(eye, gc[c][0:1, :], 0.0)
                    - jnp.where(same_head, _dot_tn(bh[c], kd[c].astype(BF16)), 0.0))
        qe_sc[rss[c], lss[c]] = rt[c] - _dot(bb[c], stack(kd[c]))
    for c in cs:
        kh = (kq[c] * gc[c]).astype(BF16)
        nm_sc[c] = jnp.where(same_head, _dot_tn(
            jnp.concatenate([kh, bh[c]], axis=0),
            jnp.concatenate([v[c].astype(BF16), (-vd[c]).astype(BF16)], axis=0)), 0.0)
        y_sc[rss[c], lss[c]] = _dot(jnp.concatenate([bk[c], bb[c]], axis=1),
                                    jnp.concatenate([vs[c], stack(-vd[c])], axis=0))

    hcur = [h_sc[p] for p in range(npair)]
    for u, (c, p) in enumerate(units):
        hb = hcur[p].astype(BF16)
        y_sc[rss[u], lss[u]] = y_sc[rss[u], lss[u]] + _dot(qe_sc[rss[u], lss[u]].astype(BF16), hb)
        hcur[p] = _dot(gm_sc[u].astype(BF16), hb) + nm_sc[u]
    for p in range(npair):
        h_sc[p] = hcur[p]

    e = e_ref[...]
    for p in range(npair):
        ls = slice(p * LANE, (p + 1) * LANE)
        y = y_sc[:, ls]
        mean = _dot3_r(y, e) * (1.0 / RW_N)
        yc = y - mean
        var = _dot3_r(yc * yc, e) * (1.0 / RW_N)
        out = ((yc * lax.rsqrt(var + RW_GN_EPS) * lnw_ref[:, ls] + lnb_ref[:, ls] + bon_ref[:, ls])
               * g_ref[:, ls])
        o_ref[:, ls] = out[:to]

    @pl.when(i == pl.num_programs(1) - 1)
    def _():
        for p in range(npair):
            hout_ref[p] = hcur[p]


def _rw_chain(prep, tmat, h0, lnw, lnb, t_out):
    kt, bt, kq, rt, v, gc, _, g, bon = prep
    b, tp, _ = kt.shape
    tt = min(tp, 128)
    to = min(t_out, tt)
    hp = RW_H // 2
    hidx = jnp.arange(LANE) // RW_N
    e128 = (hidx[:, None] == hidx[None, :]).astype(BF16)
    tok = pl.BlockSpec((None, tt, RW_W), lambda bi, i: (bi, i, 0))
    vec = pl.BlockSpec((1, RW_W), lambda bi, i: (0, 0))
    st = pl.BlockSpec((None, hp, LANE, LANE), lambda bi, i: (bi, 0, 0, 0))
    nunit = tt // RW_CHUNK * hp
    return pl.pallas_call(
        functools.partial(_rw_chain_kernel, tt=tt, to=to),
        grid=(b, tp // tt),
        in_specs=[tok] * 6 + [
            pl.BlockSpec((None, RW_H, tt, RW_N), lambda bi, i: (bi, 0, i, 0)),
            tok, tok, vec, vec,
            pl.BlockSpec((LANE, LANE), lambda bi, i: (0, 0)),
            st,
        ],
        out_specs=[pl.BlockSpec((None, to, RW_W), lambda bi, i: (bi, i, 0)), st],
        out_shape=[
            jax.ShapeDtypeStruct((b, t_out, RW_W), F32),
            jax.ShapeDtypeStruct((b, hp, LANE, LANE), F32),
        ],
        scratch_shapes=[
            pltpu.VMEM((hp, LANE, LANE), F32),
            pltpu.VMEM((nunit, LANE, LANE), F32),
            pltpu.VMEM((nunit, LANE, LANE), F32),
            pltpu.VMEM((tt, RW_W), F32),
            pltpu.VMEM((tt, RW_W), F32),
        ],
        compiler_params=_cparams(("parallel", "arbitrary"), 40),
        name="rwkv_chain",
    )(kt, bt, kq, rt, v, gc, tmat, g, bon, lnw, lnb, e128, h0)


def _rwkv(proj, shift0, s0, wl, t):
    b = proj.shape[0]
    prep = _rw_prep(proj, shift0, wl, t)
    ab = prep[6]
    tp = ab.shape[2]
    cc = RW_CHUNK * RW_CHUNK
    a2d = ab.reshape(-1, cc)
    nch = a2d.shape[0]
    nch_p = -(-nch // LANE) * LANE
    if nch_p != nch:
        a2d = jnp.pad(a2d, ((0, nch_p - nch), (0, 0)))
    tmat = _rw_solve(a2d)[:nch].reshape(b, RW_H, tp, RW_N)
    st = jnp.swapaxes(s0, -1, -2).reshape(b, RW_H // 2, 2, RW_N, RW_N)
    z = jnp.zeros_like(st[:, :, 0])
    h0 = jnp.concatenate([jnp.concatenate([st[:, :, 0], z], -1), jnp.concatenate([z, st[:, :, 1]], -1)], -2)
    o, hout = _rw_chain(prep, tmat, h0, wl["lnw"], wl["lnb"], t)
    s_new = jnp.stack([hout[:, :, :RW_N, :RW_N], hout[:, :, RW_N:, RW_N:]], axis=2)
    s_new = jnp.swapaxes(s_new.reshape(b, RW_H, RW_N, RW_N), -1, -2)
    return o, s_new


def _merge_kernel(x_ref, gm_ref, oa_ref, ob_ref, oc_ref, ga_ref, gb_ref, gc_ref,
                  wa_ref, wb_ref, wc_ref, wo_ref, o_ref):
    merged = (_sigmoid(ga_ref[...]) * _dot(oa_ref[...].astype(BF16), wa_ref[...])
              + _sigmoid(gb_ref[...]) * _dot(ob_ref[...].astype(BF16), wb_ref[...])
              + _sigmoid(gc_ref[...]) * _dot(oc_ref[...].astype(BF16), wc_ref[...]))
    mix = _dot(merged.astype(BF16), wo_ref[...])
    o_ref[...] = x_ref[...] + gm_ref[...] * mix


def _merge(x, g1, o_ret, o_fox, o_rw, proj, wl):
    g_, r, d = x.shape
    tm = min(r, 512)
    rm = g1.shape[1]
    w = o_ret.shape[-1]

    def tok(width, cb=0):
        return pl.BlockSpec((None, tm, width), lambda g, i, cb=cb: (g, i, cb))

    def const(shape):
        return pl.BlockSpec(shape, lambda g, i: (0, 0))

    return pl.pallas_call(
        _merge_kernel,
        grid=(g_, r // tm),
        in_specs=[
            tok(d), _row_spec(rm, tm, d), tok(w), tok(w), tok(w),
            tok(d, C_GATE // d), tok(d, C_GATE // d + 1), tok(d, C_GATE // d + 2),
            const((w, d)), const((w, d)), const((w, d)), const((d, d)),
        ],
        out_specs=tok(d),
        out_shape=jax.ShapeDtypeStruct((g_, r, d), F32),
        compiler_params=_cparams(("parallel", "parallel"), 48),
        name="merge_out",
    )(x, g1, o_ret, o_fox, o_rw, proj, proj, proj, wl["w_br_ret"], wl["w_br_fox"], wl["w_br_rw"], wl["w_out"])


def _ffn_kernel(x_ref, sc_ref, sh_ref, gm_ref, g_ref, wu_ref, wd_ref, nf_ref, o_ref, h_sc, acc_sc, *, final):
    f = pl.program_id(2)

    @pl.when(f == 0)
    def _():
        h_sc[...] = _norm_mod(x_ref[...], sc_ref[...], sh_ref[...], g_ref[...]).astype(BF16)
        acc_sc[...] = jnp.zeros_like(acc_sc)

    u = jnp.maximum(_dot(h_sc[...], wu_ref[...]), 0.0)
    acc_sc[...] += _dot((u * u).astype(BF16), wd_ref[...])

    @pl.when(f == pl.num_programs(2) - 1)
    def _():
        xn = x_ref[...] + gm_ref[...] * acc_sc[...]
        if final:
            ms = jnp.mean(xn * xn, axis=-1, keepdims=True)
            xn = xn * lax.rsqrt(ms + NORM_EPS) * nf_ref[...]
        o_ref[...] = xn


def _ffn(x, sc, sh, g2, gamma, w_up, w_down, norm_final, final):
    g_, r, d = x.shape
    dff = w_up.shape[1]
    tm = min(r, 1024)
    tf = 1024
    rm = sc.shape[1]
    return pl.pallas_call(
        functools.partial(_ffn_kernel, final=final),
        grid=(g_, r // tm, dff // tf),
        in_specs=[
            pl.BlockSpec((None, tm, d), lambda g, i, f: (g, i, 0)),
            _row_spec(rm, tm, d), _row_spec(rm, tm, d), _row_spec(rm, tm, d),
            pl.BlockSpec((1, d), lambda g, i, f: (0, 0)),
            pl.BlockSpec((d, tf), lambda g, i, f: (0, f)),
            pl.BlockSpec((tf, d), lambda g, i, f: (f, 0)),
            pl.BlockSpec((1, d), lambda g, i, f: (0, 0)),
        ],
        out_specs=pl.BlockSpec((None, tm, d), lambda g, i, f: (g, i, 0)),
        out_shape=jax.ShapeDtypeStruct((g_, r, d), F32),
        scratch_shapes=[pltpu.VMEM((tm, d), BF16), pltpu.VMEM((tm, d), F32)],
        compiler_params=_cparams(("parallel", "parallel", "arbitrary"), 48),
        name="ffn",
    )(x, sc, sh, g2, gamma, w_up, w_down, norm_final)


def _rope_tables(q0, t):
    d = RET_D
    inv = 1.0 / (ROPE_BASE ** (jnp.arange(0, d, 2, dtype=F32) / d))
    ang = (q0 + jnp.arange(t)).astype(F32)[:, None] * inv[None, :]
    cos, sin = jnp.cos(ang), jnp.sin(ang)
    return jnp.concatenate([cos, cos], -1), jnp.concatenate([-sin, sin], -1)


def _prep_layer_weights(l, w):
    w_in = w["w_in"][l]
    d = w_in.shape[0]
    ret_cols, fox_qkv = 2048, 1536
    f0 = ret_cols + fox_qkv
    rw0 = f0 + FOX_H
    g0 = rw0 + RW_PROJ
    zpad = jnp.zeros((d, LANE - FOX_H), F32)
    packed = jnp.concatenate([
        w_in[:, rw0:g0], w_in[:, f0:rw0], zpad, jnp.zeros((d, LANE), F32),
        w_in[:, g0:], w_in[:, ret_cols:f0], w_in[:, :ret_cols]], axis=1).astype(BF16)
    zl = jnp.zeros((RW_LORA // 2, RW_W), F32)
    w2a2 = jnp.concatenate([jnp.concatenate([w["rw_w2"][l], zl], 1),
                            jnp.concatenate([zl, w["rw_a2"][l]], 1)], 0)
    row = lambda a: a.reshape(1, -1)
    return {
        "w_in": packed,
        "norm_mix": row(w["norm_mix"][l]), "norm_ffn": row(w["norm_ffn"][l]),
        "bf": jnp.pad(w["fox_bf"][l], (0, LANE - FOX_H)).reshape(1, LANE),
        "mu": row(w["rw_mu"][l]), "w2a2": w2a2,
        "w0a0": jnp.concatenate([w["rw_w0"][l], w["rw_a0"][l]]).reshape(1, -1),
        "g2": w["rw_g2"][l].astype(BF16),
        "kk": row(w["rw_kk"][l]), "ka": row(w["rw_ka"][l]), "rk": row(w["rw_rk"][l]),
        "lnw": row(w["rw_lnw"][l]), "lnb": row(w["rw_lnb"][l]),
        "w_br_ret": w["w_br_ret"][l].astype(BF16), "w_br_fox": w["w_br_fox"][l].astype(BF16),
        "w_br_rw": w["w_br_rw"][l].astype(BF16), "w_out": w["w_out"][l].astype(BF16),
        "w_up": w["w_up"][l].astype(BF16), "w_down": w["w_down"][l].astype(BF16),
    }


def _ret_log_gamma_table():
    lg = jnp.log1p(-jnp.exp2(-5.0 - jnp.arange(RET_H, dtype=F32)))
    return jnp.broadcast_to(lg[:, None, None], (RET_H, 1, RET_D))


def _layer(x, mod, q0, past, ret0, wkv0, shift0, wl, norm_final, final, flat):
    b, t, d = x.shape
    sh1, sc1, g1, sh2, sc2, g2 = [m[:, None, :] for m in jnp.split(mod, 6, axis=-1)]
    if flat:
        rep = lambda m: jnp.repeat(m, t, axis=1).reshape(1, b * t, d)
        sh1, sc1, g1, sh2, sc2, g2 = [rep(m) for m in (sh1, sc1, g1, sh2, sc2, g2)]
        xd = x.reshape(1, b * t, d)
    else:
        xd = x
    proj_d = _norm_mm(xd, sc1, sh1, wl["norm_mix"], wl["w_in"])
    proj = proj_d.reshape(b, t, PROJ_W)

    cosf, sinf = _rope_tables(q0, t)
    o_ret, r_new = _retention(proj, cosf, sinf, _ret_log_gamma_table(), ret0)

    lf, cum = _fox_gate(proj_d, wl["bf"])
    lf = lf.reshape(b, t, LANE)[:, :, :FOX_H]
    if past is None:
        o_fox = _fox_prompt(proj, cum)
    else:
        layer, page_table, cache_k, cache_v, cache_lf_t = past
        page = cache_k.shape[-1]
        lfn_t =jnp.pad(jnp.swapaxes(lf, 1, 2), ((0, 0), (0, 0), (0, page - t)))
        o_fox = _fox_sample(layer, proj, lfn_t, page_table, cache_k, cache_v, cache_lf_t)

    o_rw, s_new = _rwkv(proj, shift0[:, None, :], wkv0, wl, t)

    dn = lambda a: a.reshape(xd.shape[0], xd.shape[1], -1)
    x1 = _merge(xd, g1, dn(o_ret), dn(o_fox), dn(o_rw), proj_d, wl)
    x2 = _ffn(x1, sc2, sh2, g2, wl["norm_ffn"], wl["w_up"], wl["w_down"], norm_final, final)
    fk = proj[:, :, C_FK:C_FV].reshape(b, t, FOX_H, FOX_DH)
    fv = proj[:, :, C_FV:C_RQ].reshape(b, t, FOX_H, FOX_DH)
    shift = proj[:, t - 1, C_RW:C_RW + RW_PROJ]
    return x2.reshape(b, t, d), (fk, fv, lf, r_new, s_new, shift)


def kernel(x_prompt, x_sample, c_prompt, c_sample, cache_k, cache_v, cache_logf, page_table, state_ret, state_wkv, state_shift, ada_w, ada_b, norm_mix, norm_ffn, w_in, fox_bf, rw_mu, rw_w0, rw_w2, rw_a0, rw_a2, rw_g2, rw_kk, rw_ka, rw_rk, rw_lnw, rw_lnb, w_br_ret, w_br_fox, w_br_rw, w_out, w_up, w_down, norm_final):
    depth = w_in.shape[0]
    b, t, d = x_prompt.shape
    db, ts, _ = x_sample.shape
    n_pool, page = cache_k.shape[1], cache_k.shape[2]
    past_len = page_table.shape[1] * page
    w = dict(w_in=w_in, norm_mix=norm_mix, norm_ffn=norm_ffn, fox_bf=fox_bf, rw_mu=rw_mu, rw_w0=rw_w0,
             rw_w2=rw_w2, rw_a0=rw_a0, rw_a2=rw_a2, rw_g2=rw_g2, rw_kk=rw_kk, rw_ka=rw_ka, rw_rk=rw_rk,
             rw_lnw=rw_lnw, rw_lnb=rw_lnb, w_br_ret=w_br_ret, w_br_fox=w_br_fox, w_br_rw=w_br_rw,
             w_out=w_out, w_up=w_up, w_down=w_down)
    rows = b + db
    rows_p = -(-rows // 8) * 8
    c_all = jnp.pad(jnp.concatenate([c_prompt, c_sample], 0), ((0, rows_p - rows), (0, 0)))
    mod = _modulation(c_all, ada_w, ada_b)
    ck = jnp.transpose(cache_k, (0, 1, 3, 4, 2))
    cv = jnp.transpose(cache_v, (0, 1, 3, 4, 2))
    clf_t = jnp.swapaxes(cache_logf, 2, 3)
    nf = norm_final.reshape(1, d)
    xp, xs = x_prompt, x_sample
    st_p, st_s = [], []
    for l in range(depth):
        wl = _prep_layer_weights(l, w)
        final = l == depth - 1
        xp, sp = _layer(xp, mod[l, :b], 0, None, jnp.zeros((b, RET_H, RET_D, RET_D), F32),
                        jnp.zeros((b, RW_H, RW_N, RW_N), F32), jnp.zeros((b, RW_PROJ), F32),
                        wl, nf, final, False)
        st_p.append(sp)
        xs, ss = _layer(xs, mod[l, b:rows], past_len, (l, page_table, ck, cv, clf_t), state_ret[l],
                        state_wkv[l], state_shift[l], wl, nf, final, True)
        st_s.append(ss)
    outs = [xp, xs]
    for st in (st_p, st_s):
        for i in range(6):
            outs.append(jnp.stack([s[i] for s in st]))
    y_p, y_s, k_p, v_p, lf_p, ret_p, wkv_p, sh_p, k_s, v_s, lf_s, ret_s, wkv_s, sh_s = outs
    return (y_p, y_s, k_p, v_p, lf_p, ret_p, wkv_p, sh_p, k_s, v_s, lf_s, ret_s, wkv_s, sh_s)
```

```python
import functools

import jax
import jax.numpy as jnp
from jax import lax
from jax.experimental import pallas as pl
from jax.experimental.pallas import tpu as pltpu

F32 = jnp.float32
BF16 = jnp.bfloat16
HIGHEST = lax.Precision.HIGHEST

RET_H, RET_D = 4, 128
FOX_H, FOX_DH = 8, 64
RW_H, RW_N = 8, 64
RW_W = RW_H * RW_N
RW_LORA = 128
RW_GATE_R = 128
RW_PROJ = 3 * RW_W + RW_LORA + RW_GATE_R
RET_GN_EPS = 1e-6
RW_GN_EPS = 64e-5
NORM_EPS = 1e-6
ROPE_BASE = 10000.0
NEG = -1e30
LOG2E = 1.4426950408889634

LANE = 128
C_RW = 0
C_F = 1792
C_GATE = 2048
C_FQ, C_FK, C_FV = 5120, 5632, 6144
C_RQ, C_RK, C_RV, C_RG = 6656, 7168, 7680, 8192
PROJ_W = 8704

RW_CHUNK = 64
RET_CHUNK = 128


def _cparams(sem, vmem_mb):
    return pltpu.CompilerParams(dimension_semantics=sem, vmem_limit_bytes=vmem_mb << 20)


def _dot(a, b):
    return jnp.dot(a, b, preferred_element_type=F32)


def _dot_nt(a, b):
    return lax.dot_general(a, b, (((1,), (1,)), ((), ())), preferred_element_type=F32)


def _dot_tn(a, b):
    return lax.dot_general(a, b, (((0,), (0,)), ((), ())), preferred_element_type=F32)


def _split3(x):
    hi = x.astype(BF16)
    r1 = x - hi.astype(F32)
    mid = r1.astype(BF16)
    lo = (r1 - mid.astype(F32)).astype(BF16)
    return hi, mid, lo


def _dot3_l(m01, x):
    hi, mid, lo = _split3(x)
    return _dot(m01, hi) + _dot(m01, mid) + _dot(m01, lo)


def _dot3_r(x, m01):
    hi, mid, lo = _split3(x)
    return _dot(hi, m01) + _dot(mid, m01) + _dot(lo, m01)


def _sigmoid(x):
    return 1.0 / (1.0 + jnp.exp(-x))


def _softplus(x):
    return jnp.maximum(x, 0.0) + jnp.log(1.0 + jnp.exp(-jnp.abs(x)))


def _pad_rows(a, rows):
    if a.shape[0] == rows:
        return a
    return jnp.concatenate([a, jnp.zeros((rows - a.shape[0],) + a.shape[1:], a.dtype)], axis=0)


def _mod_kernel(c_ref, w_ref, b_ref, o_ref):
    c = c_ref[...]
    s = c * _sigmoid(c)
    o_ref[...] = jnp.dot(s, w_ref[...], precision=HIGHEST, preferred_element_type=F32) + b_ref[...]


def _modulation(c_all, ada_w, ada_b):
    depth, d, n = ada_w.shape
    rows = c_all.shape[0]
    tn = 1536
    return pl.pallas_call(
        _mod_kernel,
        grid=(depth, n // tn),
        in_specs=[
            pl.BlockSpec((rows, d), lambda l, j: (0, 0)),
            pl.BlockSpec((None, d, tn), lambda l, j: (l, 0, j)),
            pl.BlockSpec((None, 1, tn), lambda l, j: (l, 0, j)),
        ],
        out_specs=pl.BlockSpec((None, rows, tn), lambda l, j: (l, 0, j)),
        out_shape=jax.ShapeDtypeStruct((depth, rows, n), F32),
        compiler_params=_cparams(("parallel", "parallel"), 40),
        name="ada_mod",
    )(c_all, ada_w, ada_b.reshape(depth, 1, n))


def _norm_mod(x, sc, sh, g):
    ms = jnp.mean(x * x, axis=-1, keepdims=True)
    y = x * lax.rsqrt(ms + NORM_EPS) * g
    return y * (1.0 + sc) + sh


def _norm_mm_kernel(x_ref, sc_ref, sh_ref, g_ref, w_ref, o_ref, *rest, t_blocks):
    h_ref = rest[-1]
    j = pl.program_id(2)

    @pl.when(j == 0)
    def _():
        h_ref[...] = _norm_mod(x_ref[...], sc_ref[...], sh_ref[...], g_ref[...]).astype(BF16)

    y = _dot_nt(h_ref[...], w_ref[...])
    o_ref[...] = y
    if t_blocks is not None:
        @pl.when(jnp.logical_or(j == t_blocks[0], j == t_blocks[1]))
        def _():
            rest[0][...] = y.T


def _row_spec(rm, tm, d):
    if rm == 1:
        return pl.BlockSpec((None, 1, d), lambda g, i, *_: (g, 0, 0))
    return pl.BlockSpec((None, tm, d), lambda g, i, *_: (g, i, 0))


def _norm_mm(x, sc, sh, gamma, w_t, t_cols=None):
    g_, r, d = x.shape
    n = w_t.shape[0]
    tm = min(r, 2048)
    tn = 512
    rm = sc.shape[1]
    out_specs = [pl.BlockSpec((None, tm, tn), lambda g, i, j: (g, i, j))]
    out_shape = [jax.ShapeDtypeStruct((g_, r, n), F32)]
    t_blocks = None
    if t_cols is not None:
        t_blocks = (t_cols[0] // tn, t_cols[1] // tn)
        assert t_blocks[1] == t_blocks[0] + 1
        j0 = t_blocks[0]
        out_specs.append(pl.BlockSpec((None, None, tn, tm),
                                      lambda g, i, j: (g, jnp.clip(j - j0, 0, 1), 0, i)))
        out_shape.append(jax.ShapeDtypeStruct((g_, 2, tn, r), F32))
    return pl.pallas_call(
        functools.partial(_norm_mm_kernel, t_blocks=t_blocks),
        grid=(g_, r // tm, n // tn),
        in_specs=[
            pl.BlockSpec((None, tm, d), lambda g, i, j: (g, i, 0)),
            _row_spec(rm, tm, d),
            _row_spec(rm, tm, d),
            pl.BlockSpec((1, d), lambda g, i, j: (0, 0)),
            pl.BlockSpec((tn, d), lambda g, i, j: (j, 0)),
        ],
        out_specs=out_specs,
        out_shape=out_shape,
        scratch_shapes=[pltpu.VMEM((tm, d), BF16)],
        compiler_params=_cparams(("parallel", "parallel", "arbitrary"), 56),
        name="norm_in_proj",
    )(x, sc, sh, gamma, w_t)


def _fox_gate_kernel(f_ref, b_ref, tri_ref, lf_ref, cum_ref, carry_ref):
    @pl.when(pl.program_id(1) == 0)
    def _():
        carry_ref[...] = jnp.zeros_like(carry_ref)

    z = f_ref[...] + b_ref[...]
    lf = -_softplus(-z)
    lf_ref[...] = lf
    cum = _dot3_l(tri_ref[...], lf) + carry_ref[...]
    tb = cum.shape[0]
    carry_ref[...] = cum[tb - 1:tb, :]
    cum2 = cum * LOG2E
    for h in range(FOX_H):
        cum_ref[h] = cum2[:, h:h + 1]


def _fox_gate(proj, bf_pad):
    g_, r, _ = proj.shape
    tb = min(r, 512)
    tri = jnp.tril(jnp.ones((tb, tb), F32)).astype(BF16)
    return pl.pallas_call(
        _fox_gate_kernel,
        grid=(g_, r // tb),
        in_specs=[
            pl.BlockSpec((None, tb, LANE), lambda g, i: (g, i, C_F // LANE)),
            pl.BlockSpec((1, LANE), lambda g, i: (0, 0)),
            pl.BlockSpec((tb, tb), lambda g, i: (0, 0)),
        ],
        out_specs=[
            pl.BlockSpec((None, tb, LANE), lambda g, i: (g, i, 0)),
            pl.BlockSpec((None, FOX_H, tb, 1), lambda g, i: (g, 0, i, 0)),
        ],
        out_shape=[jax.ShapeDtypeStruct((g_, r, LANE), F32),
                   jax.ShapeDtypeStruct((g_, FOX_H, r, 1), F32)],
        scratch_shapes=[pltpu.VMEM((1, LANE), F32)],
        compiler_params=_cparams(("parallel", "arbitrary"), 32),
        name="fox_gate",
    )(proj, bf_pad, tri)


def _ret_kernel(q_ref, k_ref, v_ref, g_ref, cos_ref, sin_ref, lg_ref, r0_ref,
                o_ref, rout_ref, r_sc, *, ce, nb):
    cp = RET_CHUNK
    c = pl.program_id(1)
    heads = range(RET_H)

    @pl.when(c == 0)
    def _():
        r_sc[...] = r0_ref[...]

    rows = max(nb * ce, cp)
    cosf = _pad_rows(cos_ref[...], rows)
    sinf = _pad_rows(sin_ref[...], rows)

    def rot(x):
        return x * cosf + pltpu.roll(x, RET_D // 2, axis=1) * sinf

    row = lax.broadcasted_iota(jnp.int32, (cp, RET_D), 0).astype(F32)
    ri = lax.broadcasted_iota(jnp.int32, (cp, cp), 0)
    ci = lax.broadcasted_iota(jnp.int32, (cp, cp), 1)
    diff = (ri - ci).astype(F32)
    lss = [slice(h * RET_D, (h + 1) * RET_D) for h in heads]
    q = [rot(_pad_rows(q_ref[:, ls], rows)).astype(BF16) for ls in lss]
    kf = [rot(_pad_rows(k_ref[:, ls], rows)) * (RET_D ** -0.5) for ls in lss]
    vb = [_pad_rows(v_ref[:, ls], rows).astype(BF16) for ls in lss]
    lgs = [lg_ref[h] for h in heads]
    qdec = [jnp.exp(lg * (row + 1.0)) for lg in lgs]
    kdec = [jnp.exp(lg * (ce - 1.0 - row)) for lg in lgs]
    dmask = [jnp.where(diff >= 0.0, jnp.exp(lg[:, :1] * jnp.maximum(diff, 0.0)), 0.0) for lg in lgs]
    gdec = [jnp.exp(lg[:, :1] * float(ce)) for lg in lgs]

    r_cur = [r_sc[h] for h in heads]
    outs = [[] for _ in heads]
    for j in range(nb):
        rs = slice(j * cp, (j + 1) * cp)
        s = [_dot_nt(q[h][rs], kf[h][rs].astype(BF16)) * dmask[h] for h in heads]
        cross = [_dot(q[h][rs], r_cur[h].astype(BF16)) * qdec[h] for h in heads]
        upd = [_dot_tn((kf[h][rs] * kdec[h]).astype(BF16), vb[h][rs]) for h in heads]
        inner = [_dot(s[h].astype(BF16), vb[h][rs]) for h in heads]
        for h in heads:
            r_cur[h] = gdec[h] * r_cur[h] + upd[h]
            outs[h].append(inner[h] + cross[h])
    for h in heads:
        r_sc[h] = r_cur[h]
        o = outs[h][0] if nb == 1 else jnp.concatenate(outs[h], axis=0)
        mu = jnp.mean(o, axis=-1, keepdims=True)
        oc = o - mu
        var = jnp.mean(oc * oc, axis=-1, keepdims=True)
        gg = _pad_rows(g_ref[:, lss[h]], rows)
        out = oc * lax.rsqrt(var + RET_GN_EPS) * (gg * _sigmoid(gg))
        o_ref[:, lss[h]] = out[:nb * ce]

    @pl.when(c == pl.num_programs(1) - 1)
    def _():
        for h in heads:
            rout_ref[h] = r_cur[h]


def _retention(proj, cosf, sinf, lg_tab, r0):
    b, t, _ = proj.shape
    ce = min(RET_CHUNK, t)
    n = t // ce
    nb = 4 if (ce == RET_CHUNK and n % 4 == 0) else 1
    tr = nb * ce

    w = RET_H * RET_D

    def col(c0):
        return pl.BlockSpec((None, tr, w), lambda bi, c, c0=c0: (bi, c, c0 // w))

    state = pl.BlockSpec((None, RET_H, RET_D, RET_D), lambda bi, c: (bi, 0, 0, 0))
    return pl.pallas_call(
        functools.partial(_ret_kernel, ce=ce, nb=nb),
        grid=(b, n // nb),
        in_specs=[
            col(C_RQ), col(C_RK), col(C_RV), col(C_RG),
            pl.BlockSpec((tr, RET_D), lambda bi, c: (c, 0)),
            pl.BlockSpec((tr, RET_D), lambda bi, c: (c, 0)),
            pl.BlockSpec((RET_H, 1, RET_D), lambda bi, c: (0, 0, 0)),
            state,
        ],
        out_specs=[pl.BlockSpec((None, tr, w), lambda bi, c: (bi, c, 0)), state],
        out_shape=[
            jax.ShapeDtypeStruct((b, t, w), F32),
            jax.ShapeDtypeStruct((b, RET_H, RET_D, RET_D), F32),
        ],
        scratch_shapes=[pltpu.VMEM((RET_H, RET_D, RET_D), F32)],
        compiler_params=_cparams(("parallel", "arbitrary"), 40),
        name="retention",
    )(proj, proj, proj, proj, cosf, sinf, lg_tab, r0)


def _fox_kernel(qi_ref, ki_ref, q_ref, k_ref, v_ref, ck_ref, o_ref, qt_sc, m_sc, l_sc, acc_sc, *, tq, tk):
    qi = qi_ref[pl.program_id(2)]
    ki = ki_ref[pl.program_id(2)]
    head0 = lax.broadcasted_iota(jnp.int32, (LANE, 1), 0) < FOX_DH

    @pl.when(ki == 0)
    def _():
        q = q_ref[...] * (FOX_DH ** -0.5 * LOG2E)
        qt_sc[...] = q.T.astype(BF16)
        m_sc[...] = jnp.full_like(m_sc, NEG)
        l_sc[...] = jnp.zeros_like(l_sc)
        acc_sc[...] = jnp.zeros_like(acc_sc)

    def step(diagonal):
        kb = k_ref[...].astype(BF16)
        vt = v_ref[...].astype(BF16)
        qt = qt_sc[...]
        if diagonal:
            key = lax.broadcasted_iota(jnp.int32, (tk, tq), 0)
            qry = lax.broadcasted_iota(jnp.int32, (tk, tq), 1)
            causal = key <= qry
        for h in range(2):
            hs = slice(h * FOX_DH, (h + 1) * FOX_DH)
            qth = jnp.where(head0 if h == 0 else jnp.logical_not(head0), qt, jnp.zeros_like(qt))
            s = _dot(kb, qth) - ck_ref[h]
            if diagonal:
                s = jnp.where(causal, s, NEG)
            m_prev = m_sc[h:h + 1, :]
            m_new = jnp.maximum(m_prev, jnp.max(s, axis=0, keepdims=True))
            alpha = jnp.exp2(m_prev - m_new)
            p = jnp.exp2(s - m_new)
            l_sc[h:h + 1, :] = alpha * l_sc[h:h + 1, :] + jnp.sum(p, axis=0, keepdims=True)
            m_sc[h:h + 1, :] = m_new
            acc_sc[hs, :] = alpha * acc_sc[hs, :] + _dot(vt[hs, :], p.astype(BF16))

    @pl.when(ki < qi)
    def _():
        step(False)

    @pl.when(ki == qi)
    def _():
        step(True)
        linv = jnp.where(head0, 1.0 / l_sc[0:1, :], 1.0 / l_sc[1:2, :])
        o_ref[...] = (acc_sc[...] * linv).T


def _fox_prompt(proj, kv_t, cum_c):
    b, t, _ = proj.shape
    tq = tk = min(t, 512)
    nq = t // tq
    hp = FOX_H // 2
    pairs = [(qi, ki) for qi in range(nq) for ki in range(qi + 1)]
    qi_tab = jnp.array([p_[0] for p_ in pairs], jnp.int32)
    ki_tab = jnp.array([p_[1] for p_ in pairs], jnp.int32)
    grid_spec = pltpu.PrefetchScalarGridSpec(
        num_scalar_prefetch=2,
        grid=(b, hp, len(pairs)),
        in_specs=[
            pl.BlockSpec((None, tq, LANE), lambda bi, p, s, qt, kt: (bi, qt[s], C_FQ // LANE + p)),
            pl.BlockSpec((None, tk, LANE), lambda bi, p, s, qt, kt: (bi, kt[s], C_FK // LANE + p)),
            pl.BlockSpec((None, None, LANE, tk), lambda bi, p, s, qt, kt: (bi, 1, p, kt[s])),
            pl.BlockSpec((None, 2, tk, 1), lambda bi, p, s, qt, kt: (bi, p, kt[s], 0)),
        ],
        out_specs=pl.BlockSpec((None, tq, LANE), lambda bi, p, s, qt, kt: (bi, qt[s], p)),
        scratch_shapes=[
            pltpu.VMEM((LANE, tq), BF16),
            pltpu.VMEM((2, tq), F32),
            pltpu.VMEM((2, tq), F32),
            pltpu.VMEM((LANE, tq), F32),
        ],
    )
    return pl.pallas_call(
        functools.partial(_fox_kernel, tq=tq, tk=tk),
        grid_spec=grid_spec,
        out_shape=jax.ShapeDtypeStruct((b, t, FOX_H * FOX_DH), F32),
        compiler_params=_cparams(("parallel", "parallel", "arbitrary"), 40),
        name="fox_prompt",
    )(qi_tab, ki_tab, proj, proj, kv_t, cum_c)


def _fox_dec_kernel(pt_ref, q_ref, kn_ref, vn_ref, lfn_ref, u_ref, pm_ref, *rest, pps, ts):
    k_refs = rest[:pps]
    v_refs = rest[pps:2 * pps]
    lf_refs = rest[2 * pps:3 * pps]
    o_ref, qbd, m_sc, l_sc, acc_sc, carry_sc = rest[3 * pps:]
    j = pl.program_id(1)
    rows = FOX_H * ts
    width = FOX_H * FOX_DH
    page = k_refs[0].shape[-1]
    assert ts & (ts - 1) == 0 and FOX_DH & (FOX_DH - 1) == 0
    rowh = lax.broadcasted_iota(jnp.int32, (rows, width), 0) >> (ts.bit_length() - 1)
    colh = lax.broadcasted_iota(jnp.int32, (rows, width), 1) >> (FOX_DH.bit_length() - 1)
    own = rowh == colh

    @pl.when(j == 0)
    def _():
        q = q_ref[...] * (FOX_DH ** -0.5 * LOG2E)
        qt = jnp.broadcast_to(q[None], (FOX_H, ts, width)).reshape(rows, width)
        qbd[...] = jnp.where(own, qt, 0.0).astype(BF16)
        m_sc[...] = jnp.full_like(m_sc, NEG)
        l_sc[...] = jnp.zeros_like(l_sc)
        acc_sc[...] = jnp.zeros_like(acc_sc)
        carry_sc[...] = jnp.zeros_like(carry_sc)

    def expand(cum):
        return jnp.broadcast_to(cum[:, None, :], (FOX_H, ts, page)).reshape(rows, page)

    def update(k_list, v_list, cum_list, valid):
        qb = qbd[...]
        s = [_dot(qb, kp.astype(BF16)) - expand(cm) * LOG2E for kp, cm in zip(k_list, cum_list)]
        s = s[0] if len(s) == 1 else jnp.concatenate(s, axis=1)
        if valid is not None:
            s = jnp.where(valid, s, NEG)
        m_prev = m_sc[...]
        m_new = jnp.maximum(m_prev, jnp.max(s, axis=-1, keepdims=True))
        alpha = jnp.exp2(m_prev - m_new)
        p = jnp.exp2(s - m_new)
        l_sc[...] = alpha * l_sc[...] + jnp.sum(p, axis=-1, keepdims=True)
        m_sc[...] = m_new
        pv = None
        for i, vp in enumerate(v_list):
            t_ = _dot_nt(p[:, i * page:(i + 1) * page].astype(BF16), vp.astype(BF16))
            pv = t_ if pv is None else pv + t_
        acc_sc[...] = alpha * acc_sc[...] + pv

    lf_all = jnp.concatenate([r[...] for r in lf_refs], axis=0) if pps > 1 else lf_refs[0][...]
    cin = _dot3_r(lf_all, u_ref[...])
    carry = carry_sc[...]
    cum = cin + jnp.concatenate([carry] * pps, axis=0) if pps > 1 else cin + carry
    if pps > 1:
        cum = cum + _dot3_l(pm_ref[...], cin)[:, page - 1:page]
    carry = cum[(pps - 1) * FOX_H:pps * FOX_H, page - 1:page]
    carry_sc[...] = carry
    update([r[...].reshape(width, page) for r in k_refs], [r[...].reshape(width, page) for r in v_refs],
           [cum[i * FOX_H:(i + 1) * FOX_H] for i in range(pps)], None)

    @pl.when(j == pl.num_programs(1) - 1)
    def _():
        cumn = _dot3_r(lfn_ref[...], u_ref[...]) + carry
        kn = _pad_rows(kn_ref[...], page).T
        vn = _pad_rows(vn_ref[...], page).T
        colk = lax.broadcasted_iota(jnp.int32, (rows, page), 1)
        trow = lax.broadcasted_iota(jnp.int32, (rows, page), 0) & (ts - 1)
        update([kn], [vn], [cumn], colk <= trow)
        accn = acc_sc[...] / l_sc[...]
        o_ref[...] = jnp.sum(jnp.where(own, accn, 0.0).reshape(FOX_H, ts, width), axis=0)


def _fox_sample(layer, proj, lfn_t, page_table, cache_k, cache_v, cache_lf_t):
    db, ts, _ = proj.shape
    npages = page_table.shape[1]
    page = cache_k.shape[-1]
    width = FOX_H * FOX_DH
    pps = 16 if npages % 16 == 0 else 1
    upper = jnp.triu(jnp.ones((page, page), F32)).astype(BF16)
    ridx = jnp.arange(pps * FOX_H)
    pmat = jnp.logical_and((ridx[:, None] % FOX_H) == (ridx[None, :] % FOX_H),
                           (ridx[None, :] // FOX_H) < (ridx[:, None] // FOX_H)).astype(BF16)
    rows = FOX_H * ts

    def page_spec(shape, p_):
        return pl.BlockSpec((None, None) + shape,
                            lambda bi, j, pt, p_=p_: (layer, pt[bi, j * pps + p_]) + (0,) * len(shape))

    in_specs = [
        pl.BlockSpec((None, ts, width), lambda bi, j, pt: (bi, 0, C_FQ // width)),
        pl.BlockSpec((None, ts, width), lambda bi, j, pt: (bi, 0, C_FK // width)),
        pl.BlockSpec((None, ts, width), lambda bi, j, pt: (bi, 0, C_FV // width)),
        pl.BlockSpec((None, FOX_H, page), lambda bi, j, pt: (bi, 0, 0)),
        pl.BlockSpec((page, page), lambda bi, j, pt: (0, 0)),
        pl.BlockSpec((pps * FOX_H, pps * FOX_H), lambda bi, j, pt: (0, 0)),
    ]
    in_specs += [page_spec((FOX_H, FOX_DH, page), p_) for p_ in range(pps)]
    in_specs += [page_spec((FOX_H, FOX_DH, page), p_) for p_ in range(pps)]
    in_specs += [page_spec((FOX_H, page), p_) for p_ in range(pps)]
    grid_spec = pltpu.PrefetchScalarGridSpec(
        num_scalar_prefetch=1,
        grid=(db, npages // pps),
        in_specs=in_specs,
        out_specs=pl.BlockSpec((None, ts, width), lambda bi, j, pt: (bi, 0, 0)),
        scratch_shapes=[
            pltpu.VMEM((rows, width), BF16),
            pltpu.VMEM((rows, 1), F32),
            pltpu.VMEM((rows, 1), F32),
            pltpu.VMEM((rows, width), F32),
            pltpu.VMEM((FOX_H, 1), F32),
        ],
    )
    args = [proj, proj, proj, lfn_t, upper, pmat] + [cache_k] * pps + [cache_v] * pps + [cache_lf_t] * pps
    return pl.pallas_call(
        functools.partial(_fox_dec_kernel, pps=pps, ts=ts),
        grid_spec=grid_spec,
        out_shape=jax.ShapeDtypeStruct((db, ts, width), F32),
        compiler_params=_cparams(("parallel", "arbitrary"), 40),
        name="fox_sample",
    )(page_table, *args)


def _rw_prep_kernel(p_ref, sh0_ref, mu_ref, w2_ref, w0_ref, g2_ref, kk_ref, ka_ref, rk_ref,
                    e_ref, bl_ref,
                    kt_o, bt_o, kq_o, rt_o, v_o, gc_o, ab_o, g_o, bon_o, carry, *, tr, tv):
    i = pl.program_id(1)
    trb = p_ref.shape[0]

    @pl.when(i == 0)
    def _():
        carry[...] = sh0_ref[...]

    p = _pad_rows(p_ref[...], tr)
    rowi = lax.broadcasted_iota(jnp.int32, (tr, 1), 0)
    prev = jnp.where(rowi == 0, carry[...], pltpu.roll(p, 1, axis=0))
    carry[...] = p_ref[pl.ds(trb - 1, 1), :]
    xm = p + (prev - p) * mu_ref[...]
    r = xm[:, 0:RW_W]
    k = xm[:, RW_W:2 * RW_W]
    v = xm[:, 2 * RW_W:3 * RW_W]
    z = xm[:, 3 * RW_W:3 * RW_W + RW_LORA]
    gl = xm[:, 3 * RW_W + RW_LORA:]
    lane = lax.broadcasted_iota(jnp.int32, (1, RW_LORA), 1)
    zz = jnp.where(lane < RW_LORA // 2, jnp.tanh(z), z)
    pre = jnp.dot(zz, w2_ref[...], precision=HIGHEST, preferred_element_type=F32) + w0_ref[...]
    w_log = -_softplus(-pre[:, :RW_W]) - 0.5
    logdec = -jnp.exp(w_log)
    a = _sigmoid(pre[:, RW_W:])
    g = _dot(_sigmoid(gl).astype(BF16), g2_ref[...])
    e = e_ref[...]

    def head_sum(x):
        return jnp.concatenate(
            [_dot3_r(x[:, c0:c0 + LANE], e) for c0 in range(0, RW_W, LANE)], axis=1)

    kk = k * kk_ref[...]
    kk = kk / jnp.maximum(jnp.sqrt(head_sum(kk * kk)), 1e-12)
    k2 = k * (1.0 + (a - 1.0) * ka_ref[...])
    bonus = head_sum(r * k2 * rk_ref[...]) * v
    if tv < tr:
        valid = rowi < tv
        logdec = jnp.where(valid, logdec, 0.0)
        kk = jnp.where(valid, kk, 0.0)
        k2 = jnp.where(valid, k2, 0.0)
        v = jnp.where(valid, v, 0.0)
        r = jnp.where(valid, r, 0.0)
    cl = _dot3_l(bl_ref[...], logdec)
    ct = jnp.concatenate(
        [jnp.broadcast_to(cl[c0 + RW_CHUNK - 1:c0 + RW_CHUNK, :], (RW_CHUNK, RW_W))
         for c0 in range(0, tr, RW_CHUNK)], axis=0)
    e_inv = jnp.exp(-cl)
    kt = kk * jnp.exp(cl - logdec)
    bt = kk * a * e_inv
    kt_o[...] = kt
    bt_o[...] = bt
    kq_o[...] = k2 * e_inv
    rt_o[...] = r * jnp.exp(cl)
    v_o[...] = v
    gc_o[...] = jnp.exp(ct)
    g_o[...] = g
    bon_o[...] = bonus
    lane2 = lax.broadcasted_iota(jnp.int32, (1, LANE), 1)
    for c in range(tr // RW_CHUNK):
        rs = slice(c * RW_CHUNK, (c + 1) * RW_CHUNK)
        for hp in range(RW_H // 2):
            ls = slice(hp * LANE, (hp + 1) * LANE)
            x = kt[rs, ls]
            y = bt[rs, ls].astype(BF16)
            for h in range(2):
                mh = (lane2 < RW_N) if h == 0 else (lane2 >= RW_N)
                xh = jnp.where(mh, x, 0.0).astype(BF16)
                ab_o[2 * hp + h, rs, :] = _dot_nt(xh, y)


def _rw_prep(proj, shift0, wl, tv):
    b, t, _ = proj.shape
    tp = max(t, RW_CHUNK)
    tr = min(tp, 256)
    trb = min(t, tr)
    n = tp // tr
    cidx = jnp.arange(tr) // RW_CHUNK
    same = cidx[:, None] == cidx[None, :]
    blk_tri = jnp.logical_and(same, jnp.arange(tr)[None, :] <= jnp.arange(tr)[:, None]).astype(BF16)
    hidx = jnp.arange(LANE) // RW_N
    e128 = (hidx[:, None] == hidx[None, :]).astype(BF16)

    def const(shape):
        return pl.BlockSpec(shape, lambda bi, i: (0,) * len(shape))

    tok = pl.BlockSpec((None, tr, RW_W), lambda bi, i: (bi, i, 0))
    outs = [jax.ShapeDtypeStruct((b, tp, RW_W), F32)] * 6
    return pl.pallas_call(
        functools.partial(_rw_prep_kernel, tr=tr, tv=tv if t < tr else tr),
        grid=(b, n),
        in_specs=[
            pl.BlockSpec((None, trb, RW_PROJ), lambda bi, i: (bi, i, C_RW // RW_PROJ)),
            pl.BlockSpec((None, 1, RW_PROJ), lambda bi, i: (bi, 0, 0)),
            const((1, RW_PROJ)), const((RW_LORA, 2 * RW_W)), const((1, 2 * RW_W)),
            const((RW_GATE_R, RW_W)), const((1, RW_W)), const((1, RW_W)), const((1, RW_W)),
            const((LANE, LANE)), const((tr, tr)),
        ],
        out_specs=[tok] * 6 + [pl.BlockSpec((None, RW_H, tr, RW_N), lambda bi, i: (bi, 0, i, 0)), tok, tok],
        out_shape=outs + [jax.ShapeDtypeStruct((b, RW_H, tp, RW_N), F32)] + outs[:2],
        scratch_shapes=[pltpu.VMEM((1, RW_PROJ), F32)],
        compiler_params=_cparams(("parallel", "arbitrary"), 48),
        name="rwkv_prep",
    )(proj, shift0, wl["mu"], wl["w2a2"], wl["w0a0"], wl["g2"], wl["kk"], wl["ka"], wl["rk"],
      e128, blk_tri)


def _rw_solve_kernel(a_ref, t_ref, at_sc, tt_sc):
    c = RW_CHUNK
    per = LANE // c
    for jb in range(c * c // LANE):
        blk = a_ref[:, jb * LANE:(jb + 1) * LANE].T
        at_sc[jb * per:(jb + 1) * per] = blk.reshape(per, c, LANE)
    rows = lax.broadcasted_iota(jnp.int32, (c, LANE), 0)
    rb = 4

    def body_t(tb, carry_):
        t0 = tb * rb

        def body_u(u, accs):
            tu = tt_sc[u]
            return tuple(acc - at_sc[t0 + r, pl.ds(u, 1), :] * tu for r, acc in enumerate(accs))

        accs = lax.fori_loop(0, t0, body_u,
                             tuple(jnp.where(rows == t0 + r, 1.0, 0.0) for r in range(rb)))
        done = []
        for r in range(rb):
            acc = accs[r]
            for r2 in range(r):
                acc = acc - at_sc[t0 + r, pl.ds(t0 + r2, 1), :] * done[r2]
            tt_sc[t0 + r] = acc
            done.append(acc)
        return carry_

    lax.fori_loop(0, c // rb, body_t, 0)
    for jb in range(c * c // LANE):
        blk = tt_sc[jb * per:(jb + 1) * per].reshape(LANE, LANE)
        t_ref[:, jb * LANE:(jb + 1) * LANE] = blk.T


def _rw_solve(a2d):
    nch, cc = a2d.shape
    return pl.pallas_call(
        _rw_solve_kernel,
        grid=(nch // LANE,),
        in_specs=[pl.BlockSpec((LANE, cc), lambda i: (i, 0))],
        out_specs=pl.BlockSpec((LANE, cc), lambda i: (i, 0)),
        out_shape=jax.ShapeDtypeStruct((nch, cc), F32),
        scratch_shapes=[pltpu.VMEM((RW_CHUNK, RW_CHUNK, LANE), F32)] * 2,
        compiler_params=_cparams(("parallel",), 40),
        name="rwkv_solve",
    )(a2d)


def _rw_chain_kernel(kt_ref, bt_ref, kq_ref, rt_ref, v_ref, gc_ref, tm_ref, g_ref, bon_ref,
                     lnw_ref, lnb_ref, e_ref, h0_ref, o_ref, hout_ref,
                     h_sc, gm_sc, nm_sc, qe_sc, y_sc, *, tt, to):
    i = pl.program_id(1)
    cn = RW_CHUNK
    nck = tt // cn
    npair = RW_H // 2

    @pl.when(i == 0)
    def _():
        h_sc[...] = h0_ref[...]

    lane = lax.broadcasted_iota(jnp.int32, (1, LANE), 1)
    lo = lane < RW_N
    ri = lax.broadcasted_iota(jnp.int32, (cn, LANE), 0)
    ci = lax.broadcasted_iota(jnp.int32, (cn, LANE), 1) & (RW_N - 1)
    strict = ci < ri
    incl = ci <= ri
    rd = lax.broadcasted_iota(jnp.int32, (LANE, LANE), 0)
    cd = lax.broadcasted_iota(jnp.int32, (LANE, LANE), 1)
    eye = rd == cd
    same_head = (rd >> 6) == (cd >> 6)

    def stack(x):
        return jnp.concatenate([jnp.where(lo, x, 0.0), jnp.where(lo, 0.0, x)], axis=0).astype(BF16)

    units = [(c, p) for c in range(nck) for p in range(npair)]
    cs = range(len(units))
    rss = [slice(c * cn, (c + 1) * cn) for c, _ in units]
    lss = [slice(p * LANE, (p + 1) * LANE) for _, p in units]
    kt = [kt_ref[rs, ls] for rs, ls in zip(rss, lss)]
    bt = [bt_ref[rs, ls] for rs, ls in zip(rss, lss)]
    kq = [kq_ref[rs, ls] for rs, ls in zip(rss, lss)]
    rt = [rt_ref[rs, ls] for rs, ls in zip(rss, lss)]
    v = [v_ref[rs, ls] for rs, ls in zip(rss, lss)]
    gc = [gc_ref[rs, ls] for rs, ls in zip(rss, lss)]
    ks = [stack(kq[c]) for c in cs]
    vs = [stack(v[c]) for c in cs]
    tcat = [jnp.concatenate([tm_ref[2 * p, rs, :], tm_ref[2 * p + 1, rs, :]], axis=1).astype(BF16)
            for (_, p), rs in zip(units, rss)]
    ak = [jnp.where(strict, _dot_nt(kt[c].astype(BF16), ks[c]), 0.0).astype(BF16) for c in cs]
    kd = [_dot(tcat[c], stack(kt[c])) for c in cs]
    bb = [jnp.where(incl, _dot_nt(rt[c].astype(BF16), stack(bt[c])), 0.0).astype(BF16) for c in cs]
    bk = [jnp.where(incl, _dot_nt(rt[c].astype(BF16), ks[c]), 0.0).astype(BF16) for c in cs]
    akv = [_dot(ak[c], vs[c]) for c in cs]
    vd = [_dot(tcat[c], stack(akv[c])) for c in cs]
    bh = [(bt[c] * gc[c]).astype(BF16) for c in cs]
    for c in cs:
        gm_sc[c] = (jnp.where(eye, gc[c][0:1, :], 0.0)
                    - jnp.where(same_head, _dot_tn(bh[c], kd[c].astype(BF16)), 0.0))
        qe_sc[rss[c], lss[c]] = rt[c] - _dot(bb[c], stack(kd[c]))
    for c in cs:
        kh = (kq[c] * gc[c]).astype(BF16)
        nm_sc[c] = jnp.where(same_head, _dot_tn(
            jnp.concatenate([kh, bh[c]], axis=0),
            jnp.concatenate([v[c].astype(BF16), (-vd[c]).astype(BF16)], axis=0)), 0.0)
        y_sc[rss[c], lss[c]] = _dot(jnp.concatenate([bk[c], bb[c]], axis=1),
                                    jnp.concatenate([vs[c], stack(-vd[c])], axis=0))

    hcur = [h_sc[p] for p in range(npair)]
    for u, (c, p) in enumerate(units):
        hb = hcur[p].astype(BF16)
        y_sc[rss[u], lss[u]] = y_sc[rss[u], lss[u]] + _dot(qe_sc[rss[u], lss[u]].astype(BF16), hb)
        hcur[p] = _dot(gm_sc[u].astype(BF16), hb) + nm_sc[u]
    for p in range(npair):
        h_sc[p] = hcur[p]

    e = e_ref[...]
    for p in range(npair):
        ls = slice(p * LANE, (p + 1) * LANE)
        y = y_sc[:, ls]
        mean = _dot3_r(y, e) * (1.0 / RW_N)
        yc = y - mean
        var = _dot3_r(yc * yc, e) * (1.0 / RW_N)
        out = ((yc * lax.rsqrt(var + RW_GN_EPS) * lnw_ref[:, ls] + lnb_ref[:, ls] + bon_ref[:, ls])
               * g_ref[:, ls])
        o_ref[:, ls] = out[:to]

    @pl.when(i == pl.num_programs(1) - 1)
    def _():
        for p in range(npair):
            hout_ref[p] = hcur[p]


def _rw_chain(prep, tmat, h0, lnw, lnb, t_out):
    kt, bt, kq, rt, v, gc, _, g, bon = prep
    b, tp, _ = kt.shape
    tt = min(tp, 128)
    to = min(t_out, tt)
    hp = RW_H // 2
    hidx = jnp.arange(LANE) // RW_N
    e128 = (hidx[:, None] == hidx[None, :]).astype(BF16)
    tok = pl.BlockSpec((None, tt, RW_W), lambda bi, i: (bi, i, 0))
    vec = pl.BlockSpec((1, RW_W), lambda bi, i: (0, 0))
    st = pl.BlockSpec((None, hp, LANE, LANE), lambda bi, i: (bi, 0, 0, 0))
    nunit = tt // RW_CHUNK * hp
    return pl.pallas_call(
        functools.partial(_rw_chain_kernel, tt=tt, to=to),
        grid=(b, tp // tt),
        in_specs=[tok] * 6 + [
            pl.BlockSpec((None, RW_H, tt, RW_N), lambda bi, i: (bi, 0, i, 0)),
            tok, tok, vec, vec,
            pl.BlockSpec((LANE, LANE), lambda bi, i: (0, 0)),
            st,
        ],
        out_specs=[pl.BlockSpec((None, to, RW_W), lambda bi, i: (bi, i, 0)), st],
        out_shape=[
            jax.ShapeDtypeStruct((b, t_out, RW_W), F32),
            jax.ShapeDtypeStruct((b, hp, LANE, LANE), F32),
        ],
        scratch_shapes=[
            pltpu.VMEM((hp, LANE, LANE), F32),
            pltpu.VMEM((nunit, LANE, LANE), F32),
            pltpu.VMEM((nunit, LANE, LANE), F32),
            pltpu.VMEM((tt, RW_W), F32),
            pltpu.VMEM((tt, RW_W), F32),
        ],
        compiler_params=_cparams(("parallel", "arbitrary"), 40),
        name="rwkv_chain",
    )(kt, bt, kq, rt, v, gc, tmat, g, bon, lnw, lnb, e128, h0)


def _rwkv(proj, shift0, s0, wl, t):
    b = proj.shape[0]
    prep = _rw_prep(proj, shift0, wl, t)
    ab = prep[6]
    tp = ab.shape[2]
    cc = RW_CHUNK * RW_CHUNK
    a2d = ab.reshape(-1, cc)
    nch = a2d.shape[0]
    nch_p = -(-nch // LANE) * LANE
    if nch_p != nch:
        a2d = jnp.pad(a2d, ((0, nch_p - nch), (0, 0)))
    tmat = _rw_solve(a2d)[:nch].reshape(b, RW_H, tp, RW_N)
    st = jnp.swapaxes(s0, -1, -2).reshape(b, RW_H // 2, 2, RW_N, RW_N)
    z = jnp.zeros_like(st[:, :, 0])
    h0 = jnp.concatenate([jnp.concatenate([st[:, :, 0], z], -1), jnp.concatenate([z, st[:, :, 1]], -1)], -2)
    o, hout = _rw_chain(prep, tmat, h0, wl["lnw"], wl["lnb"], t)
    s_new = jnp.stack([hout[:, :, :RW_N, :RW_N], hout[:, :, RW_N:, RW_N:]], axis=2)
    s_new = jnp.swapaxes(s_new.reshape(b, RW_H, RW_N, RW_N), -1, -2)
    return o, s_new


def _merge_kernel(x_ref, gm_ref, oa_ref, ob_ref, oc_ref, ga_ref, gb_ref, gc_ref,
                  wa_ref, wb_ref, wc_ref, wo_ref, o_ref):
    merged = (_sigmoid(ga_ref[...]) * _dot(oa_ref[...].astype(BF16), wa_ref[...])
              + _sigmoid(gb_ref[...]) * _dot(ob_ref[...].astype(BF16), wb_ref[...])
              + _sigmoid(gc_ref[...]) * _dot(oc_ref[...].astype(BF16), wc_ref[...]))
    mix = _dot(merged.astype(BF16), wo_ref[...])
    o_ref[...] = x_ref[...] + gm_ref[...] * mix


def _merge(x, g1, o_ret, o_fox, o_rw, proj, wl):
    g_, r, d = x.shape
    tm = min(r, 512)
    rm = g1.shape[1]
    w = o_ret.shape[-1]

    def tok(width, cb=0):
        return pl.BlockSpec((None, tm, width), lambda g, i, cb=cb: (g, i, cb))

    def const(shape):
        return pl.BlockSpec(shape, lambda g, i: (0, 0))

    return pl.pallas_call(
        _merge_kernel,
        grid=(g_, r // tm),
        in_specs=[
            tok(d), _row_spec(rm, tm, d), tok(w), tok(w), tok(w),
            tok(d, C_GATE // d), tok(d, C_GATE // d + 1), tok(d, C_GATE // d + 2),
            const((w, d)), const((w, d)), const((w, d)), const((d, d)),
        ],
        out_specs=tok(d),
        out_shape=jax.ShapeDtypeStruct((g_, r, d), F32),
        compiler_params=_cparams(("parallel", "parallel"), 48),
        name="merge_out",
    )(x, g1, o_ret, o_fox, o_rw, proj, proj, proj, wl["w_br_ret"], wl["w_br_fox"], wl["w_br_rw"], wl["w_out"])


def _ffn_kernel(x_ref, sc_ref, sh_ref, gm_ref, g_ref, wu_ref, wd_ref, nf_ref, o_ref, h_sc, acc_sc, *, final):
    f = pl.program_id(2)

    @pl.when(f == 0)
    def _():
        h_sc[...] = _norm_mod(x_ref[...], sc_ref[...], sh_ref[...], g_ref[...]).astype(BF16)
        acc_sc[...] = jnp.zeros_like(acc_sc)

    u = jnp.maximum(_dot(h_sc[...], wu_ref[...]), 0.0)
    acc_sc[...] += _dot((u * u).astype(BF16), wd_ref[...])

    @pl.when(f == pl.num_programs(2) - 1)
    def _():
        xn = x_ref[...] + gm_ref[...] * acc_sc[...]
        if final:
            ms = jnp.mean(xn * xn, axis=-1, keepdims=True)
            xn = xn * lax.rsqrt(ms + NORM_EPS) * nf_ref[...]
        o_ref[...] = xn


def _ffn(x, sc, sh, g2, gamma, w_up, w_down, norm_final, final):
    g_, r, d = x.shape
    dff = w_up.shape[1]
    tm = min(r, 1024)
    tf = 1024
    rm = sc.shape[1]
    return pl.pallas_call(
        functools.partial(_ffn_kernel, final=final),
        grid=(g_, r // tm, dff // tf),
        in_specs=[
            pl.BlockSpec((None, tm, d), lambda g, i, f: (g, i, 0)),
            _row_spec(rm, tm, d), _row_spec(rm, tm, d), _row_spec(rm, tm, d),
            pl.BlockSpec((1, d), lambda g, i, f: (0, 0)),
            pl.BlockSpec((d, tf), lambda g, i, f: (0, f)),
            pl.BlockSpec((tf, d), lambda g, i, f: (f, 0)),
            pl.BlockSpec((1, d), lambda g, i, f: (0, 0)),
        ],
        out_specs=pl.BlockSpec((None, tm, d), lambda g, i, f: (g, i, 0)),
        out_shape=jax.ShapeDtypeStruct((g_, r, d), F32),
        scratch_shapes=[pltpu.VMEM((tm, d), BF16), pltpu.VMEM((tm, d), F32)],
        compiler_params=_cparams(("parallel", "parallel", "arbitrary"), 48),
        name="ffn",
    )(x, sc, sh, g2, gamma, w_up, w_down, norm_final)


def _rope_tables(q0, t):
    d = RET_D
    inv = 1.0 / (ROPE_BASE ** (jnp.arange(0, d, 2, dtype=F32) / d))
    ang = (q0 + jnp.arange(t)).astype(F32)[:, None] * inv[None, :]
    cos, sin = jnp.cos(ang), jnp.sin(ang)
    return jnp.concatenate([cos, cos], -1), jnp.concatenate([-sin, sin], -1)


def _prep_layer_weights(l, w):
    w_in = w["w_in"][l]
    d = w_in.shape[0]
    ret_cols, fox_qkv = 2048, 1536
    f0 = ret_cols + fox_qkv
    rw0 = f0 + FOX_H
    g0 = rw0 + RW_PROJ
    w_t = jnp.swapaxes(w_in, 0, 1)
    packed = jnp.concatenate([
        w_t[rw0:g0], w_t[f0:rw0], jnp.zeros((2 * LANE - FOX_H, d), F32),
        w_t[g0:], w_t[ret_cols:f0], w_t[:ret_cols]], axis=0).astype(BF16)
    zl = jnp.zeros((RW_LORA // 2, RW_W), F32)
    w2a2 = jnp.concatenate([jnp.concatenate([w["rw_w2"][l], zl], 1),
                            jnp.concatenate([zl, w["rw_a2"][l]], 1)], 0)
    row = lambda a: a.reshape(1, -1)
    return {
        "w_in": packed,
        "norm_mix": row(w["norm_mix"][l]), "norm_ffn": row(w["norm_ffn"][l]),
        "bf": jnp.pad(w["fox_bf"][l], (0, LANE - FOX_H)).reshape(1, LANE),
        "mu": row(w["rw_mu"][l]), "w2a2": w2a2,
        "w0a0": jnp.concatenate([w["rw_w0"][l], w["rw_a0"][l]]).reshape(1, -1),
        "g2": w["rw_g2"][l].astype(BF16),
        "kk": row(w["rw_kk"][l]), "ka": row(w["rw_ka"][l]), "rk": row(w["rw_rk"][l]),
        "lnw": row(w["rw_lnw"][l]), "lnb": row(w["rw_lnb"][l]),
        "w_br_ret": w["w_br_ret"][l].astype(BF16), "w_br_fox": w["w_br_fox"][l].astype(BF16),
        "w_br_rw": w["w_br_rw"][l].astype(BF16), "w_out": w["w_out"][l].astype(BF16),
        "w_up": w["w_up"][l].astype(BF16), "w_down": w["w_down"][l].astype(BF16),
    }


def _ret_log_gamma_table():
    lg = jnp.log1p(-jnp.exp2(-5.0 - jnp.arange(RET_H, dtype=F32)))
    return jnp.broadcast_to(lg[:, None, None], (RET_H, 1, RET_D))


def _layer(x, mod, q0, past, ret0, wkv0, shift0, wl, norm_final, final, flat):
    b, t, d = x.shape
    sh1, sc1, g1, sh2, sc2, g2 = [m[:, None, :] for m in jnp.split(mod, 6, axis=-1)]
    if flat:
        rep = lambda m: jnp.repeat(m, t, axis=1).reshape(1, b * t, d)
        sh1, sc1, g1, sh2, sc2, g2 = [rep(m) for m in (sh1, sc1, g1, sh2, sc2, g2)]
        xd = x.reshape(1, b * t, d)
    else:
        xd = x
    if past is None:
        proj_d, kv_t = _norm_mm(xd, sc1, sh1, wl["norm_mix"], wl["w_in"], (C_FK, C_FV))
    else:
        proj_d, = _norm_mm(xd, sc1, sh1, wl["norm_mix"], wl["w_in"])
    proj = proj_d.reshape(b, t, PROJ_W)

    cosf, sinf = _rope_tables(q0, t)
    o_ret, r_new = _retention(proj, cosf, sinf, _ret_log_gamma_table(), ret0)

    lf, cum = _fox_gate(proj_d, wl["bf"])
    lf = lf.reshape(b, t, LANE)[:, :, :FOX_H]
    if past is None:
        o_fox = _fox_prompt(proj, kv_t, cum)
    else:
        layer, page_table, cache_k, cache_v, cache_lf_t = past
        page = cache_k.shape[-1]
        lfn_t =jnp.pad(jnp.swapaxes(lf, 1, 2), ((0, 0), (0, 0), (0, page - t)))
        o_fox = _fox_sample(layer, proj, lfn_t, page_table, cache_k, cache_v, cache_lf_t)

    o_rw, s_new = _rwkv(proj, shift0[:, None, :], wkv0, wl, t)

    dn = lambda a: a.reshape(xd.shape[0], xd.shape[1], -1)
    x1 = _merge(xd, g1, dn(o_ret), dn(o_fox), dn(o_rw), proj_d, wl)
    x2 = _ffn(x1, sc2, sh2, g2, wl["norm_ffn"], wl["w_up"], wl["w_down"], norm_final, final)
    if past is None:
        fk = kv_t[:, 0].reshape(b, FOX_H, FOX_DH, t).transpose(0, 3, 1, 2)
        fv = kv_t[:, 1].reshape(b, FOX_H, FOX_DH, t).transpose(0, 3, 1, 2)
    else:
        fk = proj[:, :, C_FK:C_FV].reshape(b, t, FOX_H, FOX_DH)
        fv = proj[:, :, C_FV:C_RQ].reshape(b, t, FOX_H, FOX_DH)
    shift = proj[:, t - 1, C_RW:C_RW + RW_PROJ]
    return x2.reshape(b, t, d), (fk, fv, lf, r_new, s_new, shift)


def kernel(x_prompt, x_sample, c_prompt, c_sample, cache_k, cache_v, cache_logf, page_table, state_ret, state_wkv, state_shift, ada_w, ada_b, norm_mix, norm_ffn, w_in, fox_bf, rw_mu, rw_w0, rw_w2, rw_a0, rw_a2, rw_g2, rw_kk, rw_ka, rw_rk, rw_lnw, rw_lnb, w_br_ret, w_br_fox, w_br_rw, w_out, w_up, w_down, norm_final):
    depth = w_in.shape[0]
    b, t, d = x_prompt.shape
    db, ts, _ = x_sample.shape
    n_pool, page = cache_k.shape[1], cache_k.shape[2]
    past_len = page_table.shape[1] * page
    w = dict(w_in=w_in, norm_mix=norm_mix, norm_ffn=norm_ffn, fox_bf=fox_bf, rw_mu=rw_mu, rw_w0=rw_w0,
             rw_w2=rw_w2, rw_a0=rw_a0, rw_a2=rw_a2, rw_g2=rw_g2, rw_kk=rw_kk, rw_ka=rw_ka, rw_rk=rw_rk,
             rw_lnw=rw_lnw, rw_lnb=rw_lnb, w_br_ret=w_br_ret, w_br_fox=w_br_fox, w_br_rw=w_br_rw,
             w_out=w_out, w_up=w_up, w_down=w_down)
    rows = b + db
    rows_p = -(-rows // 8) * 8
    c_all = jnp.pad(jnp.concatenate([c_prompt, c_sample], 0), ((0, rows_p - rows), (0, 0)))
    mod = _modulation(c_all, ada_w, ada_b)
    ck = jnp.transpose(cache_k, (0, 1, 3, 4, 2))
    cv = jnp.transpose(cache_v, (0, 1, 3, 4, 2))
    clf_t = jnp.swapaxes(cache_logf, 2, 3)
    nf = norm_final.reshape(1, d)
    xp, xs = x_prompt, x_sample
    st_p, st_s = [], []
    for l in range(depth):
        wl = _prep_layer_weights(l, w)
        final = l == depth - 1
        xp, sp = _layer(xp, mod[l, :b], 0, None, jnp.zeros((b, RET_H, RET_D, RET_D), F32),
                        jnp.zeros((b, RW_H, RW_N, RW_N), F32), jnp.zeros((b, RW_PROJ), F32),
                        wl, nf, final, False)
        st_p.append(sp)
        xs, ss = _layer(xs, mod[l, b:rows], past_len, (l, page_table, ck, cv, clf_t), state_ret[l],
                        state_wkv[l], state_shift[l], wl, nf, final, True)
        st_s.append(ss)
    outs = [xp, xs]
    for st in (st_p, st_s):
        for i in range(6):
            outs.append(jnp.stack([s[i] for s in st]))
    y_p, y_s, k_p, v_p, lf_p, ret_p, wkv_p, sh_p, k_s, v_s, lf_s, ret_s, wkv_s, sh_s = outs
    return (y_p, y_s, k_p, v_p, lf_p, ret_p, wkv_p, sh_p, k_s, v_s, lf_s, ret_s, wkv_s, sh_s)
```

```python
import functools

import jax
import jax.numpy as jnp
from jax import lax
from jax.experimental import pallas as pl
from jax.experimental.pallas import tpu as pltpu

F32 = jnp.float32
BF16 = jnp.bfloat16
HIGHEST = lax.Precision.HIGHEST

RET_H, RET_D = 4, 128
FOX_H, FOX_DH = 8, 64
RW_H, RW_N = 8, 64
RW_W = RW_H * RW_N
RW_LORA = 128
RW_GATE_R = 128
RW_PROJ = 3 * RW_W + RW_LORA + RW_GATE_R
RET_GN_EPS = 1e-6
RW_GN_EPS = 64e-5
NORM_EPS = 1e-6
ROPE_BASE = 10000.0
NEG = -1e30
LOG2E = 1.4426950408889634

LANE = 128
C_RW = 0
C_F = 1792
C_GATE = 2048
C_FQ, C_FK, C_FV = 5120, 5632, 6144
C_RQ, C_RK, C_RV, C_RG = 6656, 7168, 7680, 8192
PROJ_W = 8704

RW_CHUNK = 64
RET_CHUNK = 128


def _cparams(sem, vmem_mb):
    return pltpu.CompilerParams(dimension_semantics=sem, vmem_limit_bytes=vmem_mb << 20)


def _dot(a, b):
    return jnp.dot(a, b, preferred_element_type=F32)


def _dot_nt(a, b):
    return lax.dot_general(a, b, (((1,), (1,)), ((), ())), preferred_element_type=F32)


def _dot_tn(a, b):
    return lax.dot_general(a, b, (((0,), (0,)), ((), ())), preferred_element_type=F32)


def _split3(x):
    hi = x.astype(BF16)
    r1 = x - hi.astype(F32)
    mid = r1.astype(BF16)
    lo = (r1 - mid.astype(F32)).astype(BF16)
    return hi, mid, lo


def _dot3_l(m01, x):
    hi, mid, lo = _split3(x)
    return _dot(m01, hi) + _dot(m01, mid) + _dot(m01, lo)


def _dot3_r(x, m01):
    hi, mid, lo = _split3(x)
    return _dot(hi, m01) + _dot(mid, m01) + _dot(lo, m01)


def _sigmoid(x):
    return 1.0 / (1.0 + jnp.exp(-x))


def _softplus(x):
    return jnp.maximum(x, 0.0) + jnp.log(1.0 + jnp.exp(-jnp.abs(x)))


def _pad_rows(a, rows):
    if a.shape[0] == rows:
        return a
    return jnp.concatenate([a, jnp.zeros((rows - a.shape[0],) + a.shape[1:], a.dtype)], axis=0)


def _mod_kernel(c_ref, w_ref, b_ref, o_ref):
    c = c_ref[...]
    s = c * _sigmoid(c)
    o_ref[...] = jnp.dot(s, w_ref[...], precision=HIGHEST, preferred_element_type=F32) + b_ref[...]


def _modulation(c_all, ada_w, ada_b):
    depth, d, n = ada_w.shape
    rows = c_all.shape[0]
    tn = 1536
    return pl.pallas_call(
        _mod_kernel,
        grid=(depth, n // tn),
        in_specs=[
            pl.BlockSpec((rows, d), lambda l, j: (0, 0)),
            pl.BlockSpec((None, d, tn), lambda l, j: (l, 0, j)),
            pl.BlockSpec((None, 1, tn), lambda l, j: (l, 0, j)),
        ],
        out_specs=pl.BlockSpec((None, rows, tn), lambda l, j: (l, 0, j)),
        out_shape=jax.ShapeDtypeStruct((depth, rows, n), F32),
        compiler_params=_cparams(("parallel", "parallel"), 40),
        name="ada_mod",
    )(c_all, ada_w, ada_b.reshape(depth, 1, n))


def _norm_mod(x, sc, sh, g):
    ms = jnp.mean(x * x, axis=-1, keepdims=True)
    y = x * lax.rsqrt(ms + NORM_EPS) * g
    return y * (1.0 + sc) + sh


def _norm_mm_kernel(x_ref, sc_ref, sh_ref, g_ref, w_ref, o_ref, *rest, t_blocks):
    h_ref = rest[-1]
    j = pl.program_id(2)

    @pl.when(j == 0)
    def _():
        h_ref[...] = _norm_mod(x_ref[...], sc_ref[...], sh_ref[...], g_ref[...]).astype(BF16)

    y = _dot_nt(h_ref[...], w_ref[...])
    o_ref[...] = y
    for n_, jb in enumerate(t_blocks):
        @pl.when(j == jb)
        def _(n_=n_):
            rest[n_][...] = y.T


def _row_spec(rm, tm, d):
    if rm == 1:
        return pl.BlockSpec((None, 1, d), lambda g, i, *_: (g, 0, 0))
    return pl.BlockSpec((None, tm, d), lambda g, i, *_: (g, i, 0))


def _norm_mm(x, sc, sh, gamma, w_t, t_cols=()):
    g_, r, d = x.shape
    n = w_t.shape[0]
    tm = min(r, 2048)
    tn = 512
    rm = sc.shape[1]
    out_specs = [pl.BlockSpec((None, tm, tn), lambda g, i, j: (g, i, j))]
    out_shape = [jax.ShapeDtypeStruct((g_, r, n), F32)]
    t_blocks = tuple(c0 // tn for c0 in t_cols)
    for _ in t_blocks:
        out_specs.append(pl.BlockSpec((None, tn, tm), lambda g, i, j: (g, 0, i)))
        out_shape.append(jax.ShapeDtypeStruct((g_, tn, r), F32))
    return pl.pallas_call(
        functools.partial(_norm_mm_kernel, t_blocks=t_blocks),
        grid=(g_, r // tm, n // tn),
        in_specs=[
            pl.BlockSpec((None, tm, d), lambda g, i, j: (g, i, 0)),
            _row_spec(rm, tm, d),
            _row_spec(rm, tm, d),
            pl.BlockSpec((1, d), lambda g, i, j: (0, 0)),
            pl.BlockSpec((tn, d), lambda g, i, j: (j, 0)),
        ],
        out_specs=out_specs,
        out_shape=out_shape,
        scratch_shapes=[pltpu.VMEM((tm, d), BF16)],
        compiler_params=_cparams(("parallel", "parallel", "arbitrary"), 56),
        name="norm_in_proj",
    )(x, sc, sh, gamma, w_t)


def _fox_gate_kernel(f_ref, b_ref, tri_ref, lf_ref, cum_ref, carry_ref):
    @pl.when(pl.program_id(1) == 0)
    def _():
        carry_ref[...] = jnp.zeros_like(carry_ref)

    z = f_ref[...] + b_ref[...]
    lf = -_softplus(-z)
    lf_ref[...] = lf
    cum = _dot3_l(tri_ref[...], lf) + carry_ref[...]
    tb = cum.shape[0]
    carry_ref[...] = cum[tb - 1:tb, :]
    cum2 = cum * LOG2E
    for h in range(FOX_H):
        cum_ref[h] = cum2[:, h:h + 1]


def _fox_gate(proj, bf_pad):
    g_, r, _ = proj.shape
    tb = min(r, 512)
    tri = jnp.tril(jnp.ones((tb, tb), F32)).astype(BF16)
    return pl.pallas_call(
        _fox_gate_kernel,
        grid=(g_, r // tb),
        in_specs=[
            pl.BlockSpec((None, tb, LANE), lambda g, i: (g, i, C_F // LANE)),
            pl.BlockSpec((1, LANE), lambda g, i: (0, 0)),
            pl.BlockSpec((tb, tb), lambda g, i: (0, 0)),
        ],
        out_specs=[
            pl.BlockSpec((None, tb, LANE), lambda g, i: (g, i, 0)),
            pl.BlockSpec((None, FOX_H, tb, 1), lambda g, i: (g, 0, i, 0)),
        ],
        out_shape=[jax.ShapeDtypeStruct((g_, r, LANE), F32),
                   jax.ShapeDtypeStruct((g_, FOX_H, r, 1), F32)],
        scratch_shapes=[pltpu.VMEM((1, LANE), F32)],
        compiler_params=_cparams(("parallel", "arbitrary"), 32),
        name="fox_gate",
    )(proj, bf_pad, tri)


def _ret_kernel(q_ref, k_ref, v_ref, g_ref, cos_ref, sin_ref, lg_ref, r0_ref,
                o_ref, rout_ref, r_sc, *, ce, nb):
    cp = RET_CHUNK
    c = pl.program_id(1)
    heads = range(RET_H)

    @pl.when(c == 0)
    def _():
        r_sc[...] = r0_ref[...]

    rows = max(nb * ce, cp)
    cosf = _pad_rows(cos_ref[...], rows)
    sinf = _pad_rows(sin_ref[...], rows)

    def rot(x):
        return x * cosf + pltpu.roll(x, RET_D // 2, axis=1) * sinf

    row = lax.broadcasted_iota(jnp.int32, (cp, RET_D), 0).astype(F32)
    ri = lax.broadcasted_iota(jnp.int32, (cp, cp), 0)
    ci = lax.broadcasted_iota(jnp.int32, (cp, cp), 1)
    diff = (ri - ci).astype(F32)
    lss = [slice(h * RET_D, (h + 1) * RET_D) for h in heads]
    q = [rot(_pad_rows(q_ref[:, ls], rows)).astype(BF16) for ls in lss]
    kf = [rot(_pad_rows(k_ref[:, ls], rows)) * (RET_D ** -0.5) for ls in lss]
    vb = [_pad_rows(v_ref[:, ls], rows).astype(BF16) for ls in lss]
    lgs = [lg_ref[h] for h in heads]
    qdec = [jnp.exp(lg * (row + 1.0)) for lg in lgs]
    kdec = [jnp.exp(lg * (ce - 1.0 - row)) for lg in lgs]
    dmask = [jnp.where(diff >= 0.0, jnp.exp(lg[:, :1] * jnp.maximum(diff, 0.0)), 0.0) for lg in lgs]
    gdec = [jnp.exp(lg[:, :1] * float(ce)) for lg in lgs]

    r_cur = [r_sc[h] for h in heads]
    outs = [[] for _ in heads]
    for j in range(nb):
        rs = slice(j * cp, (j + 1) * cp)
        s = [_dot_nt(q[h][rs], kf[h][rs].astype(BF16)) * dmask[h] for h in heads]
        cross = [_dot(q[h][rs], r_cur[h].astype(BF16)) * qdec[h] for h in heads]
        upd = [_dot_tn((kf[h][rs] * kdec[h]).astype(BF16), vb[h][rs]) for h in heads]
        inner = [_dot(s[h].astype(BF16), vb[h][rs]) for h in heads]
        for h in heads:
            r_cur[h] = gdec[h] * r_cur[h] + upd[h]
            outs[h].append(inner[h] + cross[h])
    for h in heads:
        r_sc[h] = r_cur[h]
        o = outs[h][0] if nb == 1 else jnp.concatenate(outs[h], axis=0)
        mu = jnp.mean(o, axis=-1, keepdims=True)
        oc = o - mu
        var = jnp.mean(oc * oc, axis=-1, keepdims=True)
        gg = _pad_rows(g_ref[:, lss[h]], rows)
        out = oc * lax.rsqrt(var + RET_GN_EPS) * (gg * _sigmoid(gg))
        o_ref[:, lss[h]] = out[:nb * ce]

    @pl.when(c == pl.num_programs(1) - 1)
    def _():
        for h in heads:
            rout_ref[h] = r_cur[h]


def _retention(proj, cosf, sinf, lg_tab, r0):
    b, t, _ = proj.shape
    ce = min(RET_CHUNK, t)
    n = t // ce
    nb = 4 if (ce == RET_CHUNK and n % 4 == 0) else 1
    tr = nb * ce

    w = RET_H * RET_D

    def col(c0):
        return pl.BlockSpec((None, tr, w), lambda bi, c, c0=c0: (bi, c, c0 // w))

    state = pl.BlockSpec((None, RET_H, RET_D, RET_D), lambda bi, c: (bi, 0, 0, 0))
    return pl.pallas_call(
        functools.partial(_ret_kernel, ce=ce, nb=nb),
        grid=(b, n // nb),
        in_specs=[
            col(C_RQ), col(C_RK), col(C_RV), col(C_RG),
            pl.BlockSpec((tr, RET_D), lambda bi, c: (c, 0)),
            pl.BlockSpec((tr, RET_D), lambda bi, c: (c, 0)),
            pl.BlockSpec((RET_H, 1, RET_D), lambda bi, c: (0, 0, 0)),
            state,
        ],
        out_specs=[pl.BlockSpec((None, tr, w), lambda bi, c: (bi, c, 0)), state],
        out_shape=[
            jax.ShapeDtypeStruct((b, t, w), F32),
            jax.ShapeDtypeStruct((b, RET_H, RET_D, RET_D), F32),
        ],
        scratch_shapes=[pltpu.VMEM((RET_H, RET_D, RET_D), F32)],
        compiler_params=_cparams(("parallel", "arbitrary"), 40),
        name="retention",
    )(proj, proj, proj, proj, cosf, sinf, lg_tab, r0)


def _fox_kernel(qi_ref, ki_ref, q_ref, k_ref, v_ref, ck_ref, o_ref, qt_sc, m_sc, l_sc, acc_sc, *, tq, tk):
    qi = qi_ref[pl.program_id(2)]
    ki = ki_ref[pl.program_id(2)]
    head0 = lax.broadcasted_iota(jnp.int32, (LANE, 1), 0) < FOX_DH

    @pl.when(ki == 0)
    def _():
        q = q_ref[...] * (FOX_DH ** -0.5 * LOG2E)
        qt_sc[...] = q.T.astype(BF16)
        m_sc[...] = jnp.full_like(m_sc, NEG)
        l_sc[...] = jnp.zeros_like(l_sc)
        acc_sc[...] = jnp.zeros_like(acc_sc)

    def step(diagonal):
        kb = k_ref[...].astype(BF16)
        vt = v_ref[...].astype(BF16)
        qt = qt_sc[...]
        if diagonal:
            key = lax.broadcasted_iota(jnp.int32, (tk, tq), 0)
            qry = lax.broadcasted_iota(jnp.int32, (tk, tq), 1)
            causal = key <= qry
        for h in range(2):
            hs = slice(h * FOX_DH, (h + 1) * FOX_DH)
            qth = jnp.where(head0 if h == 0 else jnp.logical_not(head0), qt, jnp.zeros_like(qt))
            s = _dot(kb, qth) - ck_ref[h]
            if diagonal:
                s = jnp.where(causal, s, NEG)
            m_prev = m_sc[h:h + 1, :]
            m_new = jnp.maximum(m_prev, jnp.max(s, axis=0, keepdims=True))
            alpha = jnp.exp2(m_prev - m_new)
            p = jnp.exp2(s - m_new)
            l_sc[h:h + 1, :] = alpha * l_sc[h:h + 1, :] + jnp.sum(p, axis=0, keepdims=True)
            m_sc[h:h + 1, :] = m_new
            acc_sc[hs, :] = alpha * acc_sc[hs, :] + _dot(vt[hs, :], p.astype(BF16))

    @pl.when(ki < qi)
    def _():
        step(False)

    @pl.when(ki == qi)
    def _():
        step(True)
        linv = jnp.where(head0, 1.0 / l_sc[0:1, :], 1.0 / l_sc[1:2, :])
        o_ref[...] = (acc_sc[...] * linv).T


def _fox_prompt(proj, v_t, cum_c):
    b, t, _ = proj.shape
    tq = tk = min(t, 512)
    nq = t // tq
    hp = FOX_H // 2
    pairs = [(qi, ki) for qi in range(nq) for ki in range(qi + 1)]
    qi_tab = jnp.array([p_[0] for p_ in pairs], jnp.int32)
    ki_tab = jnp.array([p_[1] for p_ in pairs], jnp.int32)
    grid_spec = pltpu.PrefetchScalarGridSpec(
        num_scalar_prefetch=2,
        grid=(b, hp, len(pairs)),
        in_specs=[
            pl.BlockSpec((None, tq, LANE), lambda bi, p, s, qt, kt: (bi, qt[s], C_FQ // LANE + p)),
            pl.BlockSpec((None, tk, LANE), lambda bi, p, s, qt, kt: (bi, kt[s], C_FK // LANE + p)),
            pl.BlockSpec((None, LANE, tk), lambda bi, p, s, qt, kt: (bi, p, kt[s])),
            pl.BlockSpec((None, 2, tk, 1), lambda bi, p, s, qt, kt: (bi, p, kt[s], 0)),
        ],
        out_specs=pl.BlockSpec((None, tq, LANE), lambda bi, p, s, qt, kt: (bi, qt[s], p)),
        scratch_shapes=[
            pltpu.VMEM((LANE, tq), BF16),
            pltpu.VMEM((2, tq), F32),
            pltpu.VMEM((2, tq), F32),
            pltpu.VMEM((LANE, tq), F32),
        ],
    )
    return pl.pallas_call(
        functools.partial(_fox_kernel, tq=tq, tk=tk),
        grid_spec=grid_spec,
        out_shape=jax.ShapeDtypeStruct((b, t, FOX_H * FOX_DH), F32),
        compiler_params=_cparams(("parallel", "parallel", "arbitrary"), 40),
        name="fox_prompt",
    )(qi_tab, ki_tab, proj, proj, v_t, cum_c)


def _fox_dec_kernel(pt_ref, q_ref, kn_ref, vn_ref, lfn_ref, u_ref, pm_ref, *rest, pps, ts):
    k_refs = rest[:pps]
    v_refs = rest[pps:2 * pps]
    lf_refs = rest[2 * pps:3 * pps]
    o_ref, qbd, m_sc, l_sc, acc_sc, carry_sc = rest[3 * pps:]
    j = pl.program_id(1)
    rows = FOX_H * ts
    width = FOX_H * FOX_DH
    page = k_refs[0].shape[-1]
    assert ts & (ts - 1) == 0 and FOX_DH & (FOX_DH - 1) == 0
    rowh = lax.broadcasted_iota(jnp.int32, (rows, width), 0) >> (ts.bit_length() - 1)
    colh = lax.broadcasted_iota(jnp.int32, (rows, width), 1) >> (FOX_DH.bit_length() - 1)
    own = rowh == colh

    @pl.when(j == 0)
    def _():
        q = q_ref[...] * (FOX_DH ** -0.5 * LOG2E)
        qt = jnp.broadcast_to(q[None], (FOX_H, ts, width)).reshape(rows, width)
        qbd[...] = jnp.where(own, qt, 0.0).astype(BF16)
        m_sc[...] = jnp.full_like(m_sc, NEG)
        l_sc[...] = jnp.zeros_like(l_sc)
        acc_sc[...] = jnp.zeros_like(acc_sc)
        carry_sc[...] = jnp.zeros_like(carry_sc)

    def expand(cum):
        return jnp.broadcast_to(cum[:, None, :], (FOX_H, ts, page)).reshape(rows, page)

    def update(k_list, v_list, cum_list, valid):
        qb = qbd[...]
        s = [_dot(qb, kp.astype(BF16)) - expand(cm) * LOG2E for kp, cm in zip(k_list, cum_list)]
        s = s[0] if len(s) == 1 else jnp.concatenate(s, axis=1)
        if valid is not None:
            s = jnp.where(valid, s, NEG)
        m_prev = m_sc[...]
        m_new = jnp.maximum(m_prev, jnp.max(s, axis=-1, keepdims=True))
        alpha = jnp.exp2(m_prev - m_new)
        p = jnp.exp2(s - m_new)
        l_sc[...] = alpha * l_sc[...] + jnp.sum(p, axis=-1, keepdims=True)
        m_sc[...] = m_new
        pv = None
        for i, vp in enumerate(v_list):
            t_ = _dot_nt(p[:, i * page:(i + 1) * page].astype(BF16), vp.astype(BF16))
            pv = t_ if pv is None else pv + t_
        acc_sc[...] = alpha * acc_sc[...] + pv

    lf_all = jnp.concatenate([r[...] for r in lf_refs], axis=0) if pps > 1 else lf_refs[0][...]
    cin = _dot3_r(lf_all, u_ref[...])
    carry = carry_sc[...]
    cum = cin + jnp.concatenate([carry] * pps, axis=0) if pps > 1 else cin + carry
    if pps > 1:
        cum = cum + _dot3_l(pm_ref[...], cin)[:, page - 1:page]
    carry = cum[(pps - 1) * FOX_H:pps * FOX_H, page - 1:page]
    carry_sc[...] = carry
    update([r[...].reshape(width, page) for r in k_refs], [r[...].reshape(width, page) for r in v_refs],
           [cum[i * FOX_H:(i + 1) * FOX_H] for i in range(pps)], None)

    @pl.when(j == pl.num_programs(1) - 1)
    def _():
        cumn = _dot3_r(lfn_ref[...], u_ref[...]) + carry
        kn = _pad_rows(kn_ref[...], page).T
        vn = _pad_rows(vn_ref[...], page).T
        colk = lax.broadcasted_iota(jnp.int32, (rows, page), 1)
        trow = lax.broadcasted_iota(jnp.int32, (rows, page), 0) & (ts - 1)
        update([kn], [vn], [cumn], colk <= trow)
        accn = acc_sc[...] / l_sc[...]
        o_ref[...] = jnp.sum(jnp.where(own, accn, 0.0).reshape(FOX_H, ts, width), axis=0)


def _fox_sample(layer, proj, lfn_t, page_table, cache_k, cache_v, cache_lf_t):
    db, ts, _ = proj.shape
    npages = page_table.shape[1]
    page = cache_k.shape[-1]
    width = FOX_H * FOX_DH
    pps = 32 if npages % 32 == 0 else (16 if npages % 16 == 0 else 1)
    upper = jnp.triu(jnp.ones((page, page), F32)).astype(BF16)
    ridx = jnp.arange(pps * FOX_H)
    pmat = jnp.logical_and((ridx[:, None] % FOX_H) == (ridx[None, :] % FOX_H),
                           (ridx[None, :] // FOX_H) < (ridx[:, None] // FOX_H)).astype(BF16)
    rows = FOX_H * ts

    def page_spec(shape, p_):
        return pl.BlockSpec((None, None) + shape,
                            lambda bi, j, pt, p_=p_: (layer, pt[bi, j * pps + p_]) + (0,) * len(shape))

    in_specs = [
        pl.BlockSpec((None, ts, width), lambda bi, j, pt: (bi, 0, C_FQ // width)),
        pl.BlockSpec((None, ts, width), lambda bi, j, pt: (bi, 0, C_FK // width)),
        pl.BlockSpec((None, ts, width), lambda bi, j, pt: (bi, 0, C_FV // width)),
        pl.BlockSpec((None, FOX_H, page), lambda bi, j, pt: (bi, 0, 0)),
        pl.BlockSpec((page, page), lambda bi, j, pt: (0, 0)),
        pl.BlockSpec((pps * FOX_H, pps * FOX_H), lambda bi, j, pt: (0, 0)),
    ]
    in_specs += [page_spec((FOX_H, FOX_DH, page), p_) for p_ in range(pps)]
    in_specs += [page_spec((FOX_H, FOX_DH, page), p_) for p_ in range(pps)]
    in_specs += [page_spec((FOX_H, page), p_) for p_ in range(pps)]
    grid_spec = pltpu.PrefetchScalarGridSpec(
        num_scalar_prefetch=1,
        grid=(db, npages // pps),
        in_specs=in_specs,
        out_specs=pl.BlockSpec((None, ts, width), lambda bi, j, pt: (bi, 0, 0)),
        scratch_shapes=[
            pltpu.VMEM((rows, width), BF16),
            pltpu.VMEM((rows, 1), F32),
            pltpu.VMEM((rows, 1), F32),
            pltpu.VMEM((rows, width), F32),
            pltpu.VMEM((FOX_H, 1), F32),
        ],
    )
    args = [proj, proj, proj, lfn_t, upper, pmat] + [cache_k] * pps + [cache_v] * pps + [cache_lf_t] * pps
    return pl.pallas_call(
        functools.partial(_fox_dec_kernel, pps=pps, ts=ts),
        grid_spec=grid_spec,
        out_shape=jax.ShapeDtypeStruct((db, ts, width), F32),
        compiler_params=_cparams(("parallel", "arbitrary"), 56),
        name="fox_sample",
    )(page_table, *args)


def _rw_prep_kernel(p_ref, sh0_ref, mu_ref, w2_ref, w0_ref, g2_ref, kk_ref, ka_ref, rk_ref,
                    e_ref, bl_ref,
                    kt_o, bt_o, kq_o, rt_o, v_o, gc_o, ab_o, g_o, bon_o, carry, *, tr, tv):
    i = pl.program_id(1)
    trb = p_ref.shape[0]

    @pl.when(i == 0)
    def _():
        carry[...] = sh0_ref[...]

    p = _pad_rows(p_ref[...], tr)
    rowi = lax.broadcasted_iota(jnp.int32, (tr, 1), 0)
    prev = jnp.where(rowi == 0, carry[...], pltpu.roll(p, 1, axis=0))
    carry[...] = p_ref[pl.ds(trb - 1, 1), :]
    xm = p + (prev - p) * mu_ref[...]
    r = xm[:, 0:RW_W]
    k = xm[:, RW_W:2 * RW_W]
    v = xm[:, 2 * RW_W:3 * RW_W]
    z = xm[:, 3 * RW_W:3 * RW_W + RW_LORA]
    gl = xm[:, 3 * RW_W + RW_LORA:]
    lane = lax.broadcasted_iota(jnp.int32, (1, RW_LORA), 1)
    zz = jnp.where(lane < RW_LORA // 2, jnp.tanh(z), z)
    z_hi = zz.astype(BF16)
    z_lo = (zz - z_hi.astype(F32)).astype(BF16)
    pre = (_dot(z_hi, w2_ref[0]) + _dot(z_hi, w2_ref[1]) + _dot(z_lo, w2_ref[0])) + w0_ref[...]
    w_log = -_softplus(-pre[:, :RW_W]) - 0.5
    logdec = -jnp.exp(w_log)
    a = _sigmoid(pre[:, RW_W:])
    g = _dot(_sigmoid(gl).astype(BF16), g2_ref[...])
    e = e_ref[...]

    def head_sum(x):
        return jnp.concatenate(
            [_dot3_r(x[:, c0:c0 + LANE], e) for c0 in range(0, RW_W, LANE)], axis=1)

    kk = k * kk_ref[...]
    kk = kk / jnp.maximum(jnp.sqrt(head_sum(kk * kk)), 1e-12)
    k2 = k * (1.0 + (a - 1.0) * ka_ref[...])
    bonus = head_sum(r * k2 * rk_ref[...]) * v
    if tv < tr:
        valid = rowi < tv
        logdec = jnp.where(valid, logdec, 0.0)
        kk = jnp.where(valid, kk, 0.0)
        k2 = jnp.where(valid, k2, 0.0)
        v = jnp.where(valid, v, 0.0)
        r = jnp.where(valid, r, 0.0)
    cl = _dot3_l(bl_ref[...], logdec)
    ct = jnp.concatenate(
        [jnp.broadcast_to(cl[c0 + RW_CHUNK - 1:c0 + RW_CHUNK, :], (RW_CHUNK, RW_W))
         for c0 in range(0, tr, RW_CHUNK)], axis=0)
    e_inv = jnp.exp(-cl)
    kt = kk * jnp.exp(cl - logdec)
    bt = kk * a * e_inv
    kt_o[...] = kt
    bt_o[...] = bt
    kq_o[...] = k2 * e_inv
    rt_o[...] = r * jnp.exp(cl)
    v_o[...] = v
    gc_o[...] = jnp.exp(ct)
    g_o[...] = g
    bon_o[...] = bonus
    lane2 = lax.broadcasted_iota(jnp.int32, (1, LANE), 1)
    for c in range(tr // RW_CHUNK):
        rs = slice(c * RW_CHUNK, (c + 1) * RW_CHUNK)
        for hp in range(RW_H // 2):
            ls = slice(hp * LANE, (hp + 1) * LANE)
            x = kt[rs, ls]
            y = bt[rs, ls].astype(BF16)
            for h in range(2):
                mh = (lane2 < RW_N) if h == 0 else (lane2 >= RW_N)
                xh = jnp.where(mh, x, 0.0).astype(BF16)
                ab_o[2 * hp + h, rs, :] = _dot_nt(xh, y)


def _rw_prep(proj, shift0, wl, tv):
    b, t, _ = proj.shape
    tp = max(t, RW_CHUNK)
    tr = min(tp, 256)
    trb = min(t, tr)
    n = tp // tr
    cidx = jnp.arange(tr) // RW_CHUNK
    same = cidx[:, None] == cidx[None, :]
    blk_tri = jnp.logical_and(same, jnp.arange(tr)[None, :] <= jnp.arange(tr)[:, None]).astype(BF16)
    hidx = jnp.arange(LANE) // RW_N
    e128 = (hidx[:, None] == hidx[None, :]).astype(BF16)

    def const(shape):
        return pl.BlockSpec(shape, lambda bi, i: (0,) * len(shape))

    tok = pl.BlockSpec((None, tr, RW_W), lambda bi, i: (bi, i, 0))
    outs = [jax.ShapeDtypeStruct((b, tp, RW_W), F32)] * 6
    return pl.pallas_call(
        functools.partial(_rw_prep_kernel, tr=tr, tv=tv if t < tr else tr),
        grid=(b, n),
        in_specs=[
            pl.BlockSpec((None, trb, RW_PROJ), lambda bi, i: (bi, i, C_RW // RW_PROJ)),
            pl.BlockSpec((None, 1, RW_PROJ), lambda bi, i: (bi, 0, 0)),
            const((1, RW_PROJ)), const((2, RW_LORA, 2 * RW_W)), const((1, 2 * RW_W)),
            const((RW_GATE_R, RW_W)), const((1, RW_W)), const((1, RW_W)), const((1, RW_W)),
            const((LANE, LANE)), const((tr, tr)),
        ],
        out_specs=[tok] * 6 + [pl.BlockSpec((None, RW_H, tr, RW_N), lambda bi, i: (bi, 0, i, 0)), tok, tok],
        out_shape=outs + [jax.ShapeDtypeStruct((b, RW_H, tp, RW_N), F32)] + outs[:2],
        scratch_shapes=[pltpu.VMEM((1, RW_PROJ), F32)],
        compiler_params=_cparams(("parallel", "arbitrary"), 48),
        name="rwkv_prep",
    )(proj, shift0, wl["mu"], wl["w2a2"], wl["w0a0"], wl["g2"], wl["kk"], wl["ka"], wl["rk"],
      e128, blk_tri)


def _rw_solve_kernel(a_ref, t_ref, at_sc, tt_sc):
    c = RW_CHUNK
    per = LANE // c
    for jb in range(c * c // LANE):
        blk = a_ref[:, jb * LANE:(jb + 1) * LANE].T
        at_sc[jb * per:(jb + 1) * per] = blk.reshape(per, c, LANE)
    rows = lax.broadcasted_iota(jnp.int32, (c, LANE), 0)
    rb = 4

    gr = 16
    ng = c // gr

    def body_t(tb, carry_):
        t0 = tb * rb
        parts = tuple(tuple(jnp.where(rows[g * gr:(g + 1) * gr] == t0 + r, 1.0, 0.0) for g in range(ng))
                      for r in range(rb))
        for k in range(ng):
            live = (k + 1) * gr

            def body_u(u, prt, k=k, live=live):
                tu = tt_sc[u, 0:live, :]
                out = []
                for r in range(rb):
                    a = at_sc[t0 + r, pl.ds(u, 1), :]
                    out.append(tuple(prt[r][g] - a * tu[g * gr:(g + 1) * gr] if g <= k else prt[r][g]
                                     for g in range(ng)))
                return tuple(out)

            parts = lax.fori_loop(k * gr, jnp.clip(t0, k * gr, live), body_u, parts)
        accs = [jnp.concatenate(parts[r], axis=0) for r in range(rb)]
        done = []
        for r in range(rb):
            acc = accs[r]
            for r2 in range(r):
                acc = acc - at_sc[t0 + r, pl.ds(t0 + r2, 1), :] * done[r2]
            tt_sc[t0 + r] = acc
            done.append(acc)
        return carry_

    lax.fori_loop(0, c // rb, body_t, 0)
    for jb in range(c * c // LANE):
        blk = tt_sc[jb * per:(jb + 1) * per].reshape(LANE, LANE)
        t_ref[:, jb * LANE:(jb + 1) * LANE] = blk.T


def _rw_solve(a2d):
    nch, cc = a2d.shape
    return pl.pallas_call(
        _rw_solve_kernel,
        grid=(nch // LANE,),
        in_specs=[pl.BlockSpec((LANE, cc), lambda i: (i, 0))],
        out_specs=pl.BlockSpec((LANE, cc), lambda i: (i, 0)),
        out_shape=jax.ShapeDtypeStruct((nch, cc), F32),
        scratch_shapes=[pltpu.VMEM((RW_CHUNK, RW_CHUNK, LANE), F32)] * 2,
        compiler_params=_cparams(("parallel",), 40),
        name="rwkv_solve",
    )(a2d)


def _rw_chain_kernel(kt_ref, bt_ref, kq_ref, rt_ref, v_ref, gc_ref, tm_ref, g_ref, bon_ref,
                     lnw_ref, lnb_ref, e_ref, h0_ref, o_ref, hout_ref,
                     h_sc, gm_sc, nm_sc, qe_sc, y_sc, *, tt, to):
    i = pl.program_id(1)
    cn = RW_CHUNK
    nck = tt // cn
    npair = RW_H // 2

    @pl.when(i == 0)
    def _():
        h_sc[...] = h0_ref[...]

    lane = lax.broadcasted_iota(jnp.int32, (1, LANE), 1)
    lo = lane < RW_N
    ri = lax.broadcasted_iota(jnp.int32, (cn, LANE), 0)
    ci = lax.broadcasted_iota(jnp.int32, (cn, LANE), 1) & (RW_N - 1)
    strict = ci < ri
    incl = ci <= ri
    rd = lax.broadcasted_iota(jnp.int32, (LANE, LANE), 0)
    cd = lax.broadcasted_iota(jnp.int32, (LANE, LANE), 1)
    eye = rd == cd
    same_head = (rd >> 6) == (cd >> 6)

    def stack(x):
        return jnp.concatenate([jnp.where(lo, x, 0.0), jnp.where(lo, 0.0, x)], axis=0).astype(BF16)

    units = [(c, p) for c in range(nck) for p in range(npair)]
    cs = range(len(units))
    rss = [slice(c * cn, (c + 1) * cn) for c, _ in units]
    lss = [slice(p * LANE, (p + 1) * LANE) for _, p in units]
    kt = [kt_ref[rs, ls] for rs, ls in zip(rss, lss)]
    bt = [bt_ref[rs, ls] for rs, ls in zip(rss, lss)]
    kq = [kq_ref[rs, ls] for rs, ls in zip(rss, lss)]
    rt = [rt_ref[rs, ls] for rs, ls in zip(rss, lss)]
    v = [v_ref[rs, ls] for rs, ls in zip(rss, lss)]
    gc = [gc_ref[rs, ls] for rs, ls in zip(rss, lss)]
    ks = [stack(kq[c]) for c in cs]
    vs = [stack(v[c]) for c in cs]
    tcat = [jnp.concatenate([tm_ref[2 * p, rs, :], tm_ref[2 * p + 1, rs, :]], axis=1).astype(BF16)
            for (_, p), rs in zip(units, rss)]
    ak = [jnp.where(strict, _dot_nt(kt[c].astype(BF16), ks[c]), 0.0).astype(BF16) for c in cs]
    kd = [_dot(tcat[c], stack(kt[c])) for c in cs]
    bb = [jnp.where(incl, _dot_nt(rt[c].astype(BF16), stack(bt[c])), 0.0).astype(BF16) for c in cs]
    bk = [jnp.where(incl, _dot_nt(rt[c].astype(BF16), ks[c]), 0.0).astype(BF16) for c in cs]
    akv = [_dot(ak[c], vs[c]) for c in cs]
    vd = [_dot(tcat[c], stack(akv[c])) for c in cs]
    bh = [(bt[c] * gc[c]).astype(BF16) for c in cs]
    for c in cs:
        gm_sc[c] = (jnp.where(eye, gc[c][0:1, :], 0.0)
                    - jnp.where(same_head, _dot_tn(bh[c], kd[c].astype(BF16)), 0.0))
        qe_sc[rss[c], lss[c]] = rt[c] - _dot(bb[c], stack(kd[c]))
    for c in cs:
        kh = (kq[c] * gc[c]).astype(BF16)
        nm_sc[c] = jnp.where(same_head, _dot_tn(
            jnp.concatenate([kh, bh[c]], axis=0),
            jnp.concatenate([v[c].astype(BF16), (-vd[c]).astype(BF16)], axis=0)), 0.0)
        y_sc[rss[c], lss[c]] = _dot(jnp.concatenate([bk[c], bb[c]], axis=1),
                                    jnp.concatenate([vs[c], stack(-vd[c])], axis=0))

    hcur = [h_sc[p] for p in range(npair)]
    for u, (c, p) in enumerate(units):
        hb = hcur[p].astype(BF16)
        y_sc[rss[u], lss[u]] = y_sc[rss[u], lss[u]] + _dot(qe_sc[rss[u], lss[u]].astype(BF16), hb)
        hcur[p] = _dot(gm_sc[u].astype(BF16), hb) + nm_sc[u]
    for p in range(npair):
        h_sc[p] = hcur[p]

    e = e_ref[...]
    for p in range(npair):
        ls = slice(p * LANE, (p + 1) * LANE)
        y = y_sc[:, ls]
        mean = _dot3_r(y, e) * (1.0 / RW_N)
        yc = y - mean
        var = _dot3_r(yc * yc, e) * (1.0 / RW_N)
        out = ((yc * lax.rsqrt(var + RW_GN_EPS) * lnw_ref[:, ls] + lnb_ref[:, ls] + bon_ref[:, ls])
               * g_ref[:, ls])
        o_ref[:, ls] = out[:to]

    @pl.when(i == pl.num_programs(1) - 1)
    def _():
        for p in range(npair):
            hout_ref[p] = hcur[p]


def _rw_chain(prep, tmat, h0, lnw, lnb, t_out):
    kt, bt, kq, rt, v, gc, _, g, bon = prep
    b, tp, _ = kt.shape
    tt = min(tp, 128)
    to = min(t_out, tt)
    hp = RW_H // 2
    hidx = jnp.arange(LANE) // RW_N
    e128 = (hidx[:, None] == hidx[None, :]).astype(BF16)
    tok = pl.BlockSpec((None, tt, RW_W), lambda bi, i: (bi, i, 0))
    vec = pl.BlockSpec((1, RW_W), lambda bi, i: (0, 0))
    st = pl.BlockSpec((None, hp, LANE, LANE), lambda bi, i: (bi, 0, 0, 0))
    nunit = tt // RW_CHUNK * hp
    return pl.pallas_call(
        functools.partial(_rw_chain_kernel, tt=tt, to=to),
        grid=(b, tp // tt),
        in_specs=[tok] * 6 + [
            pl.BlockSpec((None, RW_H, tt, RW_N), lambda bi, i: (bi, 0, i, 0)),
            tok, tok, vec, vec,
            pl.BlockSpec((LANE, LANE), lambda bi, i: (0, 0)),
            st,
        ],
        out_specs=[pl.BlockSpec((None, to, RW_W), lambda bi, i: (bi, i, 0)), st],
        out_shape=[
            jax.ShapeDtypeStruct((b, t_out, RW_W), F32),
            jax.ShapeDtypeStruct((b, hp, LANE, LANE), F32),
        ],
        scratch_shapes=[
            pltpu.VMEM((hp, LANE, LANE), F32),
            pltpu.VMEM((nunit, LANE, LANE), F32),
            pltpu.VMEM((nunit, LANE, LANE), F32),
            pltpu.VMEM((tt, RW_W), F32),
            pltpu.VMEM((tt, RW_W), F32),
        ],
        compiler_params=_cparams(("parallel", "arbitrary"), 40),
        name="rwkv_chain",
    )(kt, bt, kq, rt, v, gc, tmat, g, bon, lnw, lnb, e128, h0)


def _rwkv(proj, shift0, s0, wl, t):
    b = proj.shape[0]
    prep = _rw_prep(proj, shift0, wl, t)
    ab = prep[6]
    tp = ab.shape[2]
    cc = RW_CHUNK * RW_CHUNK
    a2d = ab.reshape(-1, cc)
    nch = a2d.shape[0]
    nch_p = -(-nch // LANE) * LANE
    if nch_p != nch:
        a2d = jnp.pad(a2d, ((0, nch_p - nch), (0, 0)))
    tmat = _rw_solve(a2d)[:nch].reshape(b, RW_H, tp, RW_N)
    st = jnp.swapaxes(s0, -1, -2).reshape(b, RW_H // 2, 2, RW_N, RW_N)
    z = jnp.zeros_like(st[:, :, 0])
    h0 = jnp.concatenate([jnp.concatenate([st[:, :, 0], z], -1), jnp.concatenate([z, st[:, :, 1]], -1)], -2)
    o, hout = _rw_chain(prep, tmat, h0, wl["lnw"], wl["lnb"], t)
    s_new = jnp.stack([hout[:, :, :RW_N, :RW_N], hout[:, :, RW_N:, RW_N:]], axis=2)
    s_new = jnp.swapaxes(s_new.reshape(b, RW_H, RW_N, RW_N), -1, -2)
    return o, s_new


def _merge_kernel(x_ref, gm_ref, oa_ref, ob_ref, oc_ref, ga_ref, gb_ref, gc_ref,
                  wa_ref, wb_ref, wc_ref, wo_ref, o_ref):
    merged = (_sigmoid(ga_ref[...]) * _dot(oa_ref[...].astype(BF16), wa_ref[...])
              + _sigmoid(gb_ref[...]) * _dot(ob_ref[...].astype(BF16), wb_ref[...])
              + _sigmoid(gc_ref[...]) * _dot(oc_ref[...].astype(BF16), wc_ref[...]))
    mix = _dot(merged.astype(BF16), wo_ref[...])
    o_ref[...] = x_ref[...] + gm_ref[...] * mix


def _merge(x, g1, o_ret, o_fox, o_rw, proj, wl):
    g_, r, d = x.shape
    tm = min(r, 512)
    rm = g1.shape[1]
    w = o_ret.shape[-1]

    def tok(width, cb=0):
        return pl.BlockSpec((None, tm, width), lambda g, i, cb=cb: (g, i, cb))

    def const(shape):
        return pl.BlockSpec(shape, lambda g, i: (0, 0))

    return pl.pallas_call(
        _merge_kernel,
        grid=(g_, r // tm),
        in_specs=[
            tok(d), _row_spec(rm, tm, d), tok(w), tok(w), tok(w),
            tok(d, C_GATE // d), tok(d, C_GATE // d + 1), tok(d, C_GATE // d + 2),
            const((w, d)), const((w, d)), const((w, d)), const((d, d)),
        ],
        out_specs=tok(d),
        out_shape=jax.ShapeDtypeStruct((g_, r, d), F32),
        compiler_params=_cparams(("parallel", "parallel"), 48),
        name="merge_out",
    )(x, g1, o_ret, o_fox, o_rw, proj, proj, proj, wl["w_br_ret"], wl["w_br_fox"], wl["w_br_rw"], wl["w_out"])


def _ffn_kernel(x_ref, sc_ref, sh_ref, gm_ref, g_ref, wu_ref, wd_ref, nf_ref, o_ref, h_sc, acc_sc, *, final):
    f = pl.program_id(2)

    @pl.when(f == 0)
    def _():
        h_sc[...] = _norm_mod(x_ref[...], sc_ref[...], sh_ref[...], g_ref[...]).astype(BF16)
        acc_sc[...] = jnp.zeros_like(acc_sc)

    u = jnp.maximum(_dot(h_sc[...], wu_ref[...]), 0.0)
    acc_sc[...] += _dot((u * u).astype(BF16), wd_ref[...])

    @pl.when(f == pl.num_programs(2) - 1)
    def _():
        xn = x_ref[...] + gm_ref[...] * acc_sc[...]
        if final:
            ms = jnp.mean(xn * xn, axis=-1, keepdims=True)
            xn = xn * lax.rsqrt(ms + NORM_EPS) * nf_ref[...]
        o_ref[...] = xn


def _ffn(x, sc, sh, g2, gamma, w_up, w_down, norm_final, final):
    g_, r, d = x.shape
    dff = w_up.shape[1]
    tm = min(r, 1024)
    tf = 1024
    rm = sc.shape[1]
    return pl.pallas_call(
        functools.partial(_ffn_kernel, final=final),
        grid=(g_, r // tm, dff // tf),
        in_specs=[
            pl.BlockSpec((None, tm, d), lambda g, i, f: (g, i, 0)),
            _row_spec(rm, tm, d), _row_spec(rm, tm, d), _row_spec(rm, tm, d),
            pl.BlockSpec((1, d), lambda g, i, f: (0, 0)),
            pl.BlockSpec((d, tf), lambda g, i, f: (0, f)),
            pl.BlockSpec((tf, d), lambda g, i, f: (f, 0)),
            pl.BlockSpec((1, d), lambda g, i, f: (0, 0)),
        ],
        out_specs=pl.BlockSpec((None, tm, d), lambda g, i, f: (g, i, 0)),
        out_shape=jax.ShapeDtypeStruct((g_, r, d), F32),
        scratch_shapes=[pltpu.VMEM((tm, d), BF16), pltpu.VMEM((tm, d), F32)],
        compiler_params=_cparams(("parallel", "parallel", "arbitrary"), 48),
        name="ffn",
    )(x, sc, sh, g2, gamma, w_up, w_down, norm_final)


def _rope_tables(q0, t):
    d = RET_D
    inv = 1.0 / (ROPE_BASE ** (jnp.arange(0, d, 2, dtype=F32) / d))
    ang = (q0 + jnp.arange(t)).astype(F32)[:, None] * inv[None, :]
    cos, sin = jnp.cos(ang), jnp.sin(ang)
    return jnp.concatenate([cos, cos], -1), jnp.concatenate([-sin, sin], -1)


def _prep_layer_weights(l, w):
    w_in = w["w_in"][l]
    d = w_in.shape[0]
    ret_cols, fox_qkv = 2048, 1536
    f0 = ret_cols + fox_qkv
    rw0 = f0 + FOX_H
    g0 = rw0 + RW_PROJ
    w_t = jnp.swapaxes(w_in, 0, 1)
    packed = jnp.concatenate([
        w_t[rw0:g0], w_t[f0:rw0], jnp.zeros((2 * LANE - FOX_H, d), F32),
        w_t[g0:], w_t[ret_cols:f0], w_t[:ret_cols]], axis=0).astype(BF16)
    zl = jnp.zeros((RW_LORA // 2, RW_W), F32)
    w2a2 = jnp.concatenate([jnp.concatenate([w["rw_w2"][l], zl], 1),
                            jnp.concatenate([zl, w["rw_a2"][l]], 1)], 0)
    w2a2_hi = w2a2.astype(BF16)
    w2a2 = jnp.stack([w2a2_hi, (w2a2 - w2a2_hi.astype(F32)).astype(BF16)])
    row = lambda a: a.reshape(1, -1)
    return {
        "w_in": packed,
        "norm_mix": row(w["norm_mix"][l]), "norm_ffn": row(w["norm_ffn"][l]),
        "bf": jnp.pad(w["fox_bf"][l], (0, LANE - FOX_H)).reshape(1, LANE),
        "mu": row(w["rw_mu"][l]), "w2a2": w2a2,
        "w0a0": jnp.concatenate([w["rw_w0"][l], w["rw_a0"][l]]).reshape(1, -1),
        "g2": w["rw_g2"][l].astype(BF16),
        "kk": row(w["rw_kk"][l]), "ka": row(w["rw_ka"][l]), "rk": row(w["rw_rk"][l]),
        "lnw": row(w["rw_lnw"][l]), "lnb": row(w["rw_lnb"][l]),
        "w_br_ret": w["w_br_ret"][l].astype(BF16), "w_br_fox": w["w_br_fox"][l].astype(BF16),
        "w_br_rw": w["w_br_rw"][l].astype(BF16), "w_out": w["w_out"][l].astype(BF16),
        "w_up": w["w_up"][l].astype(BF16), "w_down": w["w_down"][l].astype(BF16),
    }


def _ret_log_gamma_table():
    lg = jnp.log1p(-jnp.exp2(-5.0 - jnp.arange(RET_H, dtype=F32)))
    return jnp.broadcast_to(lg[:, None, None], (RET_H, 1, RET_D))


def _layer(x, mod, q0, past, ret0, wkv0, shift0, wl, norm_final, final, flat):
    b, t, d = x.shape
    sh1, sc1, g1, sh2, sc2, g2 = [m[:, None, :] for m in jnp.split(mod, 6, axis=-1)]
    if flat:
        rep = lambda m: jnp.repeat(m, t, axis=1).reshape(1, b * t, d)
        sh1, sc1, g1, sh2, sc2, g2 = [rep(m) for m in (sh1, sc1, g1, sh2, sc2, g2)]
        xd = x.reshape(1, b * t, d)
    else:
        xd = x
    if past is None:
        proj_d, k_t, v_t = _norm_mm(xd, sc1, sh1, wl["norm_mix"], wl["w_in"], (C_FK, C_FV))
    else:
        proj_d, = _norm_mm(xd, sc1, sh1, wl["norm_mix"], wl["w_in"])
    proj = proj_d.reshape(b, t, PROJ_W)

    cosf, sinf = _rope_tables(q0, t)
    o_ret, r_new = _retention(proj, cosf, sinf, _ret_log_gamma_table(), ret0)

    lf, cum = _fox_gate(proj_d, wl["bf"])
    lf = lf.reshape(b, t, LANE)[:, :, :FOX_H]
    if past is None:
        o_fox = _fox_prompt(proj, v_t, cum)
    else:
        layer, page_table, cache_k, cache_v, cache_lf_t = past
        page = cache_k.shape[-1]
        lfn_t =jnp.pad(jnp.swapaxes(lf, 1, 2), ((0, 0), (0, 0), (0, page - t)))
        o_fox = _fox_sample(layer, proj, lfn_t, page_table, cache_k, cache_v, cache_lf_t)

    o_rw, s_new = _rwkv(proj, shift0[:, None, :], wkv0, wl, t)

    dn = lambda a: a.reshape(xd.shape[0], xd.shape[1], -1)
    x1 = _merge(xd, g1, dn(o_ret), dn(o_fox), dn(o_rw), proj_d, wl)
    x2 = _ffn(x1, sc2, sh2, g2, wl["norm_ffn"], wl["w_up"], wl["w_down"], norm_final, final)
    if past is None:
        fk = k_t.reshape(b, FOX_H, FOX_DH, t).transpose(0, 3, 1, 2)
        fv = v_t.reshape(b, FOX_H, FOX_DH, t).transpose(0, 3, 1, 2)
    else:
        fk = proj[:, :, C_FK:C_FV].reshape(b, t, FOX_H, FOX_DH)
        fv = proj[:, :, C_FV:C_RQ].reshape(b, t, FOX_H, FOX_DH)
    shift = proj[:, t - 1, C_RW:C_RW + RW_PROJ]
    return x2.reshape(b, t, d), (fk, fv, lf, r_new, s_new, shift)


def kernel(x_prompt, x_sample, c_prompt, c_sample, cache_k, cache_v, cache_logf, page_table, state_ret, state_wkv, state_shift, ada_w, ada_b, norm_mix, norm_ffn, w_in, fox_bf, rw_mu, rw_w0, rw_w2, rw_a0, rw_a2, rw_g2, rw_kk, rw_ka, rw_rk, rw_lnw, rw_lnb, w_br_ret, w_br_fox, w_br_rw, w_out, w_up, w_down, norm_final):
    depth = w_in.shape[0]
    b, t, d = x_prompt.shape
    db, ts, _ = x_sample.shape
    n_pool, page = cache_k.shape[1], cache_k.shape[2]
    past_len = page_table.shape[1] * page
    w = dict(w_in=w_in, norm_mix=norm_mix, norm_ffn=norm_ffn, fox_bf=fox_bf, rw_mu=rw_mu, rw_w0=rw_w0,
             rw_w2=rw_w2, rw_a0=rw_a0, rw_a2=rw_a2, rw_g2=rw_g2, rw_kk=rw_kk, rw_ka=rw_ka, rw_rk=rw_rk,
             rw_lnw=rw_lnw, rw_lnb=rw_lnb, w_br_ret=w_br_ret, w_br_fox=w_br_fox, w_br_rw=w_br_rw,
             w_out=w_out, w_up=w_up, w_down=w_down)
    rows = b + db
    rows_p = -(-rows // 8) * 8
    c_all = jnp.pad(jnp.concatenate([c_prompt, c_sample], 0), ((0, rows_p - rows), (0, 0)))
    mod = _modulation(c_all, ada_w, ada_b)
    ck = jnp.transpose(cache_k, (0, 1, 3, 4, 2))
    cv = jnp.transpose(cache_v, (0, 1, 3, 4, 2))
    clf_t = jnp.swapaxes(cache_logf, 2, 3)
    nf = norm_final.reshape(1, d)
    xp, xs = x_prompt, x_sample
    st_p, st_s = [], []
    for l in range(depth):
        wl = _prep_layer_weights(l, w)
        final = l == depth - 1
        xp, sp = _layer(xp, mod[l, :b], 0, None, jnp.zeros((b, RET_H, RET_D, RET_D), F32),
                        jnp.zeros((b, RW_H, RW_N, RW_N), F32), jnp.zeros((b, RW_PROJ), F32),
                        wl, nf, final, False)
        st_p.append(sp)
        xs, ss = _layer(xs, mod[l, b:rows], past_len, (l, page_table, ck, cv, clf_t), state_ret[l],
                        state_wkv[l], state_shift[l], wl, nf, final, True)
        st_s.append(ss)
    outs = [xp, xs]
    for st in (st_p, st_s):
        for i in range(6):
            outs.append(jnp.stack([s[i] for s in st]))
    y_p, y_s, k_p, v_p, lf_p, ret_p, wkv_p, sh_p, k_s, v_s, lf_s, ret_s, wkv_s, sh_s = outs
    return (y_p, y_s, k_p, v_p, lf_p, ret_p, wkv_p, sh_p, k_s, v_s, lf_s, ret_s, wkv_s, sh_s)
```

```python
import functools

import jax
import jax.numpy as jnp
from jax import lax
from jax.experimental import pallas as pl
from jax.experimental.pallas import tpu as pltpu

F32 = jnp.float32
BF16 = jnp.bfloat16
HIGHEST = lax.Precision.HIGHEST

RET_H, RET_D = 4, 128
FOX_H, FOX_DH = 8, 64
RW_H, RW_N = 8, 64
RW_W = RW_H * RW_N
RW_LORA = 128
RW_GATE_R = 128
RW_PROJ = 3 * RW_W + RW_LORA + RW_GATE_R
RET_GN_EPS = 1e-6
RW_GN_EPS = 64e-5
NORM_EPS = 1e-6
ROPE_BASE = 10000.0
NEG = -1e30
LOG2E = 1.4426950408889634

LANE = 128
C_RW = 0
C_F = 1792
C_GATE = 2048
C_FQ, C_FK, C_FV = 5120, 5632, 6144
C_RQ, C_RK, C_RV, C_RG = 6656, 7168, 7680, 8192
PROJ_W = 8704

RW_CHUNK = 64
RET_CHUNK = 128


def _cparams(sem, vmem_mb):
    return pltpu.CompilerParams(dimension_semantics=sem, vmem_limit_bytes=vmem_mb << 20)


def _dot(a, b):
    return jnp.dot(a, b, preferred_element_type=F32)


def _dot_nt(a, b):
    return lax.dot_general(a, b, (((1,), (1,)), ((), ())), preferred_element_type=F32)


def _dot_tn(a, b):
    return lax.dot_general(a, b, (((0,), (0,)), ((), ())), preferred_element_type=F32)


def _split3(x):
    hi = x.astype(BF16)
    r1 = x - hi.astype(F32)
    mid = r1.astype(BF16)
    lo = (r1 - mid.astype(F32)).astype(BF16)
    return hi, mid, lo


def _dot3_l(m01, x):
    hi, mid, lo = _split3(x)
    return _dot(m01, hi) + _dot(m01, mid) + _dot(m01, lo)


def _dot3_r(x, m01):
    hi, mid, lo = _split3(x)
    return _dot(hi, m01) + _dot(mid, m01) + _dot(lo, m01)


def _sigmoid(x):
    return 1.0 / (1.0 + jnp.exp(-x))


def _softplus(x):
    return jnp.maximum(x, 0.0) + jnp.log(1.0 + jnp.exp(-jnp.abs(x)))


def _pad_rows(a, rows):
    if a.shape[0] == rows:
        return a
    return jnp.concatenate([a, jnp.zeros((rows - a.shape[0],) + a.shape[1:], a.dtype)], axis=0)


def _mod_kernel(c_ref, w_ref, b_ref, o_ref):
    c = c_ref[...]
    s = c * _sigmoid(c)
    o_ref[...] = jnp.dot(s, w_ref[...], precision=HIGHEST, preferred_element_type=F32) + b_ref[...]


def _modulation(c_all, ada_w, ada_b):
    depth, d, n = ada_w.shape
    rows = c_all.shape[0]
    tn = 1536
    return pl.pallas_call(
        _mod_kernel,
        grid=(depth, n // tn),
        in_specs=[
            pl.BlockSpec((rows, d), lambda l, j: (0, 0)),
            pl.BlockSpec((None, d, tn), lambda l, j: (l, 0, j)),
            pl.BlockSpec((None, 1, tn), lambda l, j: (l, 0, j)),
        ],
        out_specs=pl.BlockSpec((None, rows, tn), lambda l, j: (l, 0, j)),
        out_shape=jax.ShapeDtypeStruct((depth, rows, n), F32),
        compiler_params=_cparams(("parallel", "parallel"), 40),
        name="ada_mod",
    )(c_all, ada_w, ada_b.reshape(depth, 1, n))


def _norm_mod(x, sc, sh, g):
    ms = jnp.mean(x * x, axis=-1, keepdims=True)
    y = x * lax.rsqrt(ms + NORM_EPS) * g
    return y * (1.0 + sc) + sh


def _norm_mm_kernel(x_ref, sc_ref, sh_ref, g_ref, w_ref, o_ref, *rest, t_blocks):
    h_ref = rest[-1]
    j = pl.program_id(2)

    @pl.when(j == 0)
    def _():
        h_ref[...] = _norm_mod(x_ref[...], sc_ref[...], sh_ref[...], g_ref[...]).astype(BF16)

    y = _dot_nt(h_ref[...], w_ref[...])
    o_ref[...] = y
    for n_, jb in enumerate(t_blocks):
        @pl.when(j == jb)
        def _(n_=n_):
            rest[n_][...] = y.T


def _row_spec(rm, tm, d):
    if rm == 1:
        return pl.BlockSpec((None, 1, d), lambda g, i, *_: (g, 0, 0))
    return pl.BlockSpec((None, tm, d), lambda g, i, *_: (g, i, 0))


def _norm_mm(x, sc, sh, gamma, w_t, t_cols=()):
    g_, r, d = x.shape
    n = w_t.shape[0]
    tm = min(r, 2048)
    tn = 512
    rm = sc.shape[1]
    out_specs = [pl.BlockSpec((None, tm, tn), lambda g, i, j: (g, i, j))]
    out_shape = [jax.ShapeDtypeStruct((g_, r, n), F32)]
    t_blocks = tuple(c0 // tn for c0 in t_cols)
    for _ in t_blocks:
        out_specs.append(pl.BlockSpec((None, tn, tm), lambda g, i, j: (g, 0, i)))
        out_shape.append(jax.ShapeDtypeStruct((g_, tn, r), F32))
    return pl.pallas_call(
        functools.partial(_norm_mm_kernel, t_blocks=t_blocks),
        grid=(g_, r // tm, n // tn),
        in_specs=[
            pl.BlockSpec((None, tm, d), lambda g, i, j: (g, i, 0)),
            _row_spec(rm, tm, d),
            _row_spec(rm, tm, d),
            pl.BlockSpec((1, d), lambda g, i, j: (0, 0)),
            pl.BlockSpec((tn, d), lambda g, i, j: (j, 0)),
        ],
        out_specs=out_specs,
        out_shape=out_shape,
        scratch_shapes=[pltpu.VMEM((tm, d), BF16)],
        compiler_params=_cparams(("parallel", "parallel", "arbitrary"), 56),
        name="norm_in_proj",
    )(x, sc, sh, gamma, w_t)


def _fox_gate_kernel(f_ref, b_ref, tri_ref, lf_ref, cum_ref, carry_ref):
    @pl.when(pl.program_id(1) == 0)
    def _():
        carry_ref[...] = jnp.zeros_like(carry_ref)

    z = f_ref[...] + b_ref[...]
    lf = -_softplus(-z)
    lf_ref[...] = lf
    cum = _dot3_l(tri_ref[...], lf) + carry_ref[...]
    tb = cum.shape[0]
    carry_ref[...] = cum[tb - 1:tb, :]
    cum2 = cum * LOG2E
    for h in range(FOX_H):
        cum_ref[h] = cum2[:, h:h + 1]


def _fox_gate(proj, bf_pad):
    g_, r, _ = proj.shape
    tb = min(r, 512)
    tri = jnp.tril(jnp.ones((tb, tb), F32)).astype(BF16)
    return pl.pallas_call(
        _fox_gate_kernel,
        grid=(g_, r // tb),
        in_specs=[
            pl.BlockSpec((None, tb, LANE), lambda g, i: (g, i, C_F // LANE)),
            pl.BlockSpec((1, LANE), lambda g, i: (0, 0)),
            pl.BlockSpec((tb, tb), lambda g, i: (0, 0)),
        ],
        out_specs=[
            pl.BlockSpec((None, tb, LANE), lambda g, i: (g, i, 0)),
            pl.BlockSpec((None, FOX_H, tb, 1), lambda g, i: (g, 0, i, 0)),
        ],
        out_shape=[jax.ShapeDtypeStruct((g_, r, LANE), F32),
                   jax.ShapeDtypeStruct((g_, FOX_H, r, 1), F32)],
        scratch_shapes=[pltpu.VMEM((1, LANE), F32)],
        compiler_params=_cparams(("parallel", "arbitrary"), 32),
        name="fox_gate",
    )(proj, bf_pad, tri)


def _ret_kernel(q_ref, k_ref, v_ref, g_ref, cos_ref, sin_ref, lg_ref, r0_ref,
                o_ref, rout_ref, r_sc, *, ce, nb):
    cp = RET_CHUNK
    c = pl.program_id(1)
    heads = range(RET_H)

    @pl.when(c == 0)
    def _():
        r_sc[...] = r0_ref[...]

    rows = max(nb * ce, cp)
    cosf = _pad_rows(cos_ref[...], rows)
    sinf = _pad_rows(sin_ref[...], rows)

    def rot(x):
        return x * cosf + pltpu.roll(x, RET_D // 2, axis=1) * sinf

    row = lax.broadcasted_iota(jnp.int32, (cp, RET_D), 0).astype(F32)
    ri = lax.broadcasted_iota(jnp.int32, (cp, cp), 0)
    ci = lax.broadcasted_iota(jnp.int32, (cp, cp), 1)
    diff = (ri - ci).astype(F32)
    lss = [slice(h * RET_D, (h + 1) * RET_D) for h in heads]
    q = [rot(_pad_rows(q_ref[:, ls], rows)).astype(BF16) for ls in lss]
    kf = [rot(_pad_rows(k_ref[:, ls], rows)) * (RET_D ** -0.5) for ls in lss]
    vb = [_pad_rows(v_ref[:, ls], rows).astype(BF16) for ls in lss]
    lgs = [lg_ref[h] for h in heads]
    qdec = [jnp.exp(lg * (row + 1.0)) for lg in lgs]
    kdec = [jnp.exp(lg * (ce - 1.0 - row)) for lg in lgs]
    dmask = [jnp.where(diff >= 0.0, jnp.exp(lg[:, :1] * jnp.maximum(diff, 0.0)), 0.0) for lg in lgs]
    gdec = [jnp.exp(lg[:, :1] * float(ce)) for lg in lgs]

    r_cur = [r_sc[h] for h in heads]
    outs = [[] for _ in heads]
    for j in range(nb):
        rs = slice(j * cp, (j + 1) * cp)
        s = [_dot_nt(q[h][rs], kf[h][rs].astype(BF16)) * dmask[h] for h in heads]
        cross = [_dot(q[h][rs], r_cur[h].astype(BF16)) * qdec[h] for h in heads]
        upd = [_dot_tn((kf[h][rs] * kdec[h]).astype(BF16), vb[h][rs]) for h in heads]
        inner = [_dot(s[h].astype(BF16), vb[h][rs]) for h in heads]
        for h in heads:
            r_cur[h] = gdec[h] * r_cur[h] + upd[h]
            outs[h].append(inner[h] + cross[h])
    for h in heads:
        r_sc[h] = r_cur[h]
        o = outs[h][0] if nb == 1 else jnp.concatenate(outs[h], axis=0)
        mu = jnp.mean(o, axis=-1, keepdims=True)
        oc = o - mu
        var = jnp.mean(oc * oc, axis=-1, keepdims=True)
        gg = _pad_rows(g_ref[:, lss[h]], rows)
        out = oc * lax.rsqrt(var + RET_GN_EPS) * (gg * _sigmoid(gg))
        o_ref[:, lss[h]] = out[:nb * ce]

    @pl.when(c == pl.num_programs(1) - 1)
    def _():
        for h in heads:
            rout_ref[h] = r_cur[h]


def _retention(proj, cosf, sinf, lg_tab, r0):
    b, t, _ = proj.shape
    ce = min(RET_CHUNK, t)
    n = t // ce
    nb = 4 if (ce == RET_CHUNK and n % 4 == 0) else 1
    tr = nb * ce

    w = RET_H * RET_D

    def col(c0):
        return pl.BlockSpec((None, tr, w), lambda bi, c, c0=c0: (bi, c, c0 // w))

    state = pl.BlockSpec((None, RET_H, RET_D, RET_D), lambda bi, c: (bi, 0, 0, 0))
    return pl.pallas_call(
        functools.partial(_ret_kernel, ce=ce, nb=nb),
        grid=(b, n // nb),
        in_specs=[
            col(C_RQ), col(C_RK), col(C_RV), col(C_RG),
            pl.BlockSpec((tr, RET_D), lambda bi, c: (c, 0)),
            pl.BlockSpec((tr, RET_D), lambda bi, c: (c, 0)),
            pl.BlockSpec((RET_H, 1, RET_D), lambda bi, c: (0, 0, 0)),
            state,
        ],
        out_specs=[pl.BlockSpec((None, tr, w), lambda bi, c: (bi, c, 0)), state],
        out_shape=[
            jax.ShapeDtypeStruct((b, t, w), F32),
            jax.ShapeDtypeStruct((b, RET_H, RET_D, RET_D), F32),
        ],
        scratch_shapes=[pltpu.VMEM((RET_H, RET_D, RET_D), F32)],
        compiler_params=_cparams(("parallel", "arbitrary"), 40),
        name="retention",
    )(proj, proj, proj, proj, cosf, sinf, lg_tab, r0)


def _fox_kernel(qi_ref, ki_ref, q_ref, k_ref, v_ref, ck_ref, o_ref, qt_sc, m_sc, l_sc, acc_sc, *, tq, tk, nh):
    qi = qi_ref[pl.program_id(2)]
    ki = ki_ref[pl.program_id(2)]
    head0 = lax.broadcasted_iota(jnp.int32, (LANE, 1), 0) < FOX_DH

    @pl.when(ki == 0)
    def _():
        q = q_ref[...] * (FOX_DH ** -0.5 * LOG2E)
        qt_sc[...] = q.T.astype(BF16)
        m_sc[...] = jnp.full_like(m_sc, NEG)
        l_sc[...] = jnp.zeros_like(l_sc)
        acc_sc[...] = jnp.zeros_like(acc_sc)

    def step(diagonal):
        kb = k_ref[...].astype(BF16)
        vt = v_ref[...].astype(BF16)
        qt = qt_sc[...]
        if diagonal:
            key = lax.broadcasted_iota(jnp.int32, (tk, tq), 0)
            qry = lax.broadcasted_iota(jnp.int32, (tk, tq), 1)
            causal = key <= qry
        scores = []
        for h in range(nh):
            ps = slice(h // 2 * LANE, (h // 2 + 1) * LANE)
            qtp = qt[ps, :]
            qth = jnp.where(head0 if h % 2 == 0 else jnp.logical_not(head0), qtp, jnp.zeros_like(qtp))
            scores.append(_dot(kb[:, ps], qth))
        for h in range(nh):
            hs = slice(h * FOX_DH, (h + 1) * FOX_DH)
            s = scores[h] - ck_ref[h]
            if diagonal:
                s = jnp.where(causal, s, NEG)
            m_prev = m_sc[h:h + 1, :]
            m_new = jnp.maximum(m_prev, jnp.max(s, axis=0, keepdims=True))
            alpha = jnp.exp2(m_prev - m_new)
            p = jnp.exp2(s - m_new)
            l_sc[h:h + 1, :] = alpha * l_sc[h:h + 1, :] + jnp.sum(p, axis=0, keepdims=True)
            m_sc[h:h + 1, :] = m_new
            acc_sc[hs, :] = alpha * acc_sc[hs, :] + _dot(vt[hs, :], p.astype(BF16))

    @pl.when(ki < qi)
    def _():
        step(False)

    @pl.when(ki == qi)
    def _():
        step(True)
        linv = jnp.concatenate(
            [jnp.broadcast_to(1.0 / l_sc[h:h + 1, :], (FOX_DH, tq)) for h in range(nh)], axis=0)
        o_ref[...] = (acc_sc[...] * linv).T


def _fox_prompt(proj, v_t, cum_c):
    b, t, _ = proj.shape
    tq = tk = min(t, 512)
    nq = t // tq
    nh = 4
    w = nh * FOX_DH
    hp = FOX_H // nh
    pairs = [(qi, ki) for qi in range(nq) for ki in range(qi + 1)]
    qi_tab = jnp.array([p_[0] for p_ in pairs], jnp.int32)
    ki_tab = jnp.array([p_[1] for p_ in pairs], jnp.int32)
    grid_spec = pltpu.PrefetchScalarGridSpec(
        num_scalar_prefetch=2,
        grid=(b, hp, len(pairs)),
        in_specs=[
            pl.BlockSpec((None, tq, w), lambda bi, p, s, qt, kt: (bi, qt[s], C_FQ // w + p)),
            pl.BlockSpec((None, tk, w), lambda bi, p, s, qt, kt: (bi, kt[s], C_FK // w + p)),
            pl.BlockSpec((None, w, tk), lambda bi, p, s, qt, kt: (bi, p, kt[s])),
            pl.BlockSpec((None, nh, tk, 1), lambda bi, p, s, qt, kt: (bi, p, kt[s], 0)),
        ],
        out_specs=pl.BlockSpec((None, tq, w), lambda bi, p, s, qt, kt: (bi, qt[s], p)),
        scratch_shapes=[
            pltpu.VMEM((w, tq), BF16),
            pltpu.VMEM((nh, tq), F32),
            pltpu.VMEM((nh, tq), F32),
            pltpu.VMEM((w, tq), F32),
        ],
    )
    return pl.pallas_call(
        functools.partial(_fox_kernel, tq=tq, tk=tk, nh=nh),
        grid_spec=grid_spec,
        out_shape=jax.ShapeDtypeStruct((b, t, FOX_H * FOX_DH), F32),
        compiler_params=_cparams(("parallel", "parallel", "arbitrary"), 40),
        name="fox_prompt",
    )(qi_tab, ki_tab, proj, proj, v_t, cum_c)


def _fox_dec_kernel(pt_ref, q_ref, kn_ref, vn_ref, lfn_ref, u_ref, pm_ref, *rest, pps, ts):
    k_refs = rest[:pps]
    v_refs = rest[pps:2 * pps]
    lf_refs = rest[2 * pps:3 * pps]
    o_ref, qbd, m_sc, l_sc, acc_sc, carry_sc = rest[3 * pps:]
    j = pl.program_id(1)
    rows = FOX_H * ts
    width = FOX_H * FOX_DH
    page = k_refs[0].shape[-1]
    assert ts & (ts - 1) == 0 and FOX_DH & (FOX_DH - 1) == 0
    rowh = lax.broadcasted_iota(jnp.int32, (rows, width), 0) >> (ts.bit_length() - 1)
    colh = lax.broadcasted_iota(jnp.int32, (rows, width), 1) >> (FOX_DH.bit_length() - 1)
    own = rowh == colh

    @pl.when(j == 0)
    def _():
        q = q_ref[...] * (FOX_DH ** -0.5 * LOG2E)
        qt = jnp.broadcast_to(q[None], (FOX_H, ts, width)).reshape(rows, width)
        qbd[...] = jnp.where(own, qt, 0.0).astype(BF16)
        m_sc[...] = jnp.full_like(m_sc, NEG)
        l_sc[...] = jnp.zeros_like(l_sc)
        acc_sc[...] = jnp.zeros_like(acc_sc)
        carry_sc[...] = jnp.zeros_like(carry_sc)

    def expand(cum):
        return jnp.broadcast_to(cum[:, None, :], (FOX_H, ts, page)).reshape(rows, page)

    def update(k_list, v_list, cum_list, valid):
        qb = qbd[...]
        s = [_dot(qb, kp.astype(BF16)) - expand(cm) * LOG2E for kp, cm in zip(k_list, cum_list)]
        s = s[0] if len(s) == 1 else jnp.concatenate(s, axis=1)
        if valid is not None:
            s = jnp.where(valid, s, NEG)
        m_prev = m_sc[...]
        m_new = jnp.maximum(m_prev, jnp.max(s, axis=-1, keepdims=True))
        alpha = jnp.exp2(m_prev - m_new)
        p = jnp.exp2(s - m_new)
        l_sc[...] = alpha * l_sc[...] + jnp.sum(p, axis=-1, keepdims=True)
        m_sc[...] = m_new
        pv = None
        for i, vp in enumerate(v_list):
            t_ = _dot_nt(p[:, i * page:(i + 1) * page].astype(BF16), vp.astype(BF16))
            pv = t_ if pv is None else pv + t_
        acc_sc[...] = alpha * acc_sc[...] + pv

    lf_all = jnp.concatenate([r[...] for r in lf_refs], axis=0) if pps > 1 else lf_refs[0][...]
    cin = _dot3_r(lf_all, u_ref[...])
    carry = carry_sc[...]
    cum = cin + jnp.concatenate([carry] * pps, axis=0) if pps > 1 else cin + carry
    if pps > 1:
        cum = cum + _dot3_l(pm_ref[...], cin)[:, page - 1:page]
    carry = cum[(pps - 1) * FOX_H:pps * FOX_H, page - 1:page]
    carry_sc[...] = carry
    update([r[...].reshape(width, page) for r in k_refs], [r[...].reshape(width, page) for r in v_refs],
           [cum[i * FOX_H:(i + 1) * FOX_H] for i in range(pps)], None)

    @pl.when(j == pl.num_programs(1) - 1)
    def _():
        cumn = _dot3_r(lfn_ref[...], u_ref[...]) + carry
        kn = _pad_rows(kn_ref[...], page).T
        vn = _pad_rows(vn_ref[...], page).T
        colk = lax.broadcasted_iota(jnp.int32, (rows, page), 1)
        trow = lax.broadcasted_iota(jnp.int32, (rows, page), 0) & (ts - 1)
        update([kn], [vn], [cumn], colk <= trow)
        accn = acc_sc[...] / l_sc[...]
        o_ref[...] = jnp.sum(jnp.where(own, accn, 0.0).reshape(FOX_H, ts, width), axis=0)


def _fox_sample(layer, proj, lfn_t, page_table, cache_k, cache_v, cache_lf_t):
    db, ts, _ = proj.shape
    npages = page_table.shape[1]
    page = cache_k.shape[-1]
    width = FOX_H * FOX_DH
    pps = 32 if npages % 32 == 0 else (16 if npages % 16 == 0 else 1)
    upper = jnp.triu(jnp.ones((page, page), F32)).astype(BF16)
    ridx = jnp.arange(pps * FOX_H)
    pmat = jnp.logical_and((ridx[:, None] % FOX_H) == (ridx[None, :] % FOX_H),
                           (ridx[None, :] // FOX_H) < (ridx[:, None] // FOX_H)).astype(BF16)
    rows = FOX_H * ts

    def page_spec(shape, p_):
        return pl.BlockSpec((None, None) + shape,
                            lambda bi, j, pt, p_=p_: (layer, pt[bi, j * pps + p_]) + (0,) * len(shape))

    in_specs = [
        pl.BlockSpec((None, ts, width), lambda bi, j, pt: (bi, 0, C_FQ // width)),
        pl.BlockSpec((None, ts, width), lambda bi, j, pt: (bi, 0, C_FK // width)),
        pl.BlockSpec((None, ts, width), lambda bi, j, pt: (bi, 0, C_FV // width)),
        pl.BlockSpec((None, FOX_H, page), lambda bi, j, pt: (bi, 0, 0)),
        pl.BlockSpec((page, page), lambda bi, j, pt: (0, 0)),
        pl.BlockSpec((pps * FOX_H, pps * FOX_H), lambda bi, j, pt: (0, 0)),
    ]
    in_specs += [page_spec((FOX_H, FOX_DH, page), p_) for p_ in range(pps)]
    in_specs += [page_spec((FOX_H, FOX_DH, page), p_) for p_ in range(pps)]
    in_specs += [page_spec((FOX_H, page), p_) for p_ in range(pps)]
    grid_spec = pltpu.PrefetchScalarGridSpec(
        num_scalar_prefetch=1,
        grid=(db, npages // pps),
        in_specs=in_specs,
        out_specs=pl.BlockSpec((None, ts, width), lambda bi, j, pt: (bi, 0, 0)),
        scratch_shapes=[
            pltpu.VMEM((rows, width), BF16),
            pltpu.VMEM((rows, 1), F32),
            pltpu.VMEM((rows, 1), F32),
            pltpu.VMEM((rows, width), F32),
            pltpu.VMEM((FOX_H, 1), F32),
        ],
    )
    args = [proj, proj, proj, lfn_t, upper, pmat] + [cache_k] * pps + [cache_v] * pps + [cache_lf_t] * pps
    return pl.pallas_call(
        functools.partial(_fox_dec_kernel, pps=pps, ts=ts),
        grid_spec=grid_spec,
        out_shape=jax.ShapeDtypeStruct((db, ts, width), F32),
        compiler_params=_cparams(("parallel", "arbitrary"), 56),
        name="fox_sample",
    )(page_table, *args)


def _rw_prep_kernel(p_ref, sh0_ref, mu_ref, w2_ref, w0_ref, g2_ref, kk_ref, ka_ref, rk_ref,
                    e_ref, bl_ref,
                    kt_o, bt_o, kq_o, rt_o, v_o, gc_o, ab_o, g_o, bon_o, carry, *, tr, tv):
    i = pl.program_id(1)
    trb = p_ref.shape[0]

    @pl.when(i == 0)
    def _():
        carry[...] = sh0_ref[...]

    p = _pad_rows(p_ref[...], tr)
    rowi = lax.broadcasted_iota(jnp.int32, (tr, 1), 0)
    prev = jnp.where(rowi == 0, carry[...], pltpu.roll(p, 1, axis=0))
    carry[...] = p_ref[pl.ds(trb - 1, 1), :]
    xm = p + (prev - p) * mu_ref[...]
    r = xm[:, 0:RW_W]
    k = xm[:, RW_W:2 * RW_W]
    v = xm[:, 2 * RW_W:3 * RW_W]
    z = xm[:, 3 * RW_W:3 * RW_W + RW_LORA]
    gl = xm[:, 3 * RW_W + RW_LORA:]
    lane = lax.broadcasted_iota(jnp.int32, (1, RW_LORA), 1)
    zz = jnp.where(lane < RW_LORA // 2, jnp.tanh(z), z)
    z_hi = zz.astype(BF16)
    z_lo = (zz - z_hi.astype(F32)).astype(BF16)
    pre = (_dot(z_hi, w2_ref[0]) + _dot(z_hi, w2_ref[1]) + _dot(z_lo, w2_ref[0])) + w0_ref[...]
    w_log = -_softplus(-pre[:, :RW_W]) - 0.5
    logdec = -jnp.exp(w_log)
    a = _sigmoid(pre[:, RW_W:])
    g = _dot(_sigmoid(gl).astype(BF16), g2_ref[...])
    e = e_ref[...]

    def head_sum(x):
        return jnp.concatenate(
            [_dot3_r(x[:, c0:c0 + LANE], e) for c0 in range(0, RW_W, LANE)], axis=1)

    kk = k * kk_ref[...]
    kk = kk / jnp.maximum(jnp.sqrt(head_sum(kk * kk)), 1e-12)
    k2 = k * (1.0 + (a - 1.0) * ka_ref[...])
    bonus = head_sum(r * k2 * rk_ref[...]) * v
    if tv < tr:
        valid = rowi < tv
        logdec = jnp.where(valid, logdec, 0.0)
        kk = jnp.where(valid, kk, 0.0)
        k2 = jnp.where(valid, k2, 0.0)
        v = jnp.where(valid, v, 0.0)
        r = jnp.where(valid, r, 0.0)
    cl = _dot3_l(bl_ref[...], logdec)
    ct = jnp.concatenate(
        [jnp.broadcast_to(cl[c0 + RW_CHUNK - 1:c0 + RW_CHUNK, :], (RW_CHUNK, RW_W))
         for c0 in range(0, tr, RW_CHUNK)], axis=0)
    e_inv = jnp.exp(-cl)
    kt = kk * jnp.exp(cl - logdec)
    bt = kk * a * e_inv
    kt_o[...] = kt
    bt_o[...] = bt
    kq_o[...] = k2 * e_inv
    rt_o[...] = r * jnp.exp(cl)
    v_o[...] = v
    gc_o[...] = jnp.exp(ct)
    g_o[...] = g
    bon_o[...] = bonus
    lane2 = lax.broadcasted_iota(jnp.int32, (1, LANE), 1)
    for c in range(tr // RW_CHUNK):
        rs = slice(c * RW_CHUNK, (c + 1) * RW_CHUNK)
        for hp in range(RW_H // 2):
            ls = slice(hp * LANE, (hp + 1) * LANE)
            x = kt[rs, ls]
            y = bt[rs, ls].astype(BF16)
            for h in range(2):
                mh = (lane2 < RW_N) if h == 0 else (lane2 >= RW_N)
                xh = jnp.where(mh, x, 0.0).astype(BF16)
                ab_o[2 * hp + h, rs, :] = _dot_nt(xh, y)


def _rw_prep(proj, shift0, wl, tv):
    b, t, _ = proj.shape
    tp = max(t, RW_CHUNK)
    tr = min(tp, 256)
    trb = min(t, tr)
    n = tp // tr
    cidx = jnp.arange(tr) // RW_CHUNK
    same = cidx[:, None] == cidx[None, :]
    blk_tri = jnp.logical_and(same, jnp.arange(tr)[None, :] <= jnp.arange(tr)[:, None]).astype(BF16)
    hidx = jnp.arange(LANE) // RW_N
    e128 = (hidx[:, None] == hidx[None, :]).astype(BF16)

    def const(shape):
        return pl.BlockSpec(shape, lambda bi, i: (0,) * len(shape))

    tok = pl.BlockSpec((None, tr, RW_W), lambda bi, i: (bi, i, 0))
    outs = [jax.ShapeDtypeStruct((b, tp, RW_W), F32)] * 6
    return pl.pallas_call(
        functools.partial(_rw_prep_kernel, tr=tr, tv=tv if t < tr else tr),
        grid=(b, n),
        in_specs=[
            pl.BlockSpec((None, trb, RW_PROJ), lambda bi, i: (bi, i, C_RW // RW_PROJ)),
            pl.BlockSpec((None, 1, RW_PROJ), lambda bi, i: (bi, 0, 0)),
            const((1, RW_PROJ)), const((2, RW_LORA, 2 * RW_W)), const((1, 2 * RW_W)),
            const((RW_GATE_R, RW_W)), const((1, RW_W)), const((1, RW_W)), const((1, RW_W)),
            const((LANE, LANE)), const((tr, tr)),
        ],
        out_specs=[tok] * 6 + [pl.BlockSpec((None, RW_H, tr, RW_N), lambda bi, i: (bi, 0, i, 0)), tok, tok],
        out_shape=outs + [jax.ShapeDtypeStruct((b, RW_H, tp, RW_N), F32)] + outs[:2],
        scratch_shapes=[pltpu.VMEM((1, RW_PROJ), F32)],
        compiler_params=_cparams(("parallel", "arbitrary"), 48),
        name="rwkv_prep",
    )(proj, shift0, wl["mu"], wl["w2a2"], wl["w0a0"], wl["g2"], wl["kk"], wl["ka"], wl["rk"],
      e128, blk_tri)


def _rw_solve_kernel(a_ref, t_ref, at_sc, tt_sc):
    c = RW_CHUNK
    per = LANE // c
    for jb in range(c * c // LANE):
        blk = a_ref[:, jb * LANE:(jb + 1) * LANE].T
        at_sc[jb * per:(jb + 1) * per] = blk.reshape(per, c, LANE)
    rows = lax.broadcasted_iota(jnp.int32, (c, LANE), 0)
    rb = 4

    gr = 16
    ng = c // gr

    def body_t(tb, carry_):
        t0 = tb * rb
        parts = tuple(tuple(jnp.where(rows[g * gr:(g + 1) * gr] == t0 + r, 1.0, 0.0) for g in range(ng))
                      for r in range(rb))
        for k in range(ng):
            live = (k + 1) * gr

            def body_u(u, prt, k=k, live=live):
                tu = tt_sc[u, 0:live, :]
                out = []
                for r in range(rb):
                    a = at_sc[t0 + r, pl.ds(u, 1), :]
                    out.append(tuple(prt[r][g] - a * tu[g * gr:(g + 1) * gr] if g <= k else prt[r][g]
                                     for g in range(ng)))
                return tuple(out)

            parts = lax.fori_loop(k * gr, jnp.clip(t0, k * gr, live), body_u, parts)
        accs = [jnp.concatenate(parts[r], axis=0) for r in range(rb)]
        done = []
        for r in range(rb):
            acc = accs[r]
            for r2 in range(r):
                acc = acc - at_sc[t0 + r, pl.ds(t0 + r2, 1), :] * done[r2]
            tt_sc[t0 + r] = acc
            done.append(acc)
        return carry_

    lax.fori_loop(0, c // rb, body_t, 0)
    for jb in range(c * c // LANE):
        blk = tt_sc[jb * per:(jb + 1) * per].reshape(LANE, LANE)
        t_ref[:, jb * LANE:(jb + 1) * LANE] = blk.T


def _rw_solve(a2d):
    nch, cc = a2d.shape
    return pl.pallas_call(
        _rw_solve_kernel,
        grid=(nch // LANE,),
        in_specs=[pl.BlockSpec((LANE, cc), lambda i: (i, 0))],
        out_specs=pl.BlockSpec((LANE, cc), lambda i: (i, 0)),
        out_shape=jax.ShapeDtypeStruct((nch, cc), F32),
        scratch_shapes=[pltpu.VMEM((RW_CHUNK, RW_CHUNK, LANE), F32)] * 2,
        compiler_params=_cparams(("parallel",), 40),
        name="rwkv_solve",
    )(a2d)


def _rw_chain_kernel(kt_ref, bt_ref, kq_ref, rt_ref, v_ref, gc_ref, tm_ref, g_ref, bon_ref,
                     lnw_ref, lnb_ref, e_ref, h0_ref, o_ref, hout_ref,
                     h_sc, gm_sc, nm_sc, qe_sc, y_sc, *, tt, to):
    i = pl.program_id(1)
    cn = RW_CHUNK
    nck = tt // cn
    npair = RW_H // 2

    @pl.when(i == 0)
    def _():
        h_sc[...] = h0_ref[...]

    lane = lax.broadcasted_iota(jnp.int32, (1, LANE), 1)
    lo = lane < RW_N
    ri = lax.broadcasted_iota(jnp.int32, (cn, LANE), 0)
    ci = lax.broadcasted_iota(jnp.int32, (cn, LANE), 1) & (RW_N - 1)
    strict = ci < ri
    incl = ci <= ri
    rd = lax.broadcasted_iota(jnp.int32, (LANE, LANE), 0)
    cd = lax.broadcasted_iota(jnp.int32, (LANE, LANE), 1)
    eye = rd == cd
    same_head = (rd >> 6) == (cd >> 6)

    def stack(x):
        return jnp.concatenate([jnp.where(lo, x, 0.0), jnp.where(lo, 0.0, x)], axis=0).astype(BF16)

    units = [(c, p) for c in range(nck) for p in range(npair)]
    cs = range(len(units))
    rss = [slice(c * cn, (c + 1) * cn) for c, _ in units]
    lss = [slice(p * LANE, (p + 1) * LANE) for _, p in units]
    kt = [kt_ref[rs, ls] for rs, ls in zip(rss, lss)]
    bt = [bt_ref[rs, ls] for rs, ls in zip(rss, lss)]
    kq = [kq_ref[rs, ls] for rs, ls in zip(rss, lss)]
    rt = [rt_ref[rs, ls] for rs, ls in zip(rss, lss)]
    v = [v_ref[rs, ls] for rs, ls in zip(rss, lss)]
    gc = [gc_ref[rs, ls] for rs, ls in zip(rss, lss)]
    ks = [stack(kq[c]) for c in cs]
    vs = [stack(v[c]) for c in cs]
    tcat = [jnp.concatenate([tm_ref[2 * p, rs, :], tm_ref[2 * p + 1, rs, :]], axis=1).astype(BF16)
            for (_, p), rs in zip(units, rss)]
    ak = [jnp.where(strict, _dot_nt(kt[c].astype(BF16), ks[c]), 0.0).astype(BF16) for c in cs]
    kd = [_dot(tcat[c], stack(kt[c])) for c in cs]
    bb = [jnp.where(incl, _dot_nt(rt[c].astype(BF16), stack(bt[c])), 0.0).astype(BF16) for c in cs]
    bk = [jnp.where(incl, _dot_nt(rt[c].astype(BF16), ks[c]), 0.0).astype(BF16) for c in cs]
    akv = [_dot(ak[c], vs[c]) for c in cs]
    vd = [_dot(tcat[c], stack(akv[c])) for c in cs]
    bh = [(bt[c] * gc[c]).astype(BF16) for c in cs]
    for c in cs:
        gm_sc[c] = (jnp.where(eye, gc[c][0:1, :], 0.0)
                    - jnp.where(same_head, _dot_tn(bh[c], kd[c].astype(BF16)), 0.0))
        qe_sc[rss[c], lss[c]] = rt[c] - _dot(bb[c], stack(kd[c]))
    for c in cs:
        kh = (kq[c] * gc[c]).astype(BF16)
        nm_sc[c] = jnp.where(same_head, _dot_tn(
            jnp.concatenate([kh, bh[c]], axis=0),
            jnp.concatenate([v[c].astype(BF16), (-vd[c]).astype(BF16)], axis=0)), 0.0)
        y_sc[rss[c], lss[c]] = _dot(jnp.concatenate([bk[c], bb[c]], axis=1),
                                    jnp.concatenate([vs[c], stack(-vd[c])], axis=0))

    hcur = [h_sc[p] for p in range(npair)]
    for u, (c, p) in enumerate(units):
        hb = hcur[p].astype(BF16)
        y_sc[rss[u], lss[u]] = y_sc[rss[u], lss[u]] + _dot(qe_sc[rss[u], lss[u]].astype(BF16), hb)
        hcur[p] = _dot(gm_sc[u].astype(BF16), hb) + nm_sc[u]
    for p in range(npair):
        h_sc[p] = hcur[p]

    e = e_ref[...]
    for p in range(npair):
        ls = slice(p * LANE, (p + 1) * LANE)
        y = y_sc[:, ls]
        mean = _dot3_r(y, e) * (1.0 / RW_N)
        yc = y - mean
        var = _dot3_r(yc * yc, e) * (1.0 / RW_N)
        out = ((yc * lax.rsqrt(var + RW_GN_EPS) * lnw_ref[:, ls] + lnb_ref[:, ls] + bon_ref[:, ls])
               * g_ref[:, ls])
        o_ref[:, ls] = out[:to]

    @pl.when(i == pl.num_programs(1) - 1)
    def _():
        for p in range(npair):
            hout_ref[p] = hcur[p]


def _rw_chain(prep, tmat, h0, lnw, lnb, t_out):
    kt, bt, kq, rt, v, gc, _, g, bon = prep
    b, tp, _ = kt.shape
    tt = min(tp, 256)
    to = min(t_out, tt)
    hp = RW_H // 2
    hidx = jnp.arange(LANE) // RW_N
    e128 = (hidx[:, None] == hidx[None, :]).astype(BF16)
    tok = pl.BlockSpec((None, tt, RW_W), lambda bi, i: (bi, i, 0))
    vec = pl.BlockSpec((1, RW_W), lambda bi, i: (0, 0))
    st = pl.BlockSpec((None, hp, LANE, LANE), lambda bi, i: (bi, 0, 0, 0))
    nunit = tt // RW_CHUNK * hp
    return pl.pallas_call(
        functools.partial(_rw_chain_kernel, tt=tt, to=to),
        grid=(b, tp // tt),
        in_specs=[tok] * 6 + [
            pl.BlockSpec((None, RW_H, tt, RW_N), lambda bi, i: (bi, 0, i, 0)),
            tok, tok, vec, vec,
            pl.BlockSpec((LANE, LANE), lambda bi, i: (0, 0)),
            st,
        ],
        out_specs=[pl.BlockSpec((None, to, RW_W), lambda bi, i: (bi, i, 0)), st],
        out_shape=[
            jax.ShapeDtypeStruct((b, t_out, RW_W), F32),
            jax.ShapeDtypeStruct((b, hp, LANE, LANE), F32),
        ],
        scratch_shapes=[
            pltpu.VMEM((hp, LANE, LANE), F32),
            pltpu.VMEM((nunit, LANE, LANE), F32),
            pltpu.VMEM((nunit, LANE, LANE), F32),
            pltpu.VMEM((tt, RW_W), F32),
            pltpu.VMEM((tt, RW_W), F32),
        ],
        compiler_params=_cparams(("parallel", "arbitrary"), 40),
        name="rwkv_chain",
    )(kt, bt, kq, rt, v, gc, tmat, g, bon, lnw, lnb, e128, h0)


def _rwkv(proj, shift0, s0, wl, t):
    b = proj.shape[0]
    prep = _rw_prep(proj, shift0, wl, t)
    ab = prep[6]
    tp = ab.shape[2]
    cc = RW_CHUNK * RW_CHUNK
    a2d = ab.reshape(-1, cc)
    nch = a2d.shape[0]
    nch_p = -(-nch // LANE) * LANE
    if nch_p != nch:
        a2d = jnp.pad(a2d, ((0, nch_p - nch), (0, 0)))
    tmat = _rw_solve(a2d)[:nch].reshape(b, RW_H, tp, RW_N)
    st = jnp.swapaxes(s0, -1, -2).reshape(b, RW_H // 2, 2, RW_N, RW_N)
    z = jnp.zeros_like(st[:, :, 0])
    h0 = jnp.concatenate([jnp.concatenate([st[:, :, 0], z], -1), jnp.concatenate([z, st[:, :, 1]], -1)], -2)
    o, hout = _rw_chain(prep, tmat, h0, wl["lnw"], wl["lnb"], t)
    s_new = jnp.stack([hout[:, :, :RW_N, :RW_N], hout[:, :, RW_N:, RW_N:]], axis=2)
    s_new = jnp.swapaxes(s_new.reshape(b, RW_H, RW_N, RW_N), -1, -2)
    return o, s_new


def _merge_kernel(x_ref, gm_ref, oa_ref, ob_ref, oc_ref, ga_ref, gb_ref, gc_ref,
                  wa_ref, wb_ref, wc_ref, wo_ref, o_ref):
    merged = (_sigmoid(ga_ref[...]) * _dot(oa_ref[...].astype(BF16), wa_ref[...])
              + _sigmoid(gb_ref[...]) * _dot(ob_ref[...].astype(BF16), wb_ref[...])
              + _sigmoid(gc_ref[...]) * _dot(oc_ref[...].astype(BF16), wc_ref[...]))
    mix = _dot(merged.astype(BF16), wo_ref[...])
    o_ref[...] = x_ref[...] + gm_ref[...] * mix


def _merge(x, g1, o_ret, o_fox, o_rw, proj, wl):
    g_, r, d = x.shape
    tm = min(r, 512)
    rm = g1.shape[1]
    w = o_ret.shape[-1]

    def tok(width, cb=0):
        return pl.BlockSpec((None, tm, width), lambda g, i, cb=cb: (g, i, cb))

    def const(shape):
        return pl.BlockSpec(shape, lambda g, i: (0, 0))

    return pl.pallas_call(
        _merge_kernel,
        grid=(g_, r // tm),
        in_specs=[
            tok(d), _row_spec(rm, tm, d), tok(w), tok(w), tok(w),
            tok(d, C_GATE // d), tok(d, C_GATE // d + 1), tok(d, C_GATE // d + 2),
            const((w, d)), const((w, d)), const((w, d)), const((d, d)),
        ],
        out_specs=tok(d),
        out_shape=jax.ShapeDtypeStruct((g_, r, d), F32),
        compiler_params=_cparams(("parallel", "parallel"), 48),
        name="merge_out",
    )(x, g1, o_ret, o_fox, o_rw, proj, proj, proj, wl["w_br_ret"], wl["w_br_fox"], wl["w_br_rw"], wl["w_out"])


def _ffn_kernel(x_ref, sc_ref, sh_ref, gm_ref, g_ref, wu_ref, wd_ref, nf_ref, o_ref, h_sc, acc_sc, *, final):
    f = pl.program_id(2)

    @pl.when(f == 0)
    def _():
        h_sc[...] = _norm_mod(x_ref[...], sc_ref[...], sh_ref[...], g_ref[...]).astype(BF16)
        acc_sc[...] = jnp.zeros_like(acc_sc)

    u = jnp.maximum(_dot(h_sc[...], wu_ref[...]), 0.0)
    acc_sc[...] += _dot((u * u).astype(BF16), wd_ref[...])

    @pl.when(f == pl.num_programs(2) - 1)
    def _():
        xn = x_ref[...] + gm_ref[...] * acc_sc[...]
        if final:
            ms = jnp.mean(xn * xn, axis=-1, keepdims=True)
            xn = xn * lax.rsqrt(ms + NORM_EPS) * nf_ref[...]
        o_ref[...] = xn


def _ffn(x, sc, sh, g2, gamma, w_up, w_down, norm_final, final):
    g_, r, d = x.shape
    dff = w_up.shape[1]
    tm = min(r, 1024)
    tf = 1024
    rm = sc.shape[1]
    return pl.pallas_call(
        functools.partial(_ffn_kernel, final=final),
        grid=(g_, r // tm, dff // tf),
        in_specs=[
            pl.BlockSpec((None, tm, d), lambda g, i, f: (g, i, 0)),
            _row_spec(rm, tm, d), _row_spec(rm, tm, d), _row_spec(rm, tm, d),
            pl.BlockSpec((1, d), lambda g, i, f: (0, 0)),
            pl.BlockSpec((d, tf), lambda g, i, f: (0, f)),
            pl.BlockSpec((tf, d), lambda g, i, f: (f, 0)),
            pl.BlockSpec((1, d), lambda g, i, f: (0, 0)),
        ],
        out_specs=pl.BlockSpec((None, tm, d), lambda g, i, f: (g, i, 0)),
        out_shape=jax.ShapeDtypeStruct((g_, r, d), F32),
        scratch_shapes=[pltpu.VMEM((tm, d), BF16), pltpu.VMEM((tm, d), F32)],
        compiler_params=_cparams(("parallel", "parallel", "arbitrary"), 48),
        name="ffn",
    )(x, sc, sh, g2, gamma, w_up, w_down, norm_final)


def _rope_tables(q0, t):
    d = RET_D
    inv = 1.0 / (ROPE_BASE ** (jnp.arange(0, d, 2, dtype=F32) / d))
    ang = (q0 + jnp.arange(t)).astype(F32)[:, None] * inv[None, :]
    cos, sin = jnp.cos(ang), jnp.sin(ang)
    return jnp.concatenate([cos, cos], -1), jnp.concatenate([-sin, sin], -1)


def _prep_layer_weights(l, w):
    w_in = w["w_in"][l]
    d = w_in.shape[0]
    ret_cols, fox_qkv = 2048, 1536
    f0 = ret_cols + fox_qkv
    rw0 = f0 + FOX_H
    g0 = rw0 + RW_PROJ
    w_t = jnp.swapaxes(w_in, 0, 1)
    packed = jnp.concatenate([
        w_t[rw0:g0], w_t[f0:rw0], jnp.zeros((2 * LANE - FOX_H, d), F32),
        w_t[g0:], w_t[ret_cols:f0], w_t[:ret_cols]], axis=0).astype(BF16)
    zl = jnp.zeros((RW_LORA // 2, RW_W), F32)
    w2a2 = jnp.concatenate([jnp.concatenate([w["rw_w2"][l], zl], 1),
                            jnp.concatenate([zl, w["rw_a2"][l]], 1)], 0)
    w2a2_hi = w2a2.astype(BF16)
    w2a2 = jnp.stack([w2a2_hi, (w2a2 - w2a2_hi.astype(F32)).astype(BF16)])
    row = lambda a: a.reshape(1, -1)
    return {
        "w_in": packed,
        "norm_mix": row(w["norm_mix"][l]), "norm_ffn": row(w["norm_ffn"][l]),
        "bf": jnp.pad(w["fox_bf"][l], (0, LANE - FOX_H)).reshape(1, LANE),
        "mu": row(w["rw_mu"][l]), "w2a2": w2a2,
        "w0a0": jnp.concatenate([w["rw_w0"][l], w["rw_a0"][l]]).reshape(1, -1),
        "g2": w["rw_g2"][l].astype(BF16),
        "kk": row(w["rw_kk"][l]), "ka": row(w["rw_ka"][l]), "rk": row(w["rw_rk"][l]),
        "lnw": row(w["rw_lnw"][l]), "lnb": row(w["rw_lnb"][l]),
        "w_br_ret": w["w_br_ret"][l].astype(BF16), "w_br_fox": w["w_br_fox"][l].astype(BF16),
        "w_br_rw": w["w_br_rw"][l].astype(BF16), "w_out": w["w_out"][l].astype(BF16),
        "w_up": w["w_up"][l].astype(BF16), "w_down": w["w_down"][l].astype(BF16),
    }


def _ret_log_gamma_table():
    lg = jnp.log1p(-jnp.exp2(-5.0 - jnp.arange(RET_H, dtype=F32)))
    return jnp.broadcast_to(lg[:, None, None], (RET_H, 1, RET_D))


def _layer(x, mod, q0, past, ret0, wkv0, shift0, wl, norm_final, final, flat):
    b, t, d = x.shape
    sh1, sc1, g1, sh2, sc2, g2 = [m[:, None, :] for m in jnp.split(mod, 6, axis=-1)]
    if flat:
        rep = lambda m: jnp.repeat(m, t, axis=1).reshape(1, b * t, d)
        sh1, sc1, g1, sh2, sc2, g2 = [rep(m) for m in (sh1, sc1, g1, sh2, sc2, g2)]
        xd = x.reshape(1, b * t, d)
    else:
        xd = x
    if past is None:
        proj_d, k_t, v_t = _norm_mm(xd, sc1, sh1, wl["norm_mix"], wl["w_in"], (C_FK, C_FV))
    else:
        proj_d, = _norm_mm(xd, sc1, sh1, wl["norm_mix"], wl["w_in"])
    proj = proj_d.reshape(b, t, PROJ_W)

    cosf, sinf = _rope_tables(q0, t)
    o_ret, r_new = _retention(proj, cosf, sinf, _ret_log_gamma_table(), ret0)

    lf, cum = _fox_gate(proj_d, wl["bf"])
    lf = lf.reshape(b, t, LANE)[:, :, :FOX_H]
    if past is None:
        o_fox = _fox_prompt(proj, v_t, cum)
    else:
        layer, page_table, cache_k, cache_v, cache_lf_t = past
        page = cache_k.shape[-1]
        lfn_t =jnp.pad(jnp.swapaxes(lf, 1, 2), ((0, 0), (0, 0), (0, page - t)))
        o_fox = _fox_sample(layer, proj, lfn_t, page_table, cache_k, cache_v, cache_lf_t)

    o_rw, s_new = _rwkv(proj, shift0[:, None, :], wkv0, wl, t)

    dn = lambda a: a.reshape(xd.shape[0], xd.shape[1], -1)
    x1 = _merge(xd, g1, dn(o_ret), dn(o_fox), dn(o_rw), proj_d, wl)
    x2 = _ffn(x1, sc2, sh2, g2, wl["norm_ffn"], wl["w_up"], wl["w_down"], norm_final, final)
    if past is None:
        fk = k_t.reshape(b, FOX_H, FOX_DH, t).transpose(0, 3, 1, 2)
        fv = v_t.reshape(b, FOX_H, FOX_DH, t).transpose(0, 3, 1, 2)
    else:
        fk = proj[:, :, C_FK:C_FV].reshape(b, t, FOX_H, FOX_DH)
        fv = proj[:, :, C_FV:C_RQ].reshape(b, t, FOX_H, FOX_DH)
    shift = proj[:, t - 1, C_RW:C_RW + RW_PROJ]
    return x2.reshape(b, t, d), (fk, fv, lf, r_new, s_new, shift)


def kernel(x_prompt, x_sample, c_prompt, c_sample, cache_k, cache_v, cache_logf, page_table, state_ret, state_wkv, state_shift, ada_w, ada_b, norm_mix, norm_ffn, w_in, fox_bf, rw_mu, rw_w0, rw_w2, rw_a0, rw_a2, rw_g2, rw_kk, rw_ka, rw_rk, rw_lnw, rw_lnb, w_br_ret, w_br_fox, w_br_rw, w_out, w_up, w_down, norm_final):
    depth = w_in.shape[0]
    b, t, d = x_prompt.shape
    db, ts, _ = x_sample.shape
    n_pool, page = cache_k.shape[1], cache_k.shape[2]
    past_len = page_table.shape[1] * page
    w = dict(w_in=w_in, norm_mix=norm_mix, norm_ffn=norm_ffn, fox_bf=fox_bf, rw_mu=rw_mu, rw_w0=rw_w0,
             rw_w2=rw_w2, rw_a0=rw_a0, rw_a2=rw_a2, rw_g2=rw_g2, rw_kk=rw_kk, rw_ka=rw_ka, rw_rk=rw_rk,
             rw_lnw=rw_lnw, rw_lnb=rw_lnb, w_br_ret=w_br_ret, w_br_fox=w_br_fox, w_br_rw=w_br_rw,
             w_out=w_out, w_up=w_up, w_down=w_down)
    rows = b + db
    rows_p = -(-rows // 8) * 8
    c_all = jnp.pad(jnp.concatenate([c_prompt, c_sample], 0), ((0, rows_p - rows), (0, 0)))
    mod = _modulation(c_all, ada_w, ada_b)
    ck = jnp.transpose(cache_k, (0, 1, 3, 4, 2))
    cv = jnp.transpose(cache_v, (0, 1, 3, 4, 2))
    clf_t = jnp.swapaxes(cache_logf, 2, 3)
    nf = norm_final.reshape(1, d)
    xp, xs = x_prompt, x_sample
    st_p, st_s = [], []
    for l in range(depth):
        wl = _prep_layer_weights(l, w)
        final = l == depth - 1
        xp, sp = _layer(xp, mod[l, :b], 0, None, jnp.zeros((b, RET_H, RET_D, RET_D), F32),
                        jnp.zeros((b, RW_H, RW_N, RW_N), F32), jnp.zeros((b, RW_PROJ), F32),
                        wl, nf, final, False)
        st_p.append(sp)
        xs, ss = _layer(xs, mod[l, b:rows], past_len, (l, page_table, ck, cv, clf_t), state_ret[l],
                        state_wkv[l], state_shift[l], wl, nf, final, True)
        st_s.append(ss)
    outs = [xp, xs]
    for st in (st_p, st_s):
        for i in range(6):
            outs.append(jnp.stack([s[i] for s in st]))
    y_p, y_s, k_p, v_p, lf_p, ret_p, wkv_p, sh_p, k_s, v_s, lf_s, ret_s, wkv_s, sh_s = outs
    return (y_p, y_s, k_p, v_p, lf_p, ret_p, wkv_p, sh_p, k_s, v_s, lf_s, ret_s, wkv_s, sh_s)
```

```python
import functools

import jax
import jax.numpy as jnp
from jax import lax
from jax.experimental import pallas as pl
from jax.experimental.pallas import tpu as pltpu

F32 = jnp.float32
BF16 = jnp.bfloat16
HIGHEST = lax.Precision.HIGHEST

RET_H, RET_D = 4, 128
FOX_H, FOX_DH = 8, 64
RW_H, RW_N = 8, 64
RW_W = RW_H * RW_N
RW_LORA = 128
RW_GATE_R = 128
RW_PROJ = 3 * RW_W + RW_LORA + RW_GATE_R
RET_GN_EPS = 1e-6
RW_GN_EPS = 64e-5
NORM_EPS = 1e-6
ROPE_BASE = 10000.0
NEG = -1e30
LOG2E = 1.4426950408889634

LANE = 128
C_RW = 0
C_F = 1792
C_GATE = 2048
C_FQ, C_FK, C_FV = 5120, 5632, 6144
C_RQ, C_RK, C_RV, C_RG = 6656, 7168, 7680, 8192
PROJ_W = 8704

RW_CHUNK = 64
RET_CHUNK = 128


def _cparams(sem, vmem_mb):
    return pltpu.CompilerParams(dimension_semantics=sem, vmem_limit_bytes=vmem_mb << 20)


def _dot(a, b):
    return jnp.dot(a, b, preferred_element_type=F32)


def _dot_nt(a, b):
    return lax.dot_general(a, b, (((1,), (1,)), ((), ())), preferred_element_type=F32)


def _dot_tn(a, b):
    return lax.dot_general(a, b, (((0,), (0,)), ((), ())), preferred_element_type=F32)


def _split3(x):
    hi = x.astype(BF16)
    r1 = x - hi.astype(F32)
    mid = r1.astype(BF16)
    lo = (r1 - mid.astype(F32)).astype(BF16)
    return hi, mid, lo


def _dot3_l(m01, x):
    hi, mid, lo = _split3(x)
    return _dot(m01, hi) + _dot(m01, mid) + _dot(m01, lo)


def _dot3_r(x, m01):
    hi, mid, lo = _split3(x)
    return _dot(hi, m01) + _dot(mid, m01) + _dot(lo, m01)


def _sigmoid(x):
    return 1.0 / (1.0 + jnp.exp(-x))


def _softplus(x):
    return jnp.maximum(x, 0.0) + jnp.log(1.0 + jnp.exp(-jnp.abs(x)))


def _pad_rows(a, rows):
    if a.shape[0] == rows:
        return a
    return jnp.concatenate([a, jnp.zeros((rows - a.shape[0],) + a.shape[1:], a.dtype)], axis=0)


def _mod_kernel(c_ref, w_ref, b_ref, o_ref):
    c = c_ref[...]
    s = c * _sigmoid(c)
    o_ref[...] = jnp.dot(s, w_ref[...], precision=HIGHEST, preferred_element_type=F32) + b_ref[...]


def _modulation(c_all, ada_w, ada_b):
    depth, d, n = ada_w.shape
    rows = c_all.shape[0]
    tn = 1536
    return pl.pallas_call(
        _mod_kernel,
        grid=(depth, n // tn),
        in_specs=[
            pl.BlockSpec((rows, d), lambda l, j: (0, 0)),
            pl.BlockSpec((None, d, tn), lambda l, j: (l, 0, j)),
            pl.BlockSpec((None, 1, tn), lambda l, j: (l, 0, j)),
        ],
        out_specs=pl.BlockSpec((None, rows, tn), lambda l, j: (l, 0, j)),
        out_shape=jax.ShapeDtypeStruct((depth, rows, n), F32),
        compiler_params=_cparams(("parallel", "parallel"), 40),
        name="ada_mod",
    )(c_all, ada_w, ada_b.reshape(depth, 1, n))


def _norm_mod(x, sc, sh, g):
    ms = jnp.mean(x * x, axis=-1, keepdims=True)
    y = x * lax.rsqrt(ms + NORM_EPS) * g
    return y * (1.0 + sc) + sh


def _norm_mm_kernel(x_ref, sc_ref, sh_ref, g_ref, w_ref, o_ref, *rest, t_blocks):
    h_ref = rest[-1]
    j = pl.program_id(2)

    @pl.when(j == 0)
    def _():
        h_ref[...] = _norm_mod(x_ref[...], sc_ref[...], sh_ref[...], g_ref[...]).astype(BF16)

    y = _dot_nt(h_ref[...], w_ref[...])
    o_ref[...] = y
    for n_, jb in enumerate(t_blocks):
        @pl.when(j == jb)
        def _(n_=n_):
            rest[n_][...] = y.T


def _row_spec(rm, tm, d):
    if rm == 1:
        return pl.BlockSpec((None, 1, d), lambda g, i, *_: (g, 0, 0))
    return pl.BlockSpec((None, tm, d), lambda g, i, *_: (g, i, 0))


def _norm_mm(x, sc, sh, gamma, w_t, t_cols=()):
    g_, r, d = x.shape
    n = w_t.shape[0]
    tm = min(r, 2048)
    tn = 512
    rm = sc.shape[1]
    out_specs = [pl.BlockSpec((None, tm, tn), lambda g, i, j: (g, i, j))]
    out_shape = [jax.ShapeDtypeStruct((g_, r, n), F32)]
    t_blocks = tuple(c0 // tn for c0 in t_cols)
    for _ in t_blocks:
        out_specs.append(pl.BlockSpec((None, tn, tm), lambda g, i, j: (g, 0, i)))
        out_shape.append(jax.ShapeDtypeStruct((g_, tn, r), F32))
    return pl.pallas_call(
        functools.partial(_norm_mm_kernel, t_blocks=t_blocks),
        grid=(g_, r // tm, n // tn),
        in_specs=[
            pl.BlockSpec((None, tm, d), lambda g, i, j: (g, i, 0)),
            _row_spec(rm, tm, d),
            _row_spec(rm, tm, d),
            pl.BlockSpec((1, d), lambda g, i, j: (0, 0)),
            pl.BlockSpec((tn, d), lambda g, i, j: (j, 0)),
        ],
        out_specs=out_specs,
        out_shape=out_shape,
        scratch_shapes=[pltpu.VMEM((tm, d), BF16)],
        compiler_params=_cparams(("parallel", "parallel", "arbitrary"), 56),
        name="norm_in_proj",
    )(x, sc, sh, gamma, w_t)


def _fox_gate_kernel(f_ref, b_ref, tri_ref, lf_ref, cum_ref, carry_ref):
    @pl.when(pl.program_id(1) == 0)
    def _():
        carry_ref[...] = jnp.zeros_like(carry_ref)

    z = f_ref[...] + b_ref[...]
    lf = -_softplus(-z)
    lf_ref[...] = lf
    cum = _dot3_l(tri_ref[...], lf) + carry_ref[...]
    tb = cum.shape[0]
    carry_ref[...] = cum[tb - 1:tb, :]
    cum2 = cum * LOG2E
    for h in range(FOX_H):
        cum_ref[h] = cum2[:, h:h + 1]


def _fox_gate(proj, bf_pad):
    g_, r, _ = proj.shape
    tb = min(r, 512)
    tri = jnp.tril(jnp.ones((tb, tb), F32)).astype(BF16)
    return pl.pallas_call(
        _fox_gate_kernel,
        grid=(g_, r // tb),
        in_specs=[
            pl.BlockSpec((None, tb, LANE), lambda g, i: (g, i, C_F // LANE)),
            pl.BlockSpec((1, LANE), lambda g, i: (0, 0)),
            pl.BlockSpec((tb, tb), lambda g, i: (0, 0)),
        ],
        out_specs=[
            pl.BlockSpec((None, tb, LANE), lambda g, i: (g, i, 0)),
            pl.BlockSpec((None, FOX_H, tb, 1), lambda g, i: (g, 0, i, 0)),
        ],
        out_shape=[jax.ShapeDtypeStruct((g_, r, LANE), F32),
                   jax.ShapeDtypeStruct((g_, FOX_H, r, 1), F32)],
        scratch_shapes=[pltpu.VMEM((1, LANE), F32)],
        compiler_params=_cparams(("parallel", "arbitrary"), 32),
        name="fox_gate",
    )(proj, bf_pad, tri)


def _ret_kernel(q_ref, k_ref, v_ref, g_ref, cos_ref, sin_ref, lg_ref, r0_ref,
                o_ref, rout_ref, r_sc, *, ce, nb):
    cp = RET_CHUNK
    c = pl.program_id(1)
    heads = range(RET_H)

    @pl.when(c == 0)
    def _():
        r_sc[...] = r0_ref[...]

    rows = max(nb * ce, cp)
    cosf = _pad_rows(cos_ref[...], rows)
    sinf = _pad_rows(sin_ref[...], rows)

    def rot(x):
        return x * cosf + pltpu.roll(x, RET_D // 2, axis=1) * sinf

    row = lax.broadcasted_iota(jnp.int32, (cp, RET_D), 0).astype(F32)
    ri = lax.broadcasted_iota(jnp.int32, (cp, cp), 0)
    ci = lax.broadcasted_iota(jnp.int32, (cp, cp), 1)
    diff = (ri - ci).astype(F32)
    lss = [slice(h * RET_D, (h + 1) * RET_D) for h in heads]
    q = [rot(_pad_rows(q_ref[:, ls], rows)).astype(BF16) for ls in lss]
    kf = [rot(_pad_rows(k_ref[:, ls], rows)) * (RET_D ** -0.5) for ls in lss]
    vb = [_pad_rows(v_ref[:, ls], rows).astype(BF16) for ls in lss]
    lgs = [lg_ref[h] for h in heads]
    qdec = [jnp.exp(lg * (row + 1.0)) for lg in lgs]
    kdec = [jnp.exp(lg * (ce - 1.0 - row)) for lg in lgs]
    dmask = [jnp.where(diff >= 0.0, jnp.exp(lg[:, :1] * jnp.maximum(diff, 0.0)), 0.0) for lg in lgs]
    gdec = [jnp.exp(lg[:, :1] * float(ce)) for lg in lgs]

    r_cur = [r_sc[h] for h in heads]
    outs = [[] for _ in heads]
    for j in range(nb):
        rs = slice(j * cp, (j + 1) * cp)
        s = [_dot_nt(q[h][rs], kf[h][rs].astype(BF16)) * dmask[h] for h in heads]
        cross = [_dot(q[h][rs], r_cur[h].astype(BF16)) * qdec[h] for h in heads]
        upd = [_dot_tn((kf[h][rs] * kdec[h]).astype(BF16), vb[h][rs]) for h in heads]
        inner = [_dot(s[h].astype(BF16), vb[h][rs]) for h in heads]
        for h in heads:
            r_cur[h] = gdec[h] * r_cur[h] + upd[h]
            outs[h].append(inner[h] + cross[h])
    for h in heads:
        r_sc[h] = r_cur[h]
        o = outs[h][0] if nb == 1 else jnp.concatenate(outs[h], axis=0)
        mu = jnp.mean(o, axis=-1, keepdims=True)
        oc = o - mu
        var = jnp.mean(oc * oc, axis=-1, keepdims=True)
        gg = _pad_rows(g_ref[:, lss[h]], rows)
        out = oc * lax.rsqrt(var + RET_GN_EPS) * (gg * _sigmoid(gg))
        o_ref[:, lss[h]] = out[:nb * ce]

    @pl.when(c == pl.num_programs(1) - 1)
    def _():
        for h in heads:
            rout_ref[h] = r_cur[h]


def _retention(proj, cosf, sinf, lg_tab, r0):
    b, t, _ = proj.shape
    ce = min(RET_CHUNK, t)
    n = t // ce
    nb = 4 if (ce == RET_CHUNK and n % 4 == 0) else 1
    tr = nb * ce

    w = RET_H * RET_D

    def col(c0):
        return pl.BlockSpec((None, tr, w), lambda bi, c, c0=c0: (bi, c, c0 // w))

    state = pl.BlockSpec((None, RET_H, RET_D, RET_D), lambda bi, c: (bi, 0, 0, 0))
    return pl.pallas_call(
        functools.partial(_ret_kernel, ce=ce, nb=nb),
        grid=(b, n // nb),
        in_specs=[
            col(C_RQ), col(C_RK), col(C_RV), col(C_RG),
            pl.BlockSpec((tr, RET_D), lambda bi, c: (c, 0)),
            pl.BlockSpec((tr, RET_D), lambda bi, c: (c, 0)),
            pl.BlockSpec((RET_H, 1, RET_D), lambda bi, c: (0, 0, 0)),
            state,
        ],
        out_specs=[pl.BlockSpec((None, tr, w), lambda bi, c: (bi, c, 0)), state],
        out_shape=[
            jax.ShapeDtypeStruct((b, t, w), F32),
            jax.ShapeDtypeStruct((b, RET_H, RET_D, RET_D), F32),
        ],
        scratch_shapes=[pltpu.VMEM((RET_H, RET_D, RET_D), F32)],
        compiler_params=_cparams(("parallel", "arbitrary"), 40),
        name="retention",
    )(proj, proj, proj, proj, cosf, sinf, lg_tab, r0)


def _fox_kernel(qi_ref, ki_ref, q_ref, k_ref, v_ref, ck_ref, o_ref, qt_sc, m_sc, l_sc, acc_sc, *, tq, tk, nh):
    qi = qi_ref[pl.program_id(2)]
    ki = ki_ref[pl.program_id(2)]
    head0 = lax.broadcasted_iota(jnp.int32, (LANE, 1), 0) < FOX_DH

    @pl.when(ki == 0)
    def _():
        q = q_ref[...] * (FOX_DH ** -0.5 * LOG2E)
        qt_sc[...] = q.T.astype(BF16)
        m_sc[...] = jnp.full_like(m_sc, NEG)
        l_sc[...] = jnp.zeros_like(l_sc)
        acc_sc[...] = jnp.zeros_like(acc_sc)

    def step(diagonal):
        kb = k_ref[...].astype(BF16)
        vt = v_ref[...].astype(BF16)
        qt = qt_sc[...]
        if diagonal:
            key = lax.broadcasted_iota(jnp.int32, (tk, tq), 0)
            qry = lax.broadcasted_iota(jnp.int32, (tk, tq), 1)
            causal = key <= qry
        scores = []
        for h in range(nh):
            ps = slice(h // 2 * LANE, (h // 2 + 1) * LANE)
            qtp = qt[ps, :]
            qth = jnp.where(head0 if h % 2 == 0 else jnp.logical_not(head0), qtp, jnp.zeros_like(qtp))
            scores.append(_dot(kb[:, ps], qth))
        for h in range(nh):
            hs = slice(h * FOX_DH, (h + 1) * FOX_DH)
            s = scores[h] - ck_ref[h]
            if diagonal:
                s = jnp.where(causal, s, NEG)
            m_prev = m_sc[h:h + 1, :]
            m_new = jnp.maximum(m_prev, jnp.max(s, axis=0, keepdims=True))
            alpha = jnp.exp2(m_prev - m_new)
            p = jnp.exp2(s - m_new)
            l_sc[h:h + 1, :] = alpha * l_sc[h:h + 1, :] + jnp.sum(p, axis=0, keepdims=True)
            m_sc[h:h + 1, :] = m_new
            acc_sc[hs, :] = alpha * acc_sc[hs, :] + _dot(vt[hs, :], p.astype(BF16))

    @pl.when(ki < qi)
    def _():
        step(False)

    @pl.when(ki == qi)
    def _():
        step(True)
        linv = jnp.concatenate(
            [jnp.broadcast_to(1.0 / l_sc[h:h + 1, :], (FOX_DH, tq)) for h in range(nh)], axis=0)
        o_ref[...] = (acc_sc[...] * linv).T


def _fox_prompt(proj, v_t, cum_c):
    b, t, _ = proj.shape
    tq = tk = min(t, 512)
    nq = t // tq
    nh = 8
    w = nh * FOX_DH
    hp = FOX_H // nh
    pairs = [(qi, ki) for qi in range(nq) for ki in range(qi + 1)]
    qi_tab = jnp.array([p_[0] for p_ in pairs], jnp.int32)
    ki_tab = jnp.array([p_[1] for p_ in pairs], jnp.int32)
    grid_spec = pltpu.PrefetchScalarGridSpec(
        num_scalar_prefetch=2,
        grid=(b, hp, len(pairs)),
        in_specs=[
            pl.BlockSpec((None, tq, w), lambda bi, p, s, qt, kt: (bi, qt[s], C_FQ // w + p)),
            pl.BlockSpec((None, tk, w), lambda bi, p, s, qt, kt: (bi, kt[s], C_FK // w + p)),
            pl.BlockSpec((None, w, tk), lambda bi, p, s, qt, kt: (bi, p, kt[s])),
            pl.BlockSpec((None, nh, tk, 1), lambda bi, p, s, qt, kt: (bi, p, kt[s], 0)),
        ],
        out_specs=pl.BlockSpec((None, tq, w), lambda bi, p, s, qt, kt: (bi, qt[s], p)),
        scratch_shapes=[
            pltpu.VMEM((w, tq), BF16),
            pltpu.VMEM((nh, tq), F32),
            pltpu.VMEM((nh, tq), F32),
            pltpu.VMEM((w, tq), F32),
        ],
    )
    return pl.pallas_call(
        functools.partial(_fox_kernel, tq=tq, tk=tk, nh=nh),
        grid_spec=grid_spec,
        out_shape=jax.ShapeDtypeStruct((b, t, FOX_H * FOX_DH), F32),
        compiler_params=_cparams(("parallel", "parallel", "arbitrary"), 40),
        name="fox_prompt",
    )(qi_tab, ki_tab, proj, proj, v_t, cum_c)


def _fox_dec_kernel(pt_ref, q_ref, kn_ref, vn_ref, lfn_ref, u_ref, pm_ref, *rest, pps, ts):
    k_refs = rest[:pps]
    v_refs = rest[pps:2 * pps]
    lf_refs = rest[2 * pps:3 * pps]
    o_ref, qbd, m_sc, l_sc, acc_sc, carry_sc = rest[3 * pps:]
    j = pl.program_id(1)
    rows = FOX_H * ts
    width = FOX_H * FOX_DH
    page = k_refs[0].shape[-1]
    assert ts & (ts - 1) == 0 and FOX_DH & (FOX_DH - 1) == 0
    rowh = lax.broadcasted_iota(jnp.int32, (rows, width), 0) >> (ts.bit_length() - 1)
    colh = lax.broadcasted_iota(jnp.int32, (rows, width), 1) >> (FOX_DH.bit_length() - 1)
    own = rowh == colh

    @pl.when(j == 0)
    def _():
        q = q_ref[...] * (FOX_DH ** -0.5 * LOG2E)
        qt = jnp.broadcast_to(q[None], (FOX_H, ts, width)).reshape(rows, width)
        qbd[...] = jnp.where(own, qt, 0.0).astype(BF16)
        m_sc[...] = jnp.full_like(m_sc, NEG)
        l_sc[...] = jnp.zeros_like(l_sc)
        acc_sc[...] = jnp.zeros_like(acc_sc)
        carry_sc[...] = jnp.zeros_like(carry_sc)

    def expand(cum):
        return jnp.broadcast_to(cum[:, None, :], (FOX_H, ts, page)).reshape(rows, page)

    def update(k_list, v_list, cum_list, valid):
        qb = qbd[...]
        s = [_dot(qb, kp.astype(BF16)) - expand(cm) * LOG2E for kp, cm in zip(k_list, cum_list)]
        s = s[0] if len(s) == 1 else jnp.concatenate(s, axis=1)
        if valid is not None:
            s = jnp.where(valid, s, NEG)
        m_prev = m_sc[...]
        m_new = jnp.maximum(m_prev, jnp.max(s, axis=-1, keepdims=True))
        alpha = jnp.exp2(m_prev - m_new)
        p = jnp.exp2(s - m_new)
        l_sc[...] = alpha * l_sc[...] + jnp.sum(p, axis=-1, keepdims=True)
        m_sc[...] = m_new
        pv = None
        for i, vp in enumerate(v_list):
            t_ = _dot_nt(p[:, i * page:(i + 1) * page].astype(BF16), vp.astype(BF16))
            pv = t_ if pv is None else pv + t_
        acc_sc[...] = alpha * acc_sc[...] + pv

    lf_all = jnp.concatenate([r[...] for r in lf_refs], axis=0) if pps > 1 else lf_refs[0][...]
    cin = _dot3_r(lf_all, u_ref[...])
    carry = carry_sc[...]
    cum = cin + jnp.concatenate([carry] * pps, axis=0) if pps > 1 else cin + carry
    if pps > 1:
        cum = cum + _dot3_l(pm_ref[...], cin)[:, page - 1:page]
    carry = cum[(pps - 1) * FOX_H:pps * FOX_H, page - 1:page]
    carry_sc[...] = carry
    update([r[...].reshape(width, page) for r in k_refs], [r[...].reshape(width, page) for r in v_refs],
           [cum[i * FOX_H:(i + 1) * FOX_H] for i in range(pps)], None)

    @pl.when(j == pl.num_programs(1) - 1)
    def _():
        cumn = _dot3_r(lfn_ref[...], u_ref[...]) + carry
        kn = _pad_rows(kn_ref[...], page).T
        vn = _pad_rows(vn_ref[...], page).T
        colk = lax.broadcasted_iota(jnp.int32, (rows, page), 1)
        trow = lax.broadcasted_iota(jnp.int32, (rows, page), 0) & (ts - 1)
        update([kn], [vn], [cumn], colk <= trow)
        accn = acc_sc[...] / l_sc[...]
        o_ref[...] = jnp.sum(jnp.where(own, accn, 0.0).reshape(FOX_H, ts, width), axis=0)


def _fox_sample(layer, proj, lfn_t, page_table, cache_k, cache_v, cache_lf_t):
    db, ts, _ = proj.shape
    npages = page_table.shape[1]
    page = cache_k.shape[-1]
    width = FOX_H * FOX_DH
    pps = 32 if npages % 32 == 0 else (16 if npages % 16 == 0 else 1)
    upper = jnp.triu(jnp.ones((page, page), F32)).astype(BF16)
    ridx = jnp.arange(pps * FOX_H)
    pmat = jnp.logical_and((ridx[:, None] % FOX_H) == (ridx[None, :] % FOX_H),
                           (ridx[None, :] // FOX_H) < (ridx[:, None] // FOX_H)).astype(BF16)
    rows = FOX_H * ts

    def page_spec(shape, p_):
        return pl.BlockSpec((None, None) + shape,
                            lambda bi, j, pt, p_=p_: (layer, pt[bi, j * pps + p_]) + (0,) * len(shape))

    in_specs = [
        pl.BlockSpec((None, ts, width), lambda bi, j, pt: (bi, 0, C_FQ // width)),
        pl.BlockSpec((None, ts, width), lambda bi, j, pt: (bi, 0, C_FK // width)),
        pl.BlockSpec((None, ts, width), lambda bi, j, pt: (bi, 0, C_FV // width)),
        pl.BlockSpec((None, FOX_H, page), lambda bi, j, pt: (bi, 0, 0)),
        pl.BlockSpec((page, page), lambda bi, j, pt: (0, 0)),
        pl.BlockSpec((pps * FOX_H, pps * FOX_H), lambda bi, j, pt: (0, 0)),
    ]
    in_specs += [page_spec((FOX_H, FOX_DH, page), p_) for p_ in range(pps)]
    in_specs += [page_spec((FOX_H, FOX_DH, page), p_) for p_ in range(pps)]
    in_specs += [page_spec((FOX_H, page), p_) for p_ in range(pps)]
    grid_spec = pltpu.PrefetchScalarGridSpec(
        num_scalar_prefetch=1,
        grid=(db, npages // pps),
        in_specs=in_specs,
        out_specs=pl.BlockSpec((None, ts, width), lambda bi, j, pt: (bi, 0, 0)),
        scratch_shapes=[
            pltpu.VMEM((rows, width), BF16),
            pltpu.VMEM((rows, 1), F32),
            pltpu.VMEM((rows, 1), F32),
            pltpu.VMEM((rows, width), F32),
            pltpu.VMEM((FOX_H, 1), F32),
        ],
    )
    args = [proj, proj, proj, lfn_t, upper, pmat] + [cache_k] * pps + [cache_v] * pps + [cache_lf_t] * pps
    return pl.pallas_call(
        functools.partial(_fox_dec_kernel, pps=pps, ts=ts),
        grid_spec=grid_spec,
        out_shape=jax.ShapeDtypeStruct((db, ts, width), F32),
        compiler_params=_cparams(("parallel", "arbitrary"), 56),
        name="fox_sample",
    )(page_table, *args)


def _rw_prep_kernel(p_ref, sh0_ref, mu_ref, w2_ref, w0_ref, g2_ref, kk_ref, ka_ref, rk_ref,
                    e_ref, bl_ref,
                    kt_o, bt_o, kq_o, rt_o, v_o, gc_o, ab_o, g_o, bon_o, carry, *, tr, tv):
    i = pl.program_id(1)
    trb = p_ref.shape[0]

    @pl.when(i == 0)
    def _():
        carry[...] = sh0_ref[...]

    p = _pad_rows(p_ref[...], tr)
    rowi = lax.broadcasted_iota(jnp.int32, (tr, 1), 0)
    prev = jnp.where(rowi == 0, carry[...], pltpu.roll(p, 1, axis=0))
    carry[...] = p_ref[pl.ds(trb - 1, 1), :]
    xm = p + (prev - p) * mu_ref[...]
    r = xm[:, 0:RW_W]
    k = xm[:, RW_W:2 * RW_W]
    v = xm[:, 2 * RW_W:3 * RW_W]
    z = xm[:, 3 * RW_W:3 * RW_W + RW_LORA]
    gl = xm[:, 3 * RW_W + RW_LORA:]
    lane = lax.broadcasted_iota(jnp.int32, (1, RW_LORA), 1)
    zz = jnp.where(lane < RW_LORA // 2, jnp.tanh(z), z)
    z_hi = zz.astype(BF16)
    z_lo = (zz - z_hi.astype(F32)).astype(BF16)
    pre = (_dot(z_hi, w2_ref[0]) + _dot(z_hi, w2_ref[1]) + _dot(z_lo, w2_ref[0])) + w0_ref[...]
    w_log = -_softplus(-pre[:, :RW_W]) - 0.5
    logdec = -jnp.exp(w_log)
    a = _sigmoid(pre[:, RW_W:])
    g = _dot(_sigmoid(gl).astype(BF16), g2_ref[...])
    e = e_ref[...]

    def head_sum(x):
        return jnp.concatenate(
            [_dot3_r(x[:, c0:c0 + LANE], e) for c0 in range(0, RW_W, LANE)], axis=1)

    kk = k * kk_ref[...]
    kk = kk / jnp.maximum(jnp.sqrt(head_sum(kk * kk)), 1e-12)
    k2 = k * (1.0 + (a - 1.0) * ka_ref[...])
    bonus = head_sum(r * k2 * rk_ref[...]) * v
    if tv < tr:
        valid = rowi < tv
        logdec = jnp.where(valid, logdec, 0.0)
        kk = jnp.where(valid, kk, 0.0)
        k2 = jnp.where(valid, k2, 0.0)
        v = jnp.where(valid, v, 0.0)
        r = jnp.where(valid, r, 0.0)
    cl = _dot3_l(bl_ref[...], logdec)
    ct = jnp.concatenate(
        [jnp.broadcast_to(cl[c0 + RW_CHUNK - 1:c0 + RW_CHUNK, :], (RW_CHUNK, RW_W))
         for c0 in range(0, tr, RW_CHUNK)], axis=0)
    e_inv = jnp.exp(-cl)
    kt = kk * jnp.exp(cl - logdec)
    bt = kk * a * e_inv
    kt_o[...] = kt
    bt_o[...] = bt
    kq_o[...] = k2 * e_inv
    rt_o[...] = r * jnp.exp(cl)
    v_o[...] = v
    gc_o[...] = jnp.exp(ct)
    g_o[...] = g
    bon_o[...] = bonus
    lane2 = lax.broadcasted_iota(jnp.int32, (1, LANE), 1)
    for c in range(tr // RW_CHUNK):
        rs = slice(c * RW_CHUNK, (c + 1) * RW_CHUNK)
        for hp in range(RW_H // 2):
            ls = slice(hp * LANE, (hp + 1) * LANE)
            x = kt[rs, ls]
            y = bt[rs, ls].astype(BF16)
            for h in range(2):
                mh = (lane2 < RW_N) if h == 0 else (lane2 >= RW_N)
                xh = jnp.where(mh, x, 0.0).astype(BF16)
                ab_o[2 * hp + h, rs, :] = _dot_nt(xh, y)


def _rw_prep(proj, shift0, wl, tv):
    b, t, _ = proj.shape
    tp = max(t, RW_CHUNK)
    tr = min(tp, 256)
    trb = min(t, tr)
    n = tp // tr
    cidx = jnp.arange(tr) // RW_CHUNK
    same = cidx[:, None] == cidx[None, :]
    blk_tri = jnp.logical_and(same, jnp.arange(tr)[None, :] <= jnp.arange(tr)[:, None]).astype(BF16)
    hidx = jnp.arange(LANE) // RW_N
    e128 = (hidx[:, None] == hidx[None, :]).astype(BF16)

    def const(shape):
        return pl.BlockSpec(shape, lambda bi, i: (0,) * len(shape))

    tok = pl.BlockSpec((None, tr, RW_W), lambda bi, i: (bi, i, 0))
    outs = [jax.ShapeDtypeStruct((b, tp, RW_W), F32)] * 6
    return pl.pallas_call(
        functools.partial(_rw_prep_kernel, tr=tr, tv=tv if t < tr else tr),
        grid=(b, n),
        in_specs=[
            pl.BlockSpec((None, trb, RW_PROJ), lambda bi, i: (bi, i, C_RW // RW_PROJ)),
            pl.BlockSpec((None, 1, RW_PROJ), lambda bi, i: (bi, 0, 0)),
            const((1, RW_PROJ)), const((2, RW_LORA, 2 * RW_W)), const((1, 2 * RW_W)),
            const((RW_GATE_R, RW_W)), const((1, RW_W)), const((1, RW_W)), const((1, RW_W)),
            const((LANE, LANE)), const((tr, tr)),
        ],
        out_specs=[tok] * 6 + [pl.BlockSpec((None, RW_H, tr, RW_N), lambda bi, i: (bi, 0, i, 0)), tok, tok],
        out_shape=outs + [jax.ShapeDtypeStruct((b, RW_H, tp, RW_N), F32)] + outs[:2],
        scratch_shapes=[pltpu.VMEM((1, RW_PROJ), F32)],
        compiler_params=_cparams(("parallel", "arbitrary"), 48),
        name="rwkv_prep",
    )(proj, shift0, wl["mu"], wl["w2a2"], wl["w0a0"], wl["g2"], wl["kk"], wl["ka"], wl["rk"],
      e128, blk_tri)


def _rw_solve_kernel(a_ref, t_ref, at_sc, tt_sc):
    c = RW_CHUNK
    per = LANE // c
    for jb in range(c * c // LANE):
        blk = a_ref[:, jb * LANE:(jb + 1) * LANE].T
        at_sc[jb * per:(jb + 1) * per] = blk.reshape(per, c, LANE)
    rows = lax.broadcasted_iota(jnp.int32, (c, LANE), 0)
    rb = 4

    gr = 16
    ng = c // gr

    def body_t(tb, carry_):
        t0 = tb * rb
        parts = tuple(tuple(jnp.where(rows[g * gr:(g + 1) * gr] == t0 + r, 1.0, 0.0) for g in range(ng))
                      for r in range(rb))
        for k in range(ng):
            live = (k + 1) * gr

            def body_u(u, prt, k=k, live=live):
                tu = tt_sc[u, 0:live, :]
                out = []
                for r in range(rb):
                    a = at_sc[t0 + r, pl.ds(u, 1), :]
                    out.append(tuple(prt[r][g] - a * tu[g * gr:(g + 1) * gr] if g <= k else prt[r][g]
                                     for g in range(ng)))
                return tuple(out)

            parts = lax.fori_loop(k * gr, jnp.clip(t0, k * gr, live), body_u, parts)
        accs = [jnp.concatenate(parts[r], axis=0) for r in range(rb)]
        done = []
        for r in range(rb):
            acc = accs[r]
            for r2 in range(r):
                acc = acc - at_sc[t0 + r, pl.ds(t0 + r2, 1), :] * done[r2]
            tt_sc[t0 + r] = acc
            done.append(acc)
        return carry_

    lax.fori_loop(0, c // rb, body_t, 0)
    for jb in range(c * c // LANE):
        blk = tt_sc[jb * per:(jb + 1) * per].reshape(LANE, LANE)
        t_ref[:, jb * LANE:(jb + 1) * LANE] = blk.T


def _rw_solve(a2d):
    nch, cc = a2d.shape
    return pl.pallas_call(
        _rw_solve_kernel,
        grid=(nch // LANE,),
        in_specs=[pl.BlockSpec((LANE, cc), lambda i: (i, 0))],
        out_specs=pl.BlockSpec((LANE, cc), lambda i: (i, 0)),
        out_shape=jax.ShapeDtypeStruct((nch, cc), F32),
        scratch_shapes=[pltpu.VMEM((RW_CHUNK, RW_CHUNK, LANE), F32)] * 2,
        compiler_params=_cparams(("parallel",), 40),
        name="rwkv_solve",
    )(a2d)


def _rw_chain_kernel(kt_ref, bt_ref, kq_ref, rt_ref, v_ref, gc_ref, tm_ref, g_ref, bon_ref,
                     lnw_ref, lnb_ref, e_ref, h0_ref, o_ref, hout_ref,
                     h_sc, gm_sc, nm_sc, qe_sc, y_sc, *, tt, to):
    i = pl.program_id(1)
    cn = RW_CHUNK
    nck = tt // cn
    npair = RW_H // 2

    @pl.when(i == 0)
    def _():
        h_sc[...] = h0_ref[...]

    lane = lax.broadcasted_iota(jnp.int32, (1, LANE), 1)
    lo = lane < RW_N
    ri = lax.broadcasted_iota(jnp.int32, (cn, LANE), 0)
    ci = lax.broadcasted_iota(jnp.int32, (cn, LANE), 1) & (RW_N - 1)
    strict = ci < ri
    incl = ci <= ri
    rd = lax.broadcasted_iota(jnp.int32, (LANE, LANE), 0)
    cd = lax.broadcasted_iota(jnp.int32, (LANE, LANE), 1)
    eye = rd == cd
    same_head = (rd >> 6) == (cd >> 6)

    def stack(x):
        return jnp.concatenate([jnp.where(lo, x, 0.0), jnp.where(lo, 0.0, x)], axis=0).astype(BF16)

    units = [(c, p) for c in range(nck) for p in range(npair)]
    cs = range(len(units))
    rss = [slice(c * cn, (c + 1) * cn) for c, _ in units]
    lss = [slice(p * LANE, (p + 1) * LANE) for _, p in units]
    kt = [kt_ref[rs, ls] for rs, ls in zip(rss, lss)]
    bt = [bt_ref[rs, ls] for rs, ls in zip(rss, lss)]
    kq = [kq_ref[rs, ls] for rs, ls in zip(rss, lss)]
    rt = [rt_ref[rs, ls] for rs, ls in zip(rss, lss)]
    v = [v_ref[rs, ls] for rs, ls in zip(rss, lss)]
    gc = [gc_ref[rs, ls] for rs, ls in zip(rss, lss)]
    ks = [stack(kq[c]) for c in cs]
    vs = [stack(v[c]) for c in cs]
    tcat = [jnp.concatenate([tm_ref[2 * p, rs, :], tm_ref[2 * p + 1, rs, :]], axis=1).astype(BF16)
            for (_, p), rs in zip(units, rss)]
    ak = [jnp.where(strict, _dot_nt(kt[c].astype(BF16), ks[c]), 0.0).astype(BF16) for c in cs]
    kd = [_dot(tcat[c], stack(kt[c])) for c in cs]
    bb = [jnp.where(incl, _dot_nt(rt[c].astype(BF16), stack(bt[c])), 0.0).astype(BF16) for c in cs]
    bk = [jnp.where(incl, _dot_nt(rt[c].astype(BF16), ks[c]), 0.0).astype(BF16) for c in cs]
    akv = [_dot(ak[c], vs[c]) for c in cs]
    vd = [_dot(tcat[c], stack(akv[c])) for c in cs]
    bh = [(bt[c] * gc[c]).astype(BF16) for c in cs]
    for c in cs:
        gm_sc[c] = (jnp.where(eye, gc[c][0:1, :], 0.0)
                    - jnp.where(same_head, _dot_tn(bh[c], kd[c].astype(BF16)), 0.0))
        qe_sc[rss[c], lss[c]] = rt[c] - _dot(bb[c], stack(kd[c]))
    for c in cs:
        kh = (kq[c] * gc[c]).astype(BF16)
        nm_sc[c] = jnp.where(same_head, _dot_tn(
            jnp.concatenate([kh, bh[c]], axis=0),
            jnp.concatenate([v[c].astype(BF16), (-vd[c]).astype(BF16)], axis=0)), 0.0)
        y_sc[rss[c], lss[c]] = _dot(jnp.concatenate([bk[c], bb[c]], axis=1),
                                    jnp.concatenate([vs[c], stack(-vd[c])], axis=0))

    hcur = [h_sc[p] for p in range(npair)]
    for u, (c, p) in enumerate(units):
        hb = hcur[p].astype(BF16)
        y_sc[rss[u], lss[u]] = y_sc[rss[u], lss[u]] + _dot(qe_sc[rss[u], lss[u]].astype(BF16), hb)
        hcur[p] = _dot(gm_sc[u].astype(BF16), hb) + nm_sc[u]
    for p in range(npair):
        h_sc[p] = hcur[p]

    e = e_ref[...]
    for p in range(npair):
        ls = slice(p * LANE, (p + 1) * LANE)
        y = y_sc[:, ls]
        mean = _dot3_r(y, e) * (1.0 / RW_N)
        yc = y - mean
        var = _dot3_r(yc * yc, e) * (1.0 / RW_N)
        out = ((yc * lax.rsqrt(var + RW_GN_EPS) * lnw_ref[:, ls] + lnb_ref[:, ls] + bon_ref[:, ls])
               * g_ref[:, ls])
        o_ref[:, ls] = out[:to]

    @pl.when(i == pl.num_programs(1) - 1)
    def _():
        for p in range(npair):
            hout_ref[p] = hcur[p]


def _rw_chain(prep, tmat, h0, lnw, lnb, t_out):
    kt, bt, kq, rt, v, gc, _, g, bon = prep
    b, tp, _ = kt.shape
    tt = min(tp, 512)
    to = min(t_out, tt)
    hp = RW_H // 2
    hidx = jnp.arange(LANE) // RW_N
    e128 = (hidx[:, None] == hidx[None, :]).astype(BF16)
    tok = pl.BlockSpec((None, tt, RW_W), lambda bi, i: (bi, i, 0))
    vec = pl.BlockSpec((1, RW_W), lambda bi, i: (0, 0))
    st = pl.BlockSpec((None, hp, LANE, LANE), lambda bi, i: (bi, 0, 0, 0))
    nunit = tt // RW_CHUNK * hp
    return pl.pallas_call(
        functools.partial(_rw_chain_kernel, tt=tt, to=to),
        grid=(b, tp // tt),
        in_specs=[tok] * 6 + [
            pl.BlockSpec((None, RW_H, tt, RW_N), lambda bi, i: (bi, 0, i, 0)),
            tok, tok, vec, vec,
            pl.BlockSpec((LANE, LANE), lambda bi, i: (0, 0)),
            st,
        ],
        out_specs=[pl.BlockSpec((None, to, RW_W), lambda bi, i: (bi, i, 0)), st],
        out_shape=[
            jax.ShapeDtypeStruct((b, t_out, RW_W), F32),
            jax.ShapeDtypeStruct((b, hp, LANE, LANE), F32),
        ],
        scratch_shapes=[
            pltpu.VMEM((hp, LANE, LANE), F32),
            pltpu.VMEM((nunit, LANE, LANE), F32),
            pltpu.VMEM((nunit, LANE, LANE), F32),
            pltpu.VMEM((tt, RW_W), F32),
            pltpu.VMEM((tt, RW_W), F32),
        ],
        compiler_params=_cparams(("parallel", "arbitrary"), 40),
        name="rwkv_chain",
    )(kt, bt, kq, rt, v, gc, tmat, g, bon, lnw, lnb, e128, h0)


def _rwkv(proj, shift0, s0, wl, t):
    b = proj.shape[0]
    prep = _rw_prep(proj, shift0, wl, t)
    ab = prep[6]
    tp = ab.shape[2]
    cc = RW_CHUNK * RW_CHUNK
    a2d = ab.reshape(-1, cc)
    nch = a2d.shape[0]
    nch_p = -(-nch // LANE) * LANE
    if nch_p != nch:
        a2d = jnp.pad(a2d, ((0, nch_p - nch), (0, 0)))
    tmat = _rw_solve(a2d)[:nch].reshape(b, RW_H, tp, RW_N)
    st = jnp.swapaxes(s0, -1, -2).reshape(b, RW_H // 2, 2, RW_N, RW_N)
    z = jnp.zeros_like(st[:, :, 0])
    h0 = jnp.concatenate([jnp.concatenate([st[:, :, 0], z], -1), jnp.concatenate([z, st[:, :, 1]], -1)], -2)
    o, hout = _rw_chain(prep, tmat, h0, wl["lnw"], wl["lnb"], t)
    s_new = jnp.stack([hout[:, :, :RW_N, :RW_N], hout[:, :, RW_N:, RW_N:]], axis=2)
    s_new = jnp.swapaxes(s_new.reshape(b, RW_H, RW_N, RW_N), -1, -2)
    return o, s_new


def _merge_kernel(x_ref, gm_ref, oa_ref, ob_ref, oc_ref, ga_ref, gb_ref, gc_ref,
                  wa_ref, wb_ref, wc_ref, wo_ref, o_ref):
    merged = (_sigmoid(ga_ref[...]) * _dot(oa_ref[...].astype(BF16), wa_ref[...])
              + _sigmoid(gb_ref[...]) * _dot(ob_ref[...].astype(BF16), wb_ref[...])
              + _sigmoid(gc_ref[...]) * _dot(oc_ref[...].astype(BF16), wc_ref[...]))
    mix = _dot(merged.astype(BF16), wo_ref[...])
    o_ref[...] = x_ref[...] + gm_ref[...] * mix


def _merge(x, g1, o_ret, o_fox, o_rw, proj, wl):
    g_, r, d = x.shape
    tm = min(r, 512)
    rm = g1.shape[1]
    w = o_ret.shape[-1]

    def tok(width, cb=0):
        return pl.BlockSpec((None, tm, width), lambda g, i, cb=cb: (g, i, cb))

    def const(shape):
        return pl.BlockSpec(shape, lambda g, i: (0, 0))

    return pl.pallas_call(
        _merge_kernel,
        grid=(g_, r // tm),
        in_specs=[
            tok(d), _row_spec(rm, tm, d), tok(w), tok(w), tok(w),
            tok(d, C_GATE // d), tok(d, C_GATE // d + 1), tok(d, C_GATE // d + 2),
            const((w, d)), const((w, d)), const((w, d)), const((d, d)),
        ],
        out_specs=tok(d),
        out_shape=jax.ShapeDtypeStruct((g_, r, d), F32),
        compiler_params=_cparams(("parallel", "parallel"), 48),
        name="merge_out",
    )(x, g1, o_ret, o_fox, o_rw, proj, proj, proj, wl["w_br_ret"], wl["w_br_fox"], wl["w_br_rw"], wl["w_out"])


def _ffn_kernel(x_ref, sc_ref, sh_ref, gm_ref, g_ref, wu_ref, wd_ref, nf_ref, o_ref, h_sc, acc_sc, *, final):
    f = pl.program_id(2)

    @pl.when(f == 0)
    def _():
        h_sc[...] = _norm_mod(x_ref[...], sc_ref[...], sh_ref[...], g_ref[...]).astype(BF16)
        acc_sc[...] = jnp.zeros_like(acc_sc)

    u = jnp.maximum(_dot(h_sc[...], wu_ref[...]), 0.0)
    acc_sc[...] += _dot((u * u).astype(BF16), wd_ref[...])

    @pl.when(f == pl.num_programs(2) - 1)
    def _():
        xn = x_ref[...] + gm_ref[...] * acc_sc[...]
        if final:
            ms = jnp.mean(xn * xn, axis=-1, keepdims=True)
            xn = xn * lax.rsqrt(ms + NORM_EPS) * nf_ref[...]
        o_ref[...] = xn


def _ffn(x, sc, sh, g2, gamma, w_up, w_down, norm_final, final):
    g_, r, d = x.shape
    dff = w_up.shape[1]
    tm = min(r, 1024)
    tf = 1024
    rm = sc.shape[1]
    return pl.pallas_call(
        functools.partial(_ffn_kernel, final=final),
        grid=(g_, r // tm, dff // tf),
        in_specs=[
            pl.BlockSpec((None, tm, d), lambda g, i, f: (g, i, 0)),
            _row_spec(rm, tm, d), _row_spec(rm, tm, d), _row_spec(rm, tm, d),
            pl.BlockSpec((1, d), lambda g, i, f: (0, 0)),
            pl.BlockSpec((d, tf), lambda g, i, f: (0, f)),
            pl.BlockSpec((tf, d), lambda g, i, f: (f, 0)),
            pl.BlockSpec((1, d), lambda g, i, f: (0, 0)),
        ],
        out_specs=pl.BlockSpec((None, tm, d), lambda g, i, f: (g, i, 0)),
        out_shape=jax.ShapeDtypeStruct((g_, r, d), F32),
        scratch_shapes=[pltpu.VMEM((tm, d), BF16), pltpu.VMEM((tm, d), F32)],
        compiler_params=_cparams(("parallel", "parallel", "arbitrary"), 48),
        name="ffn",
    )(x, sc, sh, g2, gamma, w_up, w_down, norm_final)


def _rope_tables(q0, t):
    d = RET_D
    inv = 1.0 / (ROPE_BASE ** (jnp.arange(0, d, 2, dtype=F32) / d))
    ang = (q0 + jnp.arange(t)).astype(F32)[:, None] * inv[None, :]
    cos, sin = jnp.cos(ang), jnp.sin(ang)
    return jnp.concatenate([cos, cos], -1), jnp.concatenate([-sin, sin], -1)


def _prep_layer_weights(l, w):
    w_in = w["w_in"][l]
    d = w_in.shape[0]
    ret_cols, fox_qkv = 2048, 1536
    f0 = ret_cols + fox_qkv
    rw0 = f0 + FOX_H
    g0 = rw0 + RW_PROJ
    w_t = jnp.swapaxes(w_in, 0, 1)
    packed = jnp.concatenate([
        w_t[rw0:g0], w_t[f0:rw0], jnp.zeros((2 * LANE - FOX_H, d), F32),
        w_t[g0:], w_t[ret_cols:f0], w_t[:ret_cols]], axis=0).astype(BF16)
    zl = jnp.zeros((RW_LORA // 2, RW_W), F32)
    w2a2 = jnp.concatenate([jnp.concatenate([w["rw_w2"][l], zl], 1),
                            jnp.concatenate([zl, w["rw_a2"][l]], 1)], 0)
    w2a2_hi = w2a2.astype(BF16)
    w2a2 = jnp.stack([w2a2_hi, (w2a2 - w2a2_hi.astype(F32)).astype(BF16)])
    row = lambda a: a.reshape(1, -1)
    return {
        "w_in": packed,
        "norm_mix": row(w["norm_mix"][l]), "norm_ffn": row(w["norm_ffn"][l]),
        "bf": jnp.pad(w["fox_bf"][l], (0, LANE - FOX_H)).reshape(1, LANE),
        "mu": row(w["rw_mu"][l]), "w2a2": w2a2,
        "w0a0": jnp.concatenate([w["rw_w0"][l], w["rw_a0"][l]]).reshape(1, -1),
        "g2": w["rw_g2"][l].astype(BF16),
        "kk": row(w["rw_kk"][l]), "ka": row(w["rw_ka"][l]), "rk": row(w["rw_rk"][l]),
        "lnw": row(w["rw_lnw"][l]), "lnb": row(w["rw_lnb"][l]),
        "w_br_ret": w["w_br_ret"][l].astype(BF16), "w_br_fox": w["w_br_fox"][l].astype(BF16),
        "w_br_rw": w["w_br_rw"][l].astype(BF16), "w_out": w["w_out"][l].astype(BF16),
        "w_up": w["w_up"][l].astype(BF16), "w_down": w["w_down"][l].astype(BF16),
    }


def _ret_log_gamma_table():
    lg = jnp.log1p(-jnp.exp2(-5.0 - jnp.arange(RET_H, dtype=F32)))
    return jnp.broadcast_to(lg[:, None, None], (RET_H, 1, RET_D))


def _layer(x, mod, q0, past, ret0, wkv0, shift0, wl, norm_final, final, flat):
    b, t, d = x.shape
    sh1, sc1, g1, sh2, sc2, g2 = [m[:, None, :] for m in jnp.split(mod, 6, axis=-1)]
    if flat:
        rep = lambda m: jnp.repeat(m, t, axis=1).reshape(1, b * t, d)
        sh1, sc1, g1, sh2, sc2, g2 = [rep(m) for m in (sh1, sc1, g1, sh2, sc2, g2)]
        xd = x.reshape(1, b * t, d)
    else:
        xd = x
    if past is None:
        proj_d, k_t, v_t = _norm_mm(xd, sc1, sh1, wl["norm_mix"], wl["w_in"], (C_FK, C_FV))
    else:
        proj_d, = _norm_mm(xd, sc1, sh1, wl["norm_mix"], wl["w_in"])
    proj = proj_d.reshape(b, t, PROJ_W)

    cosf, sinf = _rope_tables(q0, t)
    o_ret, r_new = _retention(proj, cosf, sinf, _ret_log_gamma_table(), ret0)

    lf, cum = _fox_gate(proj_d, wl["bf"])
    lf = lf.reshape(b, t, LANE)[:, :, :FOX_H]
    if past is None:
        o_fox = _fox_prompt(proj, v_t, cum)
    else:
        layer, page_table, cache_k, cache_v, cache_lf_t = past
        page = cache_k.shape[-1]
        lfn_t =jnp.pad(jnp.swapaxes(lf, 1, 2), ((0, 0), (0, 0), (0, page - t)))
        o_fox = _fox_sample(layer, proj, lfn_t, page_table, cache_k, cache_v, cache_lf_t)

    o_rw, s_new = _rwkv(proj, shift0[:, None, :], wkv0, wl, t)

    dn = lambda a: a.reshape(xd.shape[0], xd.shape[1], -1)
    x1 = _merge(xd, g1, dn(o_ret), dn(o_fox), dn(o_rw), proj_d, wl)
    x2 = _ffn(x1, sc2, sh2, g2, wl["norm_ffn"], wl["w_up"], wl["w_down"], norm_final, final)
    if past is None:
        fk = k_t.reshape(b, FOX_H, FOX_DH, t).transpose(0, 3, 1, 2)
        fv = v_t.reshape(b, FOX_H, FOX_DH, t).transpose(0, 3, 1, 2)
    else:
        fk = proj[:, :, C_FK:C_FV].reshape(b, t, FOX_H, FOX_DH)
        fv = proj[:, :, C_FV:C_RQ].reshape(b, t, FOX_H, FOX_DH)
    shift = proj[:, t - 1, C_RW:C_RW + RW_PROJ]
    return x2.reshape(b, t, d), (fk, fv, lf, r_new, s_new, shift)


def kernel(x_prompt, x_sample, c_prompt, c_sample, cache_k, cache_v, cache_logf, page_table, state_ret, state_wkv, state_shift, ada_w, ada_b, norm_mix, norm_ffn, w_in, fox_bf, rw_mu, rw_w0, rw_w2, rw_a0, rw_a2, rw_g2, rw_kk, rw_ka, rw_rk, rw_lnw, rw_lnb, w_br_ret, w_br_fox, w_br_rw, w_out, w_up, w_down, norm_final):
    depth = w_in.shape[0]
    b, t, d = x_prompt.shape
    db, ts, _ = x_sample.shape
    n_pool, page = cache_k.shape[1], cache_k.shape[2]
    past_len = page_table.shape[1] * page
    w = dict(w_in=w_in, norm_mix=norm_mix, norm_ffn=norm_ffn, fox_bf=fox_bf, rw_mu=rw_mu, rw_w0=rw_w0,
             rw_w2=rw_w2, rw_a0=rw_a0, rw_a2=rw_a2, rw_g2=rw_g2, rw_kk=rw_kk, rw_ka=rw_ka, rw_rk=rw_rk,
             rw_lnw=rw_lnw, rw_lnb=rw_lnb, w_br_ret=w_br_ret, w_br_fox=w_br_fox, w_br_rw=w_br_rw,
             w_out=w_out, w_up=w_up, w_down=w_down)
    rows = b + db
    rows_p = -(-rows // 8) * 8
    c_all = jnp.pad(jnp.concatenate([c_prompt, c_sample], 0), ((0, rows_p - rows), (0, 0)))
    mod = _modulation(c_all, ada_w, ada_b)
    ck = jnp.transpose(cache_k, (0, 1, 3, 4, 2))
    cv = jnp.transpose(cache_v, (0, 1, 3, 4, 2))
    clf_t = jnp.swapaxes(cache_logf, 2, 3)
    nf = norm_final.reshape(1, d)
    xp, xs = x_prompt, x_sample
    st_p, st_s = [], []
    for l in range(depth):
        wl = _prep_layer_weights(l, w)
        final = l == depth - 1
        xp, sp = _layer(xp, mod[l, :b], 0, None, jnp.zeros((b, RET_H, RET_D, RET_D), F32),
                        jnp.zeros((b, RW_H, RW_N, RW_N), F32), jnp.zeros((b, RW_PROJ), F32),
                        wl, nf, final, False)
        st_p.append(sp)
        xs, ss = _layer(xs, mod[l, b:rows], past_len, (l, page_table, ck, cv, clf_t), state_ret[l],
                        state_wkv[l], state_shift[l], wl, nf, final, True)
        st_s.append(ss)
    outs = [xp, xs]
    for st in (st_p, st_s):
        for i in range(6):
            outs.append(jnp.stack([s[i] for s in st]))
    y_p, y_s, k_p, v_p, lf_p, ret_p, wkv_p, sh_p, k_s, v_s, lf_s, ret_s, wkv_s, sh_s = outs
    return (y_p, y_s, k_p, v_p, lf_p, ret_p, wkv_p, sh_p, k_s, v_s, lf_s, ret_s, wkv_s, sh_s)
```

```python
import functools

import jax
import jax.numpy as jnp
from jax import lax
from jax.experimental import pallas as pl
from jax.experimental.pallas import tpu as pltpu

F32 = jnp.float32
BF16 = jnp.bfloat16
HIGHEST = lax.Precision.HIGHEST

RET_H, RET_D = 4, 128
FOX_H, FOX_DH = 8, 64
RW_H, RW_N = 8, 64
RW_W = RW_H * RW_N
RW_LORA = 128
RW_GATE_R = 128
RW_PROJ = 3 * RW_W + RW_LORA + RW_GATE_R
RET_GN_EPS = 1e-6
RW_GN_EPS = 64e-5
NORM_EPS = 1e-6
ROPE_BASE = 10000.0
NEG = -1e30
LOG2E = 1.4426950408889634

LANE = 128
C_RW = 0
C_F = 1792
C_GATE = 2048
C_FQ, C_FK, C_FV = 5120, 5632, 6144
C_RQ, C_RK, C_RV, C_RG = 6656, 7168, 7680, 8192
PROJ_W = 8704

RW_CHUNK = 64
RET_CHUNK = 128


def _cparams(sem, vmem_mb):
    return pltpu.CompilerParams(dimension_semantics=sem, vmem_limit_bytes=vmem_mb << 20)


def _dot(a, b):
    return jnp.dot(a, b, preferred_element_type=F32)


def _dot_nt(a, b):
    return lax.dot_general(a, b, (((1,), (1,)), ((), ())), preferred_element_type=F32)


def _dot_tn(a, b):
    return lax.dot_general(a, b, (((0,), (0,)), ((), ())), preferred_element_type=F32)


def _split3(x):
    hi = x.astype(BF16)
    r1 = x - hi.astype(F32)
    mid = r1.astype(BF16)
    lo = (r1 - mid.astype(F32)).astype(BF16)
    return hi, mid, lo


def _dot3_l(m01, x):
    hi, mid, lo = _split3(x)
    return _dot(m01, hi) + _dot(m01, mid) + _dot(m01, lo)


def _dot3_r(x, m01):
    hi, mid, lo = _split3(x)
    return _dot(hi, m01) + _dot(mid, m01) + _dot(lo, m01)


def _sigmoid(x):
    return 1.0 / (1.0 + jnp.exp(-x))


def _softplus(x):
    return jnp.maximum(x, 0.0) + jnp.log(1.0 + jnp.exp(-jnp.abs(x)))


def _pad_rows(a, rows):
    if a.shape[0] == rows:
        return a
    return jnp.concatenate([a, jnp.zeros((rows - a.shape[0],) + a.shape[1:], a.dtype)], axis=0)


def _mod_kernel(c_ref, w_ref, b_ref, o_ref):
    c = c_ref[...]
    s = c * _sigmoid(c)
    o_ref[...] = jnp.dot(s, w_ref[...], precision=HIGHEST, preferred_element_type=F32) + b_ref[...]


def _modulation(c_all, ada_w, ada_b):
    depth, d, n = ada_w.shape
    rows = c_all.shape[0]
    tn = 1536
    return pl.pallas_call(
        _mod_kernel,
        grid=(depth, n // tn),
        in_specs=[
            pl.BlockSpec((rows, d), lambda l, j: (0, 0)),
            pl.BlockSpec((None, d, tn), lambda l, j: (l, 0, j)),
            pl.BlockSpec((None, 1, tn), lambda l, j: (l, 0, j)),
        ],
        out_specs=pl.BlockSpec((None, rows, tn), lambda l, j: (l, 0, j)),
        out_shape=jax.ShapeDtypeStruct((depth, rows, n), F32),
        compiler_params=_cparams(("parallel", "parallel"), 40),
        name="ada_mod",
    )(c_all, ada_w, ada_b.reshape(depth, 1, n))


def _norm_mod(x, sc, sh, g):
    ms = jnp.mean(x * x, axis=-1, keepdims=True)
    y = x * lax.rsqrt(ms + NORM_EPS) * g
    return y * (1.0 + sc) + sh


def _norm_mm_kernel(x_ref, sc_ref, sh_ref, g_ref, w_ref, o_ref, *rest, t_blocks):
    h_ref = rest[-1]
    j = pl.program_id(2)

    @pl.when(j == 0)
    def _():
        h_ref[...] = _norm_mod(x_ref[...], sc_ref[...], sh_ref[...], g_ref[...]).astype(BF16)

    y = _dot_nt(h_ref[...], w_ref[...])
    o_ref[...] = y
    for n_, jb in enumerate(t_blocks):
        @pl.when(j == jb)
        def _(n_=n_):
            rest[n_][...] = y.T


def _row_spec(rm, tm, d):
    if rm == 1:
        return pl.BlockSpec((None, 1, d), lambda g, i, *_: (g, 0, 0))
    return pl.BlockSpec((None, tm, d), lambda g, i, *_: (g, i, 0))


def _norm_mm(x, sc, sh, gamma, w_t, t_cols=()):
    g_, r, d = x.shape
    n = w_t.shape[0]
    tm = min(r, 2048)
    tn = 512
    rm = sc.shape[1]
    out_specs = [pl.BlockSpec((None, tm, tn), lambda g, i, j: (g, i, j))]
    out_shape = [jax.ShapeDtypeStruct((g_, r, n), F32)]
    t_blocks = tuple(c0 // tn for c0 in t_cols)
    for _ in t_blocks:
        out_specs.append(pl.BlockSpec((None, tn, tm), lambda g, i, j: (g, 0, i)))
        out_shape.append(jax.ShapeDtypeStruct((g_, tn, r), F32))
    return pl.pallas_call(
        functools.partial(_norm_mm_kernel, t_blocks=t_blocks),
        grid=(g_, r // tm, n // tn),
        in_specs=[
            pl.BlockSpec((None, tm, d), lambda g, i, j: (g, i, 0)),
            _row_spec(rm, tm, d),
            _row_spec(rm, tm, d),
            pl.BlockSpec((1, d), lambda g, i, j: (0, 0)),
            pl.BlockSpec((tn, d), lambda g, i, j: (j, 0)),
        ],
        out_specs=out_specs,
        out_shape=out_shape,
        scratch_shapes=[pltpu.VMEM((tm, d), BF16)],
        compiler_params=_cparams(("parallel", "parallel", "arbitrary"), 56),
        name="norm_in_proj",
    )(x, sc, sh, gamma, w_t)


def _fox_gate_kernel(f_ref, b_ref, tri_ref, lf_ref, cum_ref, carry_ref):
    @pl.when(pl.program_id(1) == 0)
    def _():
        carry_ref[...] = jnp.zeros_like(carry_ref)

    z = f_ref[...] + b_ref[...]
    lf = -_softplus(-z)
    lf_ref[...] = lf
    cum = _dot3_l(tri_ref[...], lf) + carry_ref[...]
    tb = cum.shape[0]
    carry_ref[...] = cum[tb - 1:tb, :]
    cum2 = cum * LOG2E
    for h in range(FOX_H):
        cum_ref[h] = cum2[:, h:h + 1]


def _fox_gate(proj, bf_pad):
    g_, r, _ = proj.shape
    tb = min(r, 512)
    tri = jnp.tril(jnp.ones((tb, tb), F32)).astype(BF16)
    return pl.pallas_call(
        _fox_gate_kernel,
        grid=(g_, r // tb),
        in_specs=[
            pl.BlockSpec((None, tb, LANE), lambda g, i: (g, i, C_F // LANE)),
            pl.BlockSpec((1, LANE), lambda g, i: (0, 0)),
            pl.BlockSpec((tb, tb), lambda g, i: (0, 0)),
        ],
        out_specs=[
            pl.BlockSpec((None, tb, LANE), lambda g, i: (g, i, 0)),
            pl.BlockSpec((None, FOX_H, tb, 1), lambda g, i: (g, 0, i, 0)),
        ],
        out_shape=[jax.ShapeDtypeStruct((g_, r, LANE), F32),
                   jax.ShapeDtypeStruct((g_, FOX_H, r, 1), F32)],
        scratch_shapes=[pltpu.VMEM((1, LANE), F32)],
        compiler_params=_cparams(("parallel", "arbitrary"), 32),
        name="fox_gate",
    )(proj, bf_pad, tri)


def _ret_kernel(q_ref, k_ref, v_ref, g_ref, cos_ref, sin_ref, lg_ref, r0_ref,
                o_ref, rout_ref, r_sc, *, ce, nb):
    cp = RET_CHUNK
    c = pl.program_id(1)
    heads = range(RET_H)

    @pl.when(c == 0)
    def _():
        r_sc[...] = r0_ref[...]

    rows = max(nb * ce, cp)
    cosf = _pad_rows(cos_ref[...], rows)
    sinf = _pad_rows(sin_ref[...], rows)

    def rot(x):
        return x * cosf + pltpu.roll(x, RET_D // 2, axis=1) * sinf

    row = lax.broadcasted_iota(jnp.int32, (cp, RET_D), 0).astype(F32)
    ri = lax.broadcasted_iota(jnp.int32, (cp, cp), 0)
    ci = lax.broadcasted_iota(jnp.int32, (cp, cp), 1)
    diff = (ri - ci).astype(F32)
    lss = [slice(h * RET_D, (h + 1) * RET_D) for h in heads]
    q = [rot(_pad_rows(q_ref[:, ls], rows)).astype(BF16) for ls in lss]
    kf = [rot(_pad_rows(k_ref[:, ls], rows)) * (RET_D ** -0.5) for ls in lss]
    vb = [_pad_rows(v_ref[:, ls], rows).astype(BF16) for ls in lss]
    lgs = [lg_ref[h] for h in heads]
    qdec = [jnp.exp(lg * (row + 1.0)) for lg in lgs]
    kdec = [jnp.exp(lg * (ce - 1.0 - row)) for lg in lgs]
    dmask = [jnp.where(diff >= 0.0, jnp.exp(lg[:, :1] * jnp.maximum(diff, 0.0)), 0.0) for lg in lgs]
    gdec = [jnp.exp(lg[:, :1] * float(ce)) for lg in lgs]

    r_cur = [r_sc[h] for h in heads]
    outs = [[] for _ in heads]
    for j in range(nb):
        rs = slice(j * cp, (j + 1) * cp)
        s = [_dot_nt(q[h][rs], kf[h][rs].astype(BF16)) * dmask[h] for h in heads]
        cross = [_dot(q[h][rs], r_cur[h].astype(BF16)) * qdec[h] for h in heads]
        upd = [_dot_tn((kf[h][rs] * kdec[h]).astype(BF16), vb[h][rs]) for h in heads]
        inner = [_dot(s[h].astype(BF16), vb[h][rs]) for h in heads]
        for h in heads:
            r_cur[h] = gdec[h] * r_cur[h] + upd[h]
            outs[h].append(inner[h] + cross[h])
    for h in heads:
        r_sc[h] = r_cur[h]
        o = outs[h][0] if nb == 1 else jnp.concatenate(outs[h], axis=0)
        mu = jnp.mean(o, axis=-1, keepdims=True)
        oc = o - mu
        var = jnp.mean(oc * oc, axis=-1, keepdims=True)
        gg = _pad_rows(g_ref[:, lss[h]], rows)
        out = oc * lax.rsqrt(var + RET_GN_EPS) * (gg * _sigmoid(gg))
        o_ref[:, lss[h]] = out[:nb * ce]

    @pl.when(c == pl.num_programs(1) - 1)
    def _():
        for h in heads:
            rout_ref[h] = r_cur[h]


def _retention(proj, cosf, sinf, lg_tab, r0):
    b, t, _ = proj.shape
    ce = min(RET_CHUNK, t)
    n = t // ce
    nb = 4 if (ce == RET_CHUNK and n % 4 == 0) else 1
    tr = nb * ce

    w = RET_H * RET_D

    def col(c0):
        return pl.BlockSpec((None, tr, w), lambda bi, c, c0=c0: (bi, c, c0 // w))

    state = pl.BlockSpec((None, RET_H, RET_D, RET_D), lambda bi, c: (bi, 0, 0, 0))
    return pl.pallas_call(
        functools.partial(_ret_kernel, ce=ce, nb=nb),
        grid=(b, n // nb),
        in_specs=[
            col(C_RQ), col(C_RK), col(C_RV), col(C_RG),
            pl.BlockSpec((tr, RET_D), lambda bi, c: (c, 0)),
            pl.BlockSpec((tr, RET_D), lambda bi, c: (c, 0)),
            pl.BlockSpec((RET_H, 1, RET_D), lambda bi, c: (0, 0, 0)),
            state,
        ],
        out_specs=[pl.BlockSpec((None, tr, w), lambda bi, c: (bi, c, 0)), state],
        out_shape=[
            jax.ShapeDtypeStruct((b, t, w), F32),
            jax.ShapeDtypeStruct((b, RET_H, RET_D, RET_D), F32),
        ],
        scratch_shapes=[pltpu.VMEM((RET_H, RET_D, RET_D), F32)],
        compiler_params=_cparams(("parallel", "arbitrary"), 40),
        name="retention",
    )(proj, proj, proj, proj, cosf, sinf, lg_tab, r0)


def _fox_kernel(qi_ref, ki_ref, q_ref, k_ref, v_ref, ck_ref, o_ref, qt_sc, m_sc, l_sc, acc_sc, *, tq, tk, nh):
    qi = qi_ref[pl.program_id(2)]
    ki = ki_ref[pl.program_id(2)]
    head0 = lax.broadcasted_iota(jnp.int32, (LANE, 1), 0) < FOX_DH

    @pl.when(ki == 0)
    def _():
        q = q_ref[...] * (FOX_DH ** -0.5 * LOG2E)
        qt_sc[...] = q.T.astype(BF16)
        m_sc[...] = jnp.full_like(m_sc, NEG)
        l_sc[...] = jnp.zeros_like(l_sc)
        acc_sc[...] = jnp.zeros_like(acc_sc)

    def step(diagonal):
        kb = k_ref[...].astype(BF16)
        vt = v_ref[...].astype(BF16)
        qt = qt_sc[...]
        if diagonal:
            key = lax.broadcasted_iota(jnp.int32, (tk, tq), 0)
            qry = lax.broadcasted_iota(jnp.int32, (tk, tq), 1)
            causal = key <= qry
        scores = []
        for h in range(nh):
            ps = slice(h // 2 * LANE, (h // 2 + 1) * LANE)
            qtp = qt[ps, :]
            qth = jnp.where(head0 if h % 2 == 0 else jnp.logical_not(head0), qtp, jnp.zeros_like(qtp))
            scores.append(_dot(kb[:, ps], qth))
        for h in range(nh):
            hs = slice(h * FOX_DH, (h + 1) * FOX_DH)
            s = scores[h] - ck_ref[h]
            if diagonal:
                s = jnp.where(causal, s, NEG)
            m_prev = m_sc[h:h + 1, :]
            m_new = jnp.maximum(m_prev, jnp.max(s, axis=0, keepdims=True))
            alpha = jnp.exp2(m_prev - m_new)
            p = jnp.exp2(s - m_new)
            l_sc[h:h + 1, :] = alpha * l_sc[h:h + 1, :] + jnp.sum(p, axis=0, keepdims=True)
            m_sc[h:h + 1, :] = m_new
            acc_sc[hs, :] = alpha * acc_sc[hs, :] + _dot(vt[hs, :], p.astype(BF16))

    @pl.when(ki < qi)
    def _():
        step(False)

    @pl.when(ki == qi)
    def _():
        step(True)
        linv = jnp.concatenate(
            [jnp.broadcast_to(1.0 / l_sc[h:h + 1, :], (FOX_DH, tq)) for h in range(nh)], axis=0)
        o_ref[...] = (acc_sc[...] * linv).T


def _fox_prompt(proj, v_t, cum_c):
    b, t, _ = proj.shape
    tq = tk = min(t, 512)
    nq = t // tq
    nh = 8
    w = nh * FOX_DH
    hp = FOX_H // nh
    pairs = [(qi, ki) for qi in range(nq) for ki in range(qi + 1)]
    qi_tab = jnp.array([p_[0] for p_ in pairs], jnp.int32)
    ki_tab = jnp.array([p_[1] for p_ in pairs], jnp.int32)
    grid_spec = pltpu.PrefetchScalarGridSpec(
        num_scalar_prefetch=2,
        grid=(b, hp, len(pairs)),
        in_specs=[
            pl.BlockSpec((None, tq, w), lambda bi, p, s, qt, kt: (bi, qt[s], C_FQ // w + p)),
            pl.BlockSpec((None, tk, w), lambda bi, p, s, qt, kt: (bi, kt[s], C_FK // w + p)),
            pl.BlockSpec((None, w, tk), lambda bi, p, s, qt, kt: (bi, p, kt[s])),
            pl.BlockSpec((None, nh, tk, 1), lambda bi, p, s, qt, kt: (bi, p, kt[s], 0)),
        ],
        out_specs=pl.BlockSpec((None, tq, w), lambda bi, p, s, qt, kt: (bi, qt[s], p)),
        scratch_shapes=[
            pltpu.VMEM((w, tq), BF16),
            pltpu.VMEM((nh, tq), F32),
            pltpu.VMEM((nh, tq), F32),
            pltpu.VMEM((w, tq), F32),
        ],
    )
    return pl.pallas_call(
        functools.partial(_fox_kernel, tq=tq, tk=tk, nh=nh),
        grid_spec=grid_spec,
        out_shape=jax.ShapeDtypeStruct((b, t, FOX_H * FOX_DH), F32),
        compiler_params=_cparams(("parallel", "parallel", "arbitrary"), 40),
        name="fox_prompt",
    )(qi_tab, ki_tab, proj, proj, v_t, cum_c)


def _fox_dec_kernel(pt_ref, q_ref, kn_ref, vn_ref, lfn_ref, u_ref, pm_ref, *rest, pps, ts):
    k_refs = rest[:pps]
    v_refs = rest[pps:2 * pps]
    lf_refs = rest[2 * pps:3 * pps]
    o_ref, qbd, m_sc, l_sc, acc_sc, carry_sc = rest[3 * pps:]
    j = pl.program_id(1)
    rows = FOX_H * ts
    width = FOX_H * FOX_DH
    page = k_refs[0].shape[-1]
    assert ts & (ts - 1) == 0 and FOX_DH & (FOX_DH - 1) == 0
    rowh = lax.broadcasted_iota(jnp.int32, (rows, width), 0) >> (ts.bit_length() - 1)
    colh = lax.broadcasted_iota(jnp.int32, (rows, width), 1) >> (FOX_DH.bit_length() - 1)
    own = rowh == colh

    @pl.when(j == 0)
    def _():
        q = q_ref[...] * (FOX_DH ** -0.5 * LOG2E)
        qt = jnp.broadcast_to(q[None], (FOX_H, ts, width)).reshape(rows, width)
        qbd[...] = jnp.where(own, qt, 0.0).astype(BF16)
        m_sc[...] = jnp.full_like(m_sc, NEG)
        l_sc[...] = jnp.zeros_like(l_sc)
        acc_sc[...] = jnp.zeros_like(acc_sc)
        carry_sc[...] = jnp.zeros_like(carry_sc)

    def expand(cum):
        return jnp.broadcast_to(cum[:, None, :], (FOX_H, ts, page)).reshape(rows, page)

    def update(k_list, v_list, cum_list, valid):
        qb = qbd[...]
        s = [_dot(qb, kp.astype(BF16)) - expand(cm) * LOG2E for kp, cm in zip(k_list, cum_list)]
        s = s[0] if len(s) == 1 else jnp.concatenate(s, axis=1)
        if valid is not None:
            s = jnp.where(valid, s, NEG)
        m_prev = m_sc[...]
        m_new = jnp.maximum(m_prev, jnp.max(s, axis=-1, keepdims=True))
        alpha = jnp.exp2(m_prev - m_new)
        p = jnp.exp2(s - m_new)
        l_sc[...] = alpha * l_sc[...] + jnp.sum(p, axis=-1, keepdims=True)
        m_sc[...] = m_new
        pv = None
        for i, vp in enumerate(v_list):
            t_ = _dot_nt(p[:, i * page:(i + 1) * page].astype(BF16), vp.astype(BF16))
            pv = t_ if pv is None else pv + t_
        acc_sc[...] = alpha * acc_sc[...] + pv

    lf_all = jnp.concatenate([r[...] for r in lf_refs], axis=0) if pps > 1 else lf_refs[0][...]
    cin = _dot3_r(lf_all, u_ref[...])
    carry = carry_sc[...]
    cum = cin + jnp.concatenate([carry] * pps, axis=0) if pps > 1 else cin + carry
    if pps > 1:
        cum = cum + _dot3_l(pm_ref[...], cin)[:, page - 1:page]
    carry = cum[(pps - 1) * FOX_H:pps * FOX_H, page - 1:page]
    carry_sc[...] = carry
    update([r[...].reshape(width, page) for r in k_refs], [r[...].reshape(width, page) for r in v_refs],
           [cum[i * FOX_H:(i + 1) * FOX_H] for i in range(pps)], None)

    @pl.when(j == pl.num_programs(1) - 1)
    def _():
        cumn = _dot3_r(lfn_ref[...], u_ref[...]) + carry
        kn = _pad_rows(kn_ref[...], page).T
        vn = _pad_rows(vn_ref[...], page).T
        colk = lax.broadcasted_iota(jnp.int32, (rows, page), 1)
        trow = lax.broadcasted_iota(jnp.int32, (rows, page), 0) & (ts - 1)
        update([kn], [vn], [cumn], colk <= trow)
        accn = acc_sc[...] / l_sc[...]
        o_ref[...] = jnp.sum(jnp.where(own, accn, 0.0).reshape(FOX_H, ts, width), axis=0)


def _fox_sample(layer, proj, lfn_t, page_table, cache_k, cache_v, cache_lf_t):
    db, ts, _ = proj.shape
    npages = page_table.shape[1]
    page = cache_k.shape[-1]
    width = FOX_H * FOX_DH
    pps = 32 if npages % 32 == 0 else (16 if npages % 16 == 0 else 1)
    upper = jnp.triu(jnp.ones((page, page), F32)).astype(BF16)
    ridx = jnp.arange(pps * FOX_H)
    pmat = jnp.logical_and((ridx[:, None] % FOX_H) == (ridx[None, :] % FOX_H),
                           (ridx[None, :] // FOX_H) < (ridx[:, None] // FOX_H)).astype(BF16)
    rows = FOX_H * ts

    def page_spec(shape, p_):
        return pl.BlockSpec((None, None) + shape,
                            lambda bi, j, pt, p_=p_: (layer, pt[bi, j * pps + p_]) + (0,) * len(shape))

    in_specs = [
        pl.BlockSpec((None, ts, width), lambda bi, j, pt: (bi, 0, C_FQ // width)),
        pl.BlockSpec((None, ts, width), lambda bi, j, pt: (bi, 0, C_FK // width)),
        pl.BlockSpec((None, ts, width), lambda bi, j, pt: (bi, 0, C_FV // width)),
        pl.BlockSpec((None, FOX_H, page), lambda bi, j, pt: (bi, 0, 0)),
        pl.BlockSpec((page, page), lambda bi, j, pt: (0, 0)),
        pl.BlockSpec((pps * FOX_H, pps * FOX_H), lambda bi, j, pt: (0, 0)),
    ]
    in_specs += [page_spec((FOX_H, FOX_DH, page), p_) for p_ in range(pps)]
    in_specs += [page_spec((FOX_H, FOX_DH, page), p_) for p_ in range(pps)]
    in_specs += [page_spec((FOX_H, page), p_) for p_ in range(pps)]
    grid_spec = pltpu.PrefetchScalarGridSpec(
        num_scalar_prefetch=1,
        grid=(db, npages // pps),
        in_specs=in_specs,
        out_specs=pl.BlockSpec((None, ts, width), lambda bi, j, pt: (bi, 0, 0)),
        scratch_shapes=[
            pltpu.VMEM((rows, width), BF16),
            pltpu.VMEM((rows, 1), F32),
            pltpu.VMEM((rows, 1), F32),
            pltpu.VMEM((rows, width), F32),
            pltpu.VMEM((FOX_H, 1), F32),
        ],
    )
    args = [proj, proj, proj, lfn_t, upper, pmat] + [cache_k] * pps + [cache_v] * pps + [cache_lf_t] * pps
    return pl.pallas_call(
        functools.partial(_fox_dec_kernel, pps=pps, ts=ts),
        grid_spec=grid_spec,
        out_shape=jax.ShapeDtypeStruct((db, ts, width), F32),
        compiler_params=_cparams(("parallel", "arbitrary"), 56),
        name="fox_sample",
    )(page_table, *args)


def _rw_prep_kernel(p_ref, sh0_ref, mu_ref, w2_ref, w0_ref, g2_ref, kk_ref, ka_ref, rk_ref,
                    e_ref, bl_ref,
                    kt_o, bt_o, kq_o, rt_o, v_o, gc_o, ab_o, g_o, bon_o, carry, *, tr, tv):
    i = pl.program_id(1)
    trb = p_ref.shape[0]

    @pl.when(i == 0)
    def _():
        carry[...] = sh0_ref[...]

    p = _pad_rows(p_ref[...], tr)
    rowi = lax.broadcasted_iota(jnp.int32, (tr, 1), 0)
    prev = jnp.where(rowi == 0, carry[...], pltpu.roll(p, 1, axis=0))
    carry[...] = p_ref[pl.ds(trb - 1, 1), :]
    xm = p + (prev - p) * mu_ref[...]
    r = xm[:, 0:RW_W]
    k = xm[:, RW_W:2 * RW_W]
    v = xm[:, 2 * RW_W:3 * RW_W]
    z = xm[:, 3 * RW_W:3 * RW_W + RW_LORA]
    gl = xm[:, 3 * RW_W + RW_LORA:]
    lane = lax.broadcasted_iota(jnp.int32, (1, RW_LORA), 1)
    zz = jnp.where(lane < RW_LORA // 2, jnp.tanh(z), z)
    z_hi = zz.astype(BF16)
    z_lo = (zz - z_hi.astype(F32)).astype(BF16)
    pre = (_dot(z_hi, w2_ref[0]) + _dot(z_hi, w2_ref[1]) + _dot(z_lo, w2_ref[0])) + w0_ref[...]
    w_log = -_softplus(-pre[:, :RW_W]) - 0.5
    logdec = -jnp.exp(w_log)
    a = _sigmoid(pre[:, RW_W:])
    g = _dot(_sigmoid(gl).astype(BF16), g2_ref[...])
    e = e_ref[...]

    def head_sum(x):
        return jnp.concatenate(
            [_dot3_r(x[:, c0:c0 + LANE], e) for c0 in range(0, RW_W, LANE)], axis=1)

    kk = k * kk_ref[...]
    kk = kk / jnp.maximum(jnp.sqrt(head_sum(kk * kk)), 1e-12)
    k2 = k * (1.0 + (a - 1.0) * ka_ref[...])
    bonus = head_sum(r * k2 * rk_ref[...]) * v
    if tv < tr:
        valid = rowi < tv
        logdec = jnp.where(valid, logdec, 0.0)
        kk = jnp.where(valid, kk, 0.0)
        k2 = jnp.where(valid, k2, 0.0)
        v = jnp.where(valid, v, 0.0)
        r = jnp.where(valid, r, 0.0)
    cl = _dot3_l(bl_ref[...], logdec)
    ct = jnp.concatenate(
        [jnp.broadcast_to(cl[c0 + RW_CHUNK - 1:c0 + RW_CHUNK, :], (RW_CHUNK, RW_W))
         for c0 in range(0, tr, RW_CHUNK)], axis=0)
    e_inv = jnp.exp(-cl)
    kt = kk * jnp.exp(cl - logdec)
    bt = kk * a * e_inv
    kt_o[...] = kt
    bt_o[...] = bt
    kq_o[...] = k2 * e_inv
    rt_o[...] = r * jnp.exp(cl)
    v_o[...] = v
    gc_o[...] = jnp.exp(ct)
    g_o[...] = g
    bon_o[...] = bonus
    lane2 = lax.broadcasted_iota(jnp.int32, (1, LANE), 1)
    for c in range(tr // RW_CHUNK):
        rs = slice(c * RW_CHUNK, (c + 1) * RW_CHUNK)
        for hp in range(RW_H // 2):
            ls = slice(hp * LANE, (hp + 1) * LANE)
            x = kt[rs, ls].astype(BF16)
            y = bt[rs, ls]
            lo = lane2 < RW_N
            ys = jnp.concatenate([jnp.where(lo, y, 0.0), jnp.where(lo, 0.0, y)], axis=0).astype(BF16)
            ab_o[hp, rs, :] = _dot_nt(x, ys)


def _rw_prep(proj, shift0, wl, tv):
    b, t, _ = proj.shape
    tp = max(t, RW_CHUNK)
    tr = min(tp, 256)
    trb = min(t, tr)
    n = tp // tr
    cidx = jnp.arange(tr) // RW_CHUNK
    same = cidx[:, None] == cidx[None, :]
    blk_tri = jnp.logical_and(same, jnp.arange(tr)[None, :] <= jnp.arange(tr)[:, None]).astype(BF16)
    hidx = jnp.arange(LANE) // RW_N
    e128 = (hidx[:, None] == hidx[None, :]).astype(BF16)

    def const(shape):
        return pl.BlockSpec(shape, lambda bi, i: (0,) * len(shape))

    tok = pl.BlockSpec((None, tr, RW_W), lambda bi, i: (bi, i, 0))
    outs = [jax.ShapeDtypeStruct((b, tp, RW_W), F32)] * 6
    return pl.pallas_call(
        functools.partial(_rw_prep_kernel, tr=tr, tv=tv if t < tr else tr),
        grid=(b, n),
        in_specs=[
            pl.BlockSpec((None, trb, RW_PROJ), lambda bi, i: (bi, i, C_RW // RW_PROJ)),
            pl.BlockSpec((None, 1, RW_PROJ), lambda bi, i: (bi, 0, 0)),
            const((1, RW_PROJ)), const((2, RW_LORA, 2 * RW_W)), const((1, 2 * RW_W)),
            const((RW_GATE_R, RW_W)), const((1, RW_W)), const((1, RW_W)), const((1, RW_W)),
            const((LANE, LANE)), const((tr, tr)),
        ],
        out_specs=[tok] * 6 + [pl.BlockSpec((None, RW_H // 2, tr, LANE), lambda bi, i: (bi, 0, i, 0)), tok, tok],
        out_shape=outs + [jax.ShapeDtypeStruct((b, RW_H // 2, tp, LANE), F32)] + outs[:2],
        scratch_shapes=[pltpu.VMEM((1, RW_PROJ), F32)],
        compiler_params=_cparams(("parallel", "arbitrary"), 48),
        name="rwkv_prep",
    )(proj, shift0, wl["mu"], wl["w2a2"], wl["w0a0"], wl["g2"], wl["kk"], wl["ka"], wl["rk"],
      e128, blk_tri)


def _rw_solve_kernel(a_ref, t_ref, at_sc, tt_sc):
    c = RW_CHUNK
    for t in range(c):
        m = a_ref[:, t, :].T
        at_sc[0, t] = m[:c]
        at_sc[1, t] = m[c:]
    for hd in range(2):
        _rw_solve_one(at_sc.at[hd], tt_sc.at[hd])
    for t in range(c):
        t_ref[:, t, :] = jnp.concatenate([tt_sc[0, t], tt_sc[1, t]], axis=0).T


def _rw_solve_one(at_sc, tt_sc):
    c = RW_CHUNK
    rows = lax.broadcasted_iota(jnp.int32, (c, LANE), 0)
    rb = 4

    gr = 16
    ng = c // gr

    def body_t(tb, carry_):
        t0 = tb * rb
        parts = tuple(tuple(jnp.where(rows[g * gr:(g + 1) * gr] == t0 + r, 1.0, 0.0) for g in range(ng))
                      for r in range(rb))
        for k in range(ng):
            live = (k + 1) * gr

            def body_u(u, prt, k=k, live=live):
                tu = tt_sc[u, 0:live, :]
                out = []
                for r in range(rb):
                    a = at_sc[t0 + r, pl.ds(u, 1), :]
                    out.append(tuple(prt[r][g] - a * tu[g * gr:(g + 1) * gr] if g <= k else prt[r][g]
                                     for g in range(ng)))
                return tuple(out)

            parts = lax.fori_loop(k * gr, jnp.clip(t0, k * gr, live), body_u, parts)
        accs = [jnp.concatenate(parts[r], axis=0) for r in range(rb)]
        done = []
        for r in range(rb):
            acc = accs[r]
            for r2 in range(r):
                acc = acc - at_sc[t0 + r, pl.ds(t0 + r2, 1), :] * done[r2]
            tt_sc[t0 + r] = acc
            done.append(acc)
        return carry_

    lax.fori_loop(0, c // rb, body_t, 0)


def _rw_solve(a3):
    ncp = a3.shape[0]
    blk = pl.BlockSpec((LANE, RW_CHUNK, LANE), lambda i: (i, 0, 0))
    return pl.pallas_call(
        _rw_solve_kernel,
        grid=(ncp // LANE,),
        in_specs=[blk],
        out_specs=blk,
        out_shape=jax.ShapeDtypeStruct(a3.shape, F32),
        scratch_shapes=[pltpu.VMEM((2, RW_CHUNK, RW_CHUNK, LANE), F32)] * 2,
        compiler_params=_cparams(("parallel",), 48),
        name="rwkv_solve",
    )(a3)


def _rw_chain_kernel(kt_ref, bt_ref, kq_ref, rt_ref, v_ref, gc_ref, tm_ref, g_ref, bon_ref,
                     lnw_ref, lnb_ref, e_ref, h0_ref, o_ref, hout_ref,
                     h_sc, gm_sc, nm_sc, qe_sc, y_sc, *, tt, to):
    i = pl.program_id(1)
    cn = RW_CHUNK
    nck = tt // cn
    npair = RW_H // 2

    @pl.when(i == 0)
    def _():
        h_sc[...] = h0_ref[...]

    lane = lax.broadcasted_iota(jnp.int32, (1, LANE), 1)
    lo = lane < RW_N
    ri = lax.broadcasted_iota(jnp.int32, (cn, LANE), 0)
    ci = lax.broadcasted_iota(jnp.int32, (cn, LANE), 1) & (RW_N - 1)
    strict = ci < ri
    incl = ci <= ri
    rd = lax.broadcasted_iota(jnp.int32, (LANE, LANE), 0)
    cd = lax.broadcasted_iota(jnp.int32, (LANE, LANE), 1)
    eye = rd == cd
    same_head = (rd >> 6) == (cd >> 6)

    def stack(x):
        return jnp.concatenate([jnp.where(lo, x, 0.0), jnp.where(lo, 0.0, x)], axis=0).astype(BF16)

    units = [(c, p) for c in range(nck) for p in range(npair)]
    cs = range(len(units))
    rss = [slice(c * cn, (c + 1) * cn) for c, _ in units]
    lss = [slice(p * LANE, (p + 1) * LANE) for _, p in units]
    kt = [kt_ref[rs, ls] for rs, ls in zip(rss, lss)]
    bt = [bt_ref[rs, ls] for rs, ls in zip(rss, lss)]
    kq = [kq_ref[rs, ls] for rs, ls in zip(rss, lss)]
    rt = [rt_ref[rs, ls] for rs, ls in zip(rss, lss)]
    v = [v_ref[rs, ls] for rs, ls in zip(rss, lss)]
    gc = [gc_ref[rs, ls] for rs, ls in zip(rss, lss)]
    ks = [stack(kq[c]) for c in cs]
    vs = [stack(v[c]) for c in cs]
    tcat = [tm_ref[p, rs, :].astype(BF16) for (_, p), rs in zip(units, rss)]
    ak = [jnp.where(strict, _dot_nt(kt[c].astype(BF16), ks[c]), 0.0).astype(BF16) for c in cs]
    kd = [_dot(tcat[c], stack(kt[c])) for c in cs]
    bb = [jnp.where(incl, _dot_nt(rt[c].astype(BF16), stack(bt[c])), 0.0).astype(BF16) for c in cs]
    bk = [jnp.where(incl, _dot_nt(rt[c].astype(BF16), ks[c]), 0.0).astype(BF16) for c in cs]
    akv = [_dot(ak[c], vs[c]) for c in cs]
    vd = [_dot(tcat[c], stack(akv[c])) for c in cs]
    bh = [(bt[c] * gc[c]).astype(BF16) for c in cs]
    for c in cs:
        gm_sc[c] = (jnp.where(eye, gc[c][0:1, :], 0.0)
                    - jnp.where(same_head, _dot_tn(bh[c], kd[c].astype(BF16)), 0.0))
        qe_sc[rss[c], lss[c]] = rt[c] - _dot(bb[c], stack(kd[c]))
    for c in cs:
        kh = (kq[c] * gc[c]).astype(BF16)
        nm_sc[c] = jnp.where(same_head, _dot_tn(
            jnp.concatenate([kh, bh[c]], axis=0),
            jnp.concatenate([v[c].astype(BF16), (-vd[c]).astype(BF16)], axis=0)), 0.0)
        y_sc[rss[c], lss[c]] = _dot(jnp.concatenate([bk[c], bb[c]], axis=1),
                                    jnp.concatenate([vs[c], stack(-vd[c])], axis=0))

    hcur = [h_sc[p] for p in range(npair)]
    for u, (c, p) in enumerate(units):
        hb = hcur[p].astype(BF16)
        y_sc[rss[u], lss[u]] = y_sc[rss[u], lss[u]] + _dot(qe_sc[rss[u], lss[u]].astype(BF16), hb)
        hcur[p] = _dot(gm_sc[u].astype(BF16), hb) + nm_sc[u]
    for p in range(npair):
        h_sc[p] = hcur[p]

    e = e_ref[...]
    for p in range(npair):
        ls = slice(p * LANE, (p + 1) * LANE)
        y = y_sc[:, ls]
        mean = _dot3_r(y, e) * (1.0 / RW_N)
        yc = y - mean
        var = _dot3_r(yc * yc, e) * (1.0 / RW_N)
        out = ((yc * lax.rsqrt(var + RW_GN_EPS) * lnw_ref[:, ls] + lnb_ref[:, ls] + bon_ref[:, ls])
               * g_ref[:, ls])
        o_ref[:, ls] = out[:to]

    @pl.when(i == pl.num_programs(1) - 1)
    def _():
        for p in range(npair):
            hout_ref[p] = hcur[p]


def _rw_chain(prep, tmat, h0, lnw, lnb, t_out):
    kt, bt, kq, rt, v, gc, _, g, bon = prep
    b, tp, _ = kt.shape
    tt = min(tp, 512)
    to = min(t_out, tt)
    hp = RW_H // 2
    hidx = jnp.arange(LANE) // RW_N
    e128 = (hidx[:, None] == hidx[None, :]).astype(BF16)
    tok = pl.BlockSpec((None, tt, RW_W), lambda bi, i: (bi, i, 0))
    vec = pl.BlockSpec((1, RW_W), lambda bi, i: (0, 0))
    st = pl.BlockSpec((None, hp, LANE, LANE), lambda bi, i: (bi, 0, 0, 0))
    nunit = tt // RW_CHUNK * hp
    return pl.pallas_call(
        functools.partial(_rw_chain_kernel, tt=tt, to=to),
        grid=(b, tp // tt),
        in_specs=[tok] * 6 + [
            pl.BlockSpec((None, RW_H // 2, tt, LANE), lambda bi, i: (bi, 0, i, 0)),
            tok, tok, vec, vec,
            pl.BlockSpec((LANE, LANE), lambda bi, i: (0, 0)),
            st,
        ],
        out_specs=[pl.BlockSpec((None, to, RW_W), lambda bi, i: (bi, i, 0)), st],
        out_shape=[
            jax.ShapeDtypeStruct((b, t_out, RW_W), F32),
            jax.ShapeDtypeStruct((b, hp, LANE, LANE), F32),
        ],
        scratch_shapes=[
            pltpu.VMEM((hp, LANE, LANE), F32),
            pltpu.VMEM((nunit, LANE, LANE), F32),
            pltpu.VMEM((nunit, LANE, LANE), F32),
            pltpu.VMEM((tt, RW_W), F32),
            pltpu.VMEM((tt, RW_W), F32),
        ],
        compiler_params=_cparams(("parallel", "arbitrary"), 40),
        name="rwkv_chain",
    )(kt, bt, kq, rt, v, gc, tmat, g, bon, lnw, lnb, e128, h0)


def _rwkv(proj, shift0, s0, wl, t):
    b = proj.shape[0]
    prep = _rw_prep(proj, shift0, wl, t)
    ab = prep[6]
    tp = ab.shape[2]
    a3 = ab.reshape(-1, RW_CHUNK, LANE)
    ncp = a3.shape[0]
    ncp_p = -(-ncp // LANE) * LANE
    if ncp_p != ncp:
        a3 = jnp.pad(a3, ((0, ncp_p - ncp), (0, 0), (0, 0)))
    tmat = _rw_solve(a3)[:ncp].reshape(b, RW_H // 2, tp, LANE)
    st = jnp.swapaxes(s0, -1, -2).reshape(b, RW_H // 2, 2, RW_N, RW_N)
    z = jnp.zeros_like(st[:, :, 0])
    h0 = jnp.concatenate([jnp.concatenate([st[:, :, 0], z], -1), jnp.concatenate([z, st[:, :, 1]], -1)], -2)
    o, hout = _rw_chain(prep, tmat, h0, wl["lnw"], wl["lnb"], t)
    s_new = jnp.stack([hout[:, :, :RW_N, :RW_N], hout[:, :, RW_N:, RW_N:]], axis=2)
    s_new = jnp.swapaxes(s_new.reshape(b, RW_H, RW_N, RW_N), -1, -2)
    return o, s_new


def _merge_kernel(x_ref, gm_ref, oa_ref, ob_ref, oc_ref, ga_ref, gb_ref, gc_ref,
                  wa_ref, wb_ref, wc_ref, wo_ref, o_ref):
    merged = (_sigmoid(ga_ref[...]) * _dot(oa_ref[...].astype(BF16), wa_ref[...])
              + _sigmoid(gb_ref[...]) * _dot(ob_ref[...].astype(BF16), wb_ref[...])
              + _sigmoid(gc_ref[...]) * _dot(oc_ref[...].astype(BF16), wc_ref[...]))
    mix = _dot(merged.astype(BF16), wo_ref[...])
    o_ref[...] = x_ref[...] + gm_ref[...] * mix


def _merge(x, g1, o_ret, o_fox, o_rw, proj, wl):
    g_, r, d = x.shape
    tm = min(r, 512)
    rm = g1.shape[1]
    w = o_ret.shape[-1]

    def tok(width, cb=0):
        return pl.BlockSpec((None, tm, width), lambda g, i, cb=cb: (g, i, cb))

    def const(shape):
        return pl.BlockSpec(shape, lambda g, i: (0, 0))

    return pl.pallas_call(
        _merge_kernel,
        grid=(g_, r // tm),
        in_specs=[
            tok(d), _row_spec(rm, tm, d), tok(w), tok(w), tok(w),
            tok(d, C_GATE // d), tok(d, C_GATE // d + 1), tok(d, C_GATE // d + 2),
            const((w, d)), const((w, d)), const((w, d)), const((d, d)),
        ],
        out_specs=tok(d),
        out_shape=jax.ShapeDtypeStruct((g_, r, d), F32),
        compiler_params=_cparams(("parallel", "parallel"), 48),
        name="merge_out",
    )(x, g1, o_ret, o_fox, o_rw, proj, proj, proj, wl["w_br_ret"], wl["w_br_fox"], wl["w_br_rw"], wl["w_out"])


def _ffn_kernel(x_ref, sc_ref, sh_ref, gm_ref, g_ref, wu_ref, wd_ref, nf_ref, o_ref, h_sc, acc_sc, *, final):
    f = pl.program_id(2)

    @pl.when(f == 0)
    def _():
        h_sc[...] = _norm_mod(x_ref[...], sc_ref[...], sh_ref[...], g_ref[...]).astype(BF16)
        acc_sc[...] = jnp.zeros_like(acc_sc)

    u = jnp.maximum(_dot(h_sc[...], wu_ref[...]), 0.0)
    acc_sc[...] += _dot((u * u).astype(BF16), wd_ref[...])

    @pl.when(f == pl.num_programs(2) - 1)
    def _():
        xn = x_ref[...] + gm_ref[...] * acc_sc[...]
        if final:
            ms = jnp.mean(xn * xn, axis=-1, keepdims=True)
            xn = xn * lax.rsqrt(ms + NORM_EPS) * nf_ref[...]
        o_ref[...] = xn


def _ffn(x, sc, sh, g2, gamma, w_up, w_down, norm_final, final):
    g_, r, d = x.shape
    dff = w_up.shape[1]
    tm = min(r, 1024)
    tf = 1024
    rm = sc.shape[1]
    return pl.pallas_call(
        functools.partial(_ffn_kernel, final=final),
        grid=(g_, r // tm, dff // tf),
        in_specs=[
            pl.BlockSpec((None, tm, d), lambda g, i, f: (g, i, 0)),
            _row_spec(rm, tm, d), _row_spec(rm, tm, d), _row_spec(rm, tm, d),
            pl.BlockSpec((1, d), lambda g, i, f: (0, 0)),
            pl.BlockSpec((d, tf), lambda g, i, f: (0, f)),
            pl.BlockSpec((tf, d), lambda g, i, f: (f, 0)),
            pl.BlockSpec((1, d), lambda g, i, f: (0, 0)),
        ],
        out_specs=pl.BlockSpec((None, tm, d), lambda g, i, f: (g, i, 0)),
        out_shape=jax.ShapeDtypeStruct((g_, r, d), F32),
        scratch_shapes=[pltpu.VMEM((tm, d), BF16), pltpu.VMEM((tm, d), F32)],
        compiler_params=_cparams(("parallel", "parallel", "arbitrary"), 48),
        name="ffn",
    )(x, sc, sh, g2, gamma, w_up, w_down, norm_final)


def _rope_tables(q0, t):
    d = RET_D
    inv = 1.0 / (ROPE_BASE ** (jnp.arange(0, d, 2, dtype=F32) / d))
    ang = (q0 + jnp.arange(t)).astype(F32)[:, None] * inv[None, :]
    cos, sin = jnp.cos(ang), jnp.sin(ang)
    return jnp.concatenate([cos, cos], -1), jnp.concatenate([-sin, sin], -1)


def _prep_layer_weights(l, w):
    w_in = w["w_in"][l]
    d = w_in.shape[0]
    ret_cols, fox_qkv = 2048, 1536
    f0 = ret_cols + fox_qkv
    rw0 = f0 + FOX_H
    g0 = rw0 + RW_PROJ
    w_t = jnp.swapaxes(w_in, 0, 1)
    packed = jnp.concatenate([
        w_t[rw0:g0], w_t[f0:rw0], jnp.zeros((2 * LANE - FOX_H, d), F32),
        w_t[g0:], w_t[ret_cols:f0], w_t[:ret_cols]], axis=0).astype(BF16)
    zl = jnp.zeros((RW_LORA // 2, RW_W), F32)
    w2a2 = jnp.concatenate([jnp.concatenate([w["rw_w2"][l], zl], 1),
                            jnp.concatenate([zl, w["rw_a2"][l]], 1)], 0)
    w2a2_hi = w2a2.astype(BF16)
    w2a2 = jnp.stack([w2a2_hi, (w2a2 - w2a2_hi.astype(F32)).astype(BF16)])
    row = lambda a: a.reshape(1, -1)
    return {
        "w_in": packed,
        "norm_mix": row(w["norm_mix"][l]), "norm_ffn": row(w["norm_ffn"][l]),
        "bf": jnp.pad(w["fox_bf"][l], (0, LANE - FOX_H)).reshape(1, LANE),
        "mu": row(w["rw_mu"][l]), "w2a2": w2a2,
        "w0a0": jnp.concatenate([w["rw_w0"][l], w["rw_a0"][l]]).reshape(1, -1),
        "g2": w["rw_g2"][l].astype(BF16),
        "kk": row(w["rw_kk"][l]), "ka": row(w["rw_ka"][l]), "rk": row(w["rw_rk"][l]),
        "lnw": row(w["rw_lnw"][l]), "lnb": row(w["rw_lnb"][l]),
        "w_br_ret": w["w_br_ret"][l].astype(BF16), "w_br_fox": w["w_br_fox"][l].astype(BF16),
        "w_br_rw": w["w_br_rw"][l].astype(BF16), "w_out": w["w_out"][l].astype(BF16),
        "w_up": w["w_up"][l].astype(BF16), "w_down": w["w_down"][l].astype(BF16),
    }


def _ret_log_gamma_table():
    lg = jnp.log1p(-jnp.exp2(-5.0 - jnp.arange(RET_H, dtype=F32)))
    return jnp.broadcast_to(lg[:, None, None], (RET_H, 1, RET_D))


def _layer(x, mod, q0, past, ret0, wkv0, shift0, wl, norm_final, final, flat):
    b, t, d = x.shape
    sh1, sc1, g1, sh2, sc2, g2 = [m[:, None, :] for m in jnp.split(mod, 6, axis=-1)]
    if flat:
        rep = lambda m: jnp.repeat(m, t, axis=1).reshape(1, b * t, d)
        sh1, sc1, g1, sh2, sc2, g2 = [rep(m) for m in (sh1, sc1, g1, sh2, sc2, g2)]
        xd = x.reshape(1, b * t, d)
    else:
        xd = x
    if past is None:
        proj_d, k_t, v_t = _norm_mm(xd, sc1, sh1, wl["norm_mix"], wl["w_in"], (C_FK, C_FV))
    else:
        proj_d, = _norm_mm(xd, sc1, sh1, wl["norm_mix"], wl["w_in"])
    proj = proj_d.reshape(b, t, PROJ_W)

    cosf, sinf = _rope_tables(q0, t)
    o_ret, r_new = _retention(proj, cosf, sinf, _ret_log_gamma_table(), ret0)

    lf, cum = _fox_gate(proj_d, wl["bf"])
    lf = lf.reshape(b, t, LANE)[:, :, :FOX_H]
    if past is None:
        o_fox = _fox_prompt(proj, v_t, cum)
    else:
        layer, page_table, cache_k, cache_v, cache_lf_t = past
        page = cache_k.shape[-1]
        lfn_t =jnp.pad(jnp.swapaxes(lf, 1, 2), ((0, 0), (0, 0), (0, page - t)))
        o_fox = _fox_sample(layer, proj, lfn_t, page_table, cache_k, cache_v, cache_lf_t)

    o_rw, s_new = _rwkv(proj, shift0[:, None, :], wkv0, wl, t)

    dn = lambda a: a.reshape(xd.shape[0], xd.shape[1], -1)
    x1 = _merge(xd, g1, dn(o_ret), dn(o_fox), dn(o_rw), proj_d, wl)
    x2 = _ffn(x1, sc2, sh2, g2, wl["norm_ffn"], wl["w_up"], wl["w_down"], norm_final, final)
    if past is None:
        fk = k_t.reshape(b, FOX_H, FOX_DH, t).transpose(0, 3, 1, 2)
        fv = v_t.reshape(b, FOX_H, FOX_DH, t).transpose(0, 3, 1, 2)
    else:
        fk = proj[:, :, C_FK:C_FV].reshape(b, t, FOX_H, FOX_DH)
        fv = proj[:, :, C_FV:C_RQ].reshape(b, t, FOX_H, FOX_DH)
    shift = proj[:, t - 1, C_RW:C_RW + RW_PROJ]
    return x2.reshape(b, t, d), (fk, fv, lf, r_new, s_new, shift)


def kernel(x_prompt, x_sample, c_prompt, c_sample, cache_k, cache_v, cache_logf, page_table, state_ret, state_wkv, state_shift, ada_w, ada_b, norm_mix, norm_ffn, w_in, fox_bf, rw_mu, rw_w0, rw_w2, rw_a0, rw_a2, rw_g2, rw_kk, rw_ka, rw_rk, rw_lnw, rw_lnb, w_br_ret, w_br_fox, w_br_rw, w_out, w_up, w_down, norm_final):
    depth = w_in.shape[0]
    b, t, d = x_prompt.shape
    db, ts, _ = x_sample.shape
    n_pool, page = cache_k.shape[1], cache_k.shape[2]
    past_len = page_table.shape[1] * page
    w = dict(w_in=w_in, norm_mix=norm_mix, norm_ffn=norm_ffn, fox_bf=fox_bf, rw_mu=rw_mu, rw_w0=rw_w0,
             rw_w2=rw_w2, rw_a0=rw_a0, rw_a2=rw_a2, rw_g2=rw_g2, rw_kk=rw_kk, rw_ka=rw_ka, rw_rk=rw_rk,
             rw_lnw=rw_lnw, rw_lnb=rw_lnb, w_br_ret=w_br_ret, w_br_fox=w_br_fox, w_br_rw=w_br_rw,
             w_out=w_out, w_up=w_up, w_down=w_down)
    rows = b + db
    rows_p = -(-rows // 8) * 8
    c_all = jnp.pad(jnp.concatenate([c_prompt, c_sample], 0), ((0, rows_p - rows), (0, 0)))
    mod = _modulation(c_all, ada_w, ada_b)
    ck = jnp.transpose(cache_k, (0, 1, 3, 4, 2))
    cv = jnp.transpose(cache_v, (0, 1, 3, 4, 2))
    clf_t = jnp.swapaxes(cache_logf, 2, 3)
    nf = norm_final.reshape(1, d)
    xp, xs = x_prompt, x_sample
    st_p, st_s = [], []
    for l in range(depth):
        wl = _prep_layer_weights(l, w)
        final = l == depth - 1
        xp, sp = _layer(xp, mod[l, :b], 0, None, jnp.zeros((b, RET_H, RET_D, RET_D), F32),
                        jnp.zeros((b, RW_H, RW_N, RW_N), F32), jnp.zeros((b, RW_PROJ), F32),
                        wl, nf, final, False)
        st_p.append(sp)
        xs, ss = _layer(xs, mod[l, b:rows], past_len, (l, page_table, ck, cv, clf_t), state_ret[l],
                        state_wkv[l], state_shift[l], wl, nf, final, True)
        st_s.append(ss)
    outs = [xp, xs]
    for st in (st_p, st_s):
        for i in range(6):
            outs.append(jnp.stack([s[i] for s in st]))
    y_p, y_s, k_p, v_p, lf_p, ret_p, wkv_p, sh_p, k_s, v_s, lf_s, ret_s, wkv_s, sh_s = outs
    return (y_p, y_s, k_p, v_p, lf_p, ret_p, wkv_p, sh_p, k_s, v_s, lf_s, ret_s, wkv_s, sh_s)
```
